```python
import math
import numpy as np
import jax
import jax.numpy as jnp
from jax import lax

D_MODEL = 2048
BATCH = 2
SEQ = 4096
DEPTH = 4
DEC_BATCH = 8
DEC_SEQ = 8
PAST_LEN = 16384
PAGE_SIZE = 128

HEAD_DIM = 128
D_MIX = D_MODEL
N_MIX_HEADS = D_MIX // HEAD_DIM
A_HEADS = N_MIX_HEADS // 4
B_HEADS = (N_MIX_HEADS - A_HEADS) // 2
C_HEADS = N_MIX_HEADS - A_HEADS - B_HEADS
A_W = A_HEADS * HEAD_DIM
B_W = B_HEADS * HEAD_DIM
C_W = C_HEADS * HEAD_DIM
A_CHUNK = 128
B_CONV = 4
DN_CHUNK = 64
C_HALF = HEAD_DIM // 2
N_BUCKETS = 32
MAX_DISTANCE = 128
Q_BLOCK = 128
FFN_CONV = 3
D_FF = ((8 * D_MODEL // 3 + 255) // 256) * 256
D_IN = 2 * A_W + 4 * B_W + 2 * B_HEADS + 3 * C_W
EPS = 1e-6
NEG_INF = -1e30

kernel_name = 'hymba_gmlp_deltanet_diffattn_step'


def rmsnorm(x, w):
    xf = x.astype(jnp.float32)
    y = xf * lax.rsqrt(jnp.mean(xf * xf, axis=-1, keepdims=True) + EPS)
    return (y * w.astype(jnp.float32)).astype(x.dtype)


def layernorm(x, w, b):
    xf = x.astype(jnp.float32)
    mu = jnp.mean(xf, axis=-1, keepdims=True)
    xc = xf - mu
    y = xc * lax.rsqrt(jnp.mean(xc * xc, axis=-1, keepdims=True) + EPS)
    return (y * w.astype(jnp.float32) + b.astype(jnp.float32)).astype(x.dtype)


def l2norm(x):
    return x * lax.rsqrt(jnp.sum(x * x, axis=-1, keepdims=True) + EPS)


def causal_dwconv(x, state, w):
    width = w.shape[0]
    t = x.shape[1]
    xin = jnp.concatenate([state.astype(x.dtype), x], axis=1)
    y = sum(xin[:, j:j + t] * w[j] for j in range(width))
    return y, xin[:, xin.shape[1] - (width - 1):]


def t5_bucket(rel):
    n = jnp.maximum(rel, 0)
    max_exact = N_BUCKETS // 2
    nf = jnp.maximum(n, max_exact).astype(jnp.float32)
    large = max_exact + (jnp.log(nf / max_exact) / math.log(MAX_DISTANCE / max_exact)
                         * (N_BUCKETS - max_exact)).astype(jnp.int32)
    large = jnp.minimum(large, N_BUCKETS - 1)
    return jnp.where(n < max_exact, n, large)


def chunk_gmlp(u, v, w_s, b_s):
    bn, t, h, d = v.shape
    nc = -(-t // A_CHUNK)
    pad = nc * A_CHUNK - t
    vp = jnp.pad(v, ((0, 0), (0, pad), (0, 0), (0, 0))).reshape(bn, nc, A_CHUNK, h, d)
    causal = jnp.tril(jnp.ones((A_CHUNK, A_CHUNK), dtype=bool))
    w = jnp.where(causal[None], w_s, 0.0)
    mixed = jnp.einsum('hts,bcshd->bcthd', w, vp) + b_s.T[None, None, :, :, None]
    mixed = mixed.reshape(bn, nc * A_CHUNK, h, d)[:, :t]
    return (u * mixed).reshape(bn, t, h * d)


def gated_delta(q, k, v, g, beta, s0):
    bn, t, h, d = q.shape
    c = DN_CHUNK
    nc = -(-t // c)
    pad = nc * c - t

    def prep(x):
        x = jnp.pad(x, [(0, 0), (0, pad)] + [(0, 0)] * (x.ndim - 2))
        x = x.reshape((bn, nc, c) + x.shape[2:])
        return jnp.moveaxis(jnp.moveaxis(x, 3, 2), 1, 0)

    q, k, v, g, beta = prep(q), prep(k), prep(v), prep(g), prep(beta)
    gc = jnp.cumsum(g, axis=-1)
    idx = jnp.arange(c)
    incl = idx[:, None] >= idx[None, :]
    strict = idx[:, None] > idx[None, :]
    decay = jnp.exp(jnp.where(incl, gc[..., :, None] - gc[..., None, :], -jnp.inf))
    kb = k * beta[..., None]
    vb = v * beta[..., None]
    lmat = jnp.where(strict, jnp.einsum('...id,...jd->...ij', kb, k) * decay, 0.0)
    eye = jnp.eye(c, dtype=jnp.float32)
    rhs = jnp.concatenate([vb, kb * jnp.exp(gc)[..., None]], axis=-1)
    sol = lax.linalg.triangular_solve(eye + lmat, rhs, left_side=True, lower=True)
    u_c, w_c = sol[..., :d], sol[..., d:]
    a_intra = jnp.einsum('...id,...jd->...ij', q, k) * decay
    q_dec = q * jnp.exp(gc)[..., None]
    k_dec = k * jnp.exp(gc[..., -1:] - gc)[..., None]
    g_last = jnp.exp(gc[..., -1])

    def step(s, xs):
        q_c, k_c, uc, wc, ac, gl = xs
        v_new = uc - jnp.einsum('bhcd,bhde->bhce', wc, s)
        o = jnp.einsum('bhcd,bhde->bhce', q_c, s) + jnp.einsum('bhij,bhje->bhie', ac, v_new)
        s = s * gl[..., None, None] + jnp.einsum('bhcd,bhce->bhde', k_c, v_new)
        return s, o

    s_fin, o = lax.scan(step, s0, (q_dec, k_dec, u_c, w_c, a_intra, g_last))
    o = jnp.moveaxis(jnp.moveaxis(o, 0, 1), 2, 3).reshape(bn, nc * c, h, d)[:, :t]
    return o, s_fin


def diff_attn(q, k, v, q_pos, k_pos, lam, bias_table):
    bn, tq, h, _ = q.shape
    bq = math.gcd(tq, Q_BLOCK)
    nblk = tq // bq
    scale = C_HALF ** -0.5
    qb = jnp.moveaxis(q.reshape(bn, nblk, bq, h, HEAD_DIM), 1, 0)
    pb = q_pos.reshape(nblk, bq)
    k1, k2 = k[..., :C_HALF], k[..., C_HALF:]

    def block(args):
        qblk, pos = args
        rel = pos[:, None] - k_pos[None, :]
        bias = jnp.transpose(bias_table[t5_bucket(rel)], (2, 0, 1)).astype(jnp.float32)
        mask = rel >= 0

        def probs(qh, kh):
            s = jnp.einsum('bqhd,bkhd->bhqk', qh, kh).astype(jnp.float32) * scale + bias
            return jax.nn.softmax(jnp.where(mask, s, NEG_INF), axis=-1)

        p = probs(qblk[..., :C_HALF], k1) - lam * probs(qblk[..., C_HALF:], k2)
        return jnp.einsum('bhqk,bkhd->bqhd', p.astype(v.dtype), v)

    o = lax.map(block, (qb, pb))
    return jnp.moveaxis(o, 0, 1).reshape(bn, tq, h, HEAD_DIM)


def layer(x, lw, l, conv_b_state, delta_state, ffn_state, k_past, v_past, q_pos, k_pos, rel_bias):
    (attn_norm_w, w_in, a_ln_w, a_ln_b, a_ws, a_bs, b_conv_w, b_a_log, b_dt_bias, b_norm_w,
     c_lam_q1, c_lam_k1, c_lam_q2, c_lam_k2, c_norm_w, w_out, ffn_norm_w, w_gate, w_up,
     ffn_conv_w, ffn_conv_b, w_down) = lw
    f32 = jnp.float32
    bn, t, _ = x.shape
    hn = rmsnorm(x, attn_norm_w)
    proj = hn @ w_in
    offsets = np.cumsum([A_W, A_W, 3 * B_W, B_W, B_HEADS, B_HEADS, C_W, C_W]).tolist()
    a_u, a_v, b_qkv, b_z, b_beta, b_alpha, c_q, c_k, c_v = jnp.split(proj, offsets, axis=-1)

    def heads(z):
        return z.reshape(bn, t, -1, HEAD_DIM)

    u_a = heads(jax.nn.gelu(a_u))
    v_a = layernorm(heads(jax.nn.gelu(a_v)), a_ln_w, a_ln_b)
    out_a = chunk_gmlp(u_a, v_a, a_ws, a_bs)

    qkv, new_conv = causal_dwconv(b_qkv, conv_b_state, b_conv_w)
    qkv = jax.nn.silu(qkv).astype(f32)
    bq_, bk_, bv_ = jnp.split(qkv, 3, axis=-1)
    bq_ = l2norm(heads(bq_)) * HEAD_DIM ** -0.5
    bk_ = l2norm(heads(bk_))
    bv_ = heads(bv_)
    beta = jax.nn.sigmoid(b_beta.astype(f32))
    g = -jnp.exp(b_a_log.astype(f32)) * jax.nn.softplus(b_alpha.astype(f32) + b_dt_bias.astype(f32))
    o_b, new_s = gated_delta(bq_, bk_, bv_, g, beta, delta_state.astype(f32))
    out_b = (rmsnorm(o_b, b_norm_w) * jax.nn.silu(heads(b_z).astype(f32))).reshape(bn, t, B_W)

    cq, ck, cv = heads(c_q), heads(c_k), heads(c_v)
    keys = ck if k_past is None else jnp.concatenate([k_past.astype(ck.dtype), ck], axis=1)
    vals = cv if v_past is None else jnp.concatenate([v_past.astype(cv.dtype), cv], axis=1)
    lam_init = 0.8 - 0.6 * math.exp(-0.3 * l)
    lam = (jnp.exp(jnp.sum(c_lam_q1.astype(f32) * c_lam_k1.astype(f32)))
           - jnp.exp(jnp.sum(c_lam_q2.astype(f32) * c_lam_k2.astype(f32))) + lam_init)
    o_c = diff_attn(cq, keys, vals, q_pos, k_pos, lam, rel_bias)
    out_c = (rmsnorm(o_c, c_norm_w) * (1.0 - lam_init)).reshape(bn, t, C_W)

    mix = jnp.concatenate([out_a.astype(x.dtype), out_b.astype(x.dtype), out_c.astype(x.dtype)], axis=-1)
    x = x + mix @ w_out

    hn2 = rmsnorm(x, ffn_norm_w)
    gate, new_ffn = causal_dwconv(hn2 @ w_gate, ffn_state, ffn_conv_w)
    hid = jax.nn.gelu(gate + ffn_conv_b) * (hn2 @ w_up)
    x = x + hid @ w_down
    return x, (ck, cv, new_conv, new_s, new_ffn, v_a)


def setup_inputs(seed: int = 0) -> dict:
    key = jax.random.key(seed)
    ks = jax.random.split(key, 40)
    f32 = jnp.float32

    def nrm(k, shape, scale):
        return jax.random.normal(k, shape, f32) * scale

    def gain(k, shape):
        return 1.0 + 0.05 * jax.random.normal(k, shape, f32)

    n_pages = PAST_LEN // PAGE_SIZE
    n_used = DEC_BATCH * n_pages
    n_pool = n_used + n_used // 4
    page_table = jax.random.permutation(ks[0], n_pool)[:n_used].reshape(DEC_BATCH, n_pages).astype(jnp.int32)
    dt = jax.random.uniform(ks[13], (DEPTH, B_HEADS), f32, minval=1e-3, maxval=0.1)
    return {
        'x_prompt': nrm(ks[1], (BATCH, SEQ, D_MODEL), 1.0),
        'x_sample': nrm(ks[2], (DEC_BATCH, DEC_SEQ, D_MODEL), 1.0),
        'cache_k': nrm(ks[3], (DEPTH, n_pool, PAGE_SIZE, C_HEADS, HEAD_DIM), 1.0),
        'cache_v': nrm(ks[4], (DEPTH, n_pool, PAGE_SIZE, C_HEADS, HEAD_DIM), 1.0),
        'state_conv_b': nrm(ks[5], (DEPTH, DEC_BATCH, B_CONV - 1, 3 * B_W), 1.0),
        'state_delta': nrm(ks[6], (DEPTH, DEC_BATCH, B_HEADS, HEAD_DIM, HEAD_DIM), 0.1),
        'state_ffn_conv': nrm(ks[7], (DEPTH, DEC_BATCH, FFN_CONV - 1, D_FF), 1.0),
        'page_table': page_table,
        'attn_norm_w': gain(ks[8], (DEPTH, D_MODEL)),
        'w_in': nrm(ks[9], (DEPTH, D_MODEL, D_IN), D_MODEL ** -0.5),
        'a_ln_w': gain(ks[10], (DEPTH, A_HEADS, HEAD_DIM)),
        'a_ln_b': nrm(ks[11], (DEPTH, A_HEADS, HEAD_DIM), 0.02),
        'a_ws': nrm(ks[12], (DEPTH, A_HEADS, A_CHUNK, A_CHUNK), A_CHUNK ** -0.5),
        'a_bs': gain(ks[14], (DEPTH, A_HEADS, A_CHUNK)),
        'b_conv_w': nrm(ks[15], (DEPTH, B_CONV, 3 * B_W), B_CONV ** -0.5),
        'b_a_log': jnp.log(jax.random.uniform(ks[16], (DEPTH, B_HEADS), f32, minval=1.0, maxval=16.0)),
        'b_dt_bias': dt + jnp.log(-jnp.expm1(-dt)),
        'b_norm_w': gain(ks[17], (DEPTH, HEAD_DIM)),
        'c_lam_q1': nrm(ks[18], (DEPTH, C_HALF), 0.1),
        'c_lam_k1': nrm(ks[19], (DEPTH, C_HALF), 0.1),
        'c_lam_q2': nrm(ks[20], (DEPTH, C_HALF), 0.1),
        'c_lam_k2': nrm(ks[21], (DEPTH, C_HALF), 0.1),
        'c_norm_w': gain(ks[22], (DEPTH, HEAD_DIM)),
        'rel_bias': nrm(ks[23], (N_BUCKETS, C_HEADS), 0.5),
        'w_out': nrm(ks[24], (DEPTH, D_MIX, D_MODEL), D_MIX ** -0.5),
        'ffn_norm_w': gain(ks[25], (DEPTH, D_MODEL)),
        'w_gate': nrm(ks[26], (DEPTH, D_MODEL, D_FF), D_MODEL ** -0.5),
        'w_up': nrm(ks[27], (DEPTH, D_MODEL, D_FF), D_MODEL ** -0.5),
        'ffn_conv_w': nrm(ks[28], (DEPTH, FFN_CONV, D_FF), FFN_CONV ** -0.5),
        'ffn_conv_b': nrm(ks[29], (DEPTH, D_FF), 0.02),
        'w_down': nrm(ks[30], (DEPTH, D_FF, D_MODEL), D_FF ** -0.5),
        'final_norm_w': gain(ks[31], (D_MODEL,)),
    }


def reference(x_prompt, x_sample, cache_k, cache_v, state_conv_b, state_delta, state_ffn_conv, page_table,
              attn_norm_w, w_in, a_ln_w, a_ln_b, a_ws, a_bs, b_conv_w, b_a_log, b_dt_bias, b_norm_w,
              c_lam_q1, c_lam_k1, c_lam_q2, c_lam_k2, c_norm_w, rel_bias, w_out, ffn_norm_w,
              w_gate, w_up, ffn_conv_w, ffn_conv_b, w_down, final_norm_w):
    n_pages = PAST_LEN // PAGE_SIZE
    q_pos_p = jnp.arange(SEQ, dtype=jnp.int32)
    q_pos_s = PAST_LEN + jnp.arange(DEC_SEQ, dtype=jnp.int32)
    k_pos_s = jnp.arange(PAST_LEN + DEC_SEQ, dtype=jnp.int32)
    conv0 = jnp.zeros((BATCH, B_CONV - 1, 3 * B_W), x_prompt.dtype)
    delta0 = jnp.zeros((BATCH, B_HEADS, HEAD_DIM, HEAD_DIM), jnp.float32)
    ffn0 = jnp.zeros((BATCH, FFN_CONV - 1, D_FF), x_prompt.dtype)
    xp, xs = x_prompt, x_sample
    p_k, p_v, p_conv, p_delta, p_ffn = [], [], [], [], []
    s_k, s_v, s_conv, s_delta, s_ffn, s_av = [], [], [], [], [], []
    for l in range(DEPTH):
        lw = (attn_norm_w[l], w_in[l], a_ln_w[l], a_ln_b[l], a_ws[l], a_bs[l], b_conv_w[l], b_a_log[l],
              b_dt_bias[l], b_norm_w[l], c_lam_q1[l], c_lam_k1[l], c_lam_q2[l], c_lam_k2[l], c_norm_w[l],
              w_out[l], ffn_norm_w[l], w_gate[l], w_up[l], ffn_conv_w[l], ffn_conv_b[l], w_down[l])
        xp, (k_, v_, cb_, dl_, ff_, _) = layer(xp, lw, l, conv0, delta0, ffn0, None, None,
                                                q_pos_p, q_pos_p, rel_bias)
        p_k.append(k_); p_v.append(v_); p_conv.append(cb_); p_delta.append(dl_); p_ffn.append(ff_)
        k_past = cache_k[l][page_table].reshape(DEC_BATCH, n_pages * PAGE_SIZE, C_HEADS, HEAD_DIM)
        v_past = cache_v[l][page_table].reshape(DEC_BATCH, n_pages * PAGE_SIZE, C_HEADS, HEAD_DIM)
        xs, (k_, v_, cb_, dl_, ff_, av_) = layer(xs, lw, l, state_conv_b[l], state_delta[l], state_ffn_conv[l],
                                                  k_past, v_past, q_pos_s, k_pos_s, rel_bias)
        s_k.append(k_); s_v.append(v_); s_conv.append(cb_); s_delta.append(dl_); s_ffn.append(ff_)
        s_av.append(av_)
    y_prompt = rmsnorm(xp, final_norm_w)
    y_sample = rmsnorm(xs, final_norm_w)
    return (y_prompt, y_sample,
            jnp.stack(p_k), jnp.stack(p_v), jnp.stack(p_conv), jnp.stack(p_delta), jnp.stack(p_ffn),
            jnp.stack(s_k), jnp.stack(s_v), jnp.stack(s_conv), jnp.stack(s_delta), jnp.stack(s_ffn),
            jnp.stack(s_av))
```

```python
import functools
import math

import numpy as np
import jax
import jax.numpy as jnp
from jax import lax
from jax.experimental import pallas as pl
from jax.experimental.pallas import tpu as pltpu

F32 = jnp.float32
BF16 = jnp.bfloat16
HI = lax.Precision.HIGHEST

HEAD_DIM = 128
SUBLANES = 8
A_HEADS, B_HEADS, C_HEADS = 4, 6, 6
A_W, B_W, C_W = A_HEADS * HEAD_DIM, B_HEADS * HEAD_DIM, C_HEADS * HEAD_DIM
A_CHUNK = 128
B_CONV = 4
DN_CHUNK = 64
DN_PAIR = 2 * DN_CHUNK
C_HALF = HEAD_DIM // 2
N_BUCKETS = 32
MAX_DISTANCE = 128
FFN_CONV = 3
EPS = 1e-6
NEG_INF = -1e30
VMEM_LIMIT = 56 * 1024 * 1024
TM_IN, TM_OUT, TM_UP, TM_DOWN, TB_DELTA, TQ = 1024, 512, 1024, 1024, 256, 256

COL_BQ, COL_BK, COL_BV, COL_BZ, COL_CQ = 0, B_W, 2 * B_W, 3 * B_W, 4 * B_W
COL_BA = COL_CQ + C_W
COL_AU = COL_BA + 2 * HEAD_DIM
COL_AV = COL_AU + A_W
COL_CK = COL_AV + A_W
COL_CV = COL_CK + C_W
N_MAIN = COL_CV + C_W


def _t5_bucket_starts():
    n = np.arange(4 * MAX_DISTANCE)
    max_exact = N_BUCKETS // 2
    nf = np.maximum(n, max_exact).astype(np.float64)
    large = max_exact + (np.log(nf / max_exact) / math.log(MAX_DISTANCE / max_exact)
                         * (N_BUCKETS - max_exact)).astype(np.int64)
    bucket = np.where(n < max_exact, n, np.minimum(large, N_BUCKETS - 1))
    return [int(n[bucket == b].min()) for b in range(N_BUCKETS)]


BUCKET_START = _t5_bucket_starts()
FAR_DISTANCE = BUCKET_START[-1]


def _cparams(*sem):
    return pltpu.CompilerParams(dimension_semantics=sem, vmem_limit_bytes=VMEM_LIMIT)


def _gelu(x):
    return 0.5 * x * (1.0 + jnp.tanh(math.sqrt(2.0 / math.pi) * (x + 0.044715 * (x * x * x))))


def _sigmoid(x):
    return 1.0 / (1.0 + jnp.exp(-x))


def _silu(x):
    return x * _sigmoid(x)


def _softplus(x):
    return jnp.maximum(x, 0.0) + jnp.log(1.0 + jnp.exp(-jnp.abs(x)))


def _rms(x, w):
    return x * lax.rsqrt(jnp.mean(x * x, axis=-1, keepdims=True) + EPS) * w


def _dot(a, b, **kw):
    return jnp.dot(a, b, preferred_element_type=F32, **kw)


def _dot_nt(a, b, **kw):
    return lax.dot_general(a, b, (((1,), (1,)), ((), ())), preferred_element_type=F32, **kw)


def _rmsnorm_kernel(x_ref, w_ref, o_ref):
    o_ref[...] = _rms(x_ref[...], w_ref[...]).astype(o_ref.dtype)


def _rmsnorm(x, w, tm):
    m, d = x.shape
    return pl.pallas_call(
        _rmsnorm_kernel, grid=(m // tm,),
        in_specs=[pl.BlockSpec((tm, d), lambda i: (i, 0)), pl.BlockSpec((1, d), lambda i: (0, 0))],
        out_specs=pl.BlockSpec((tm, d), lambda i: (i, 0)),
        out_shape=jax.ShapeDtypeStruct((m, d), F32),
        compiler_params=_cparams("parallel"), name="final_rmsnorm")(x, w.reshape(1, d))


def _inproj_kernel(x_ref, nw_ref, w_ref, o_ref, hn_s):
    @pl.when(pl.program_id(1) == 0)
    def _():
        hn_s[...] = _rms(x_ref[...], nw_ref[...]).astype(BF16)

    o_ref[...] = _dot(hn_s[...], w_ref[...])


def _inproj(x, nw, w, l, tm, tn):
    m, d = x.shape
    n = w.shape[-1]
    return pl.pallas_call(
        _inproj_kernel, grid=(m // tm, n // tn),
        in_specs=[pl.BlockSpec((tm, d), lambda i, j: (i, 0)),
                  pl.BlockSpec((None, 1, d), lambda i, j: (l, 0, 0)),
                  pl.BlockSpec((None, d, tn), lambda i, j: (l, 0, j))],
        out_specs=pl.BlockSpec((tm, tn), lambda i, j: (i, j)),
        out_shape=jax.ShapeDtypeStruct((m, n), F32),
        scratch_shapes=[pltpu.VMEM((tm, d), BF16)],
        compiler_params=_cparams("parallel", "arbitrary"), name="inproj")(x, nw, w)


def _gmlp_kernel(u_ref, v_ref, lnw_ref, lnb_ref, w_ref, bcol_ref, *out_refs, chunk, rows, causal_mask):
    o_ref = out_refs[0]
    nchunk = max(rows // chunk, 1)
    live = min(rows, chunk)
    if causal_mask:
        ii = lax.broadcasted_iota(jnp.int32, (chunk, chunk), 0)
        jj = lax.broadcasted_iota(jnp.int32, (chunk, chunk), 1)
        keep = ii >= jj
    for h in range(A_HEADS):
        sl = slice(h * HEAD_DIM, (h + 1) * HEAD_DIM)
        u = _gelu(u_ref[:, sl])
        v = _gelu(v_ref[:, sl])
        mu = jnp.mean(v, axis=-1, keepdims=True)
        vc = v - mu
        vn = vc * lax.rsqrt(jnp.mean(vc * vc, axis=-1, keepdims=True) + EPS) * lnw_ref[:, sl] + lnb_ref[:, sl]
        if len(out_refs) > 1:
            out_refs[1][:, sl] = vn
        w = w_ref[h]
        if causal_mask:
            w = jnp.where(keep, w, 0.0)
        wb = w.astype(BF16)
        vb = vn.astype(BF16)
        bcol = bcol_ref[0:live, h:h + 1]
        if live < chunk:
            vb = jnp.concatenate([vb, jnp.zeros((chunk - live, HEAD_DIM), BF16)], axis=0)
        for c in range(nchunk):
            rs = slice(c * live, (c + 1) * live)
            mixed = _dot(wb, vb[c * chunk:(c + 1) * chunk])[0:live] + bcol
            o_ref[rs, sl] = (u[rs] * mixed).astype(o_ref.dtype)


def _gmlp(main, lnw, lnb, wmix, bcol, l, *, rows, chunk, causal_mask, want_v):
    m = main.shape[0]
    live = min(rows, chunk)
    cu, cv = COL_AU // A_W, COL_AV // A_W
    out_shape = [jax.ShapeDtypeStruct((m, A_W), BF16)]
    out_specs = [pl.BlockSpec((rows, A_W), lambda i: (i, 0))]
    if want_v:
        out_shape.append(jax.ShapeDtypeStruct((m, A_W), F32))
        out_specs.append(pl.BlockSpec((rows, A_W), lambda i: (i, 0)))
    return pl.pallas_call(
        functools.partial(_gmlp_kernel, chunk=chunk, rows=rows, causal_mask=causal_mask),
        grid=(m // rows,),
        in_specs=[pl.BlockSpec((rows, A_W), lambda i: (i, cu)),
                  pl.BlockSpec((rows, A_W), lambda i: (i, cv)),
                  pl.BlockSpec((None, 1, A_W), lambda i: (l, 0, 0)),
                  pl.BlockSpec((None, 1, A_W), lambda i: (l, 0, 0)),
                  pl.BlockSpec((None, A_HEADS, chunk, chunk), lambda i: (l, 0, 0, 0)),
                  pl.BlockSpec((None, live, A_HEADS), lambda i: (l, 0, 0))],
        out_specs=out_specs, out_shape=out_shape,
        compiler_params=_cparams("parallel"), name="gmlp")(main, main, lnw, lnb, wmix, bcol)


def _delta_kernel(q_ref, k_ref, v_ref, z_ref, ba_ref, cs_ref, cw_ref, alog_ref, dtb_ref, nw_ref, s0_ref,
                  o_ref, s_ref, buf_s, qs_s, ks_s, vs_s, g_s, bt_s, *, tb, t_valid, t_total):
    t = pl.program_id(1)
    head = SUBLANES

    @pl.when(t == 0)
    def _():
        s_ref[...] = s0_ref[...]
        for j in range(3):
            buf_s[j, 0:head, :] = cs_ref[:, j * B_W:(j + 1) * B_W]

    if t_valid < t_total:
        row = t * tb + lax.broadcasted_iota(jnp.int32, (tb, 1), 0)
        live = row < t_valid
    else:
        live = None

    def keep(x):
        return x if live is None else jnp.where(live, x, 0.0)

    for j, (src, dst) in enumerate(((q_ref, qs_s), (k_ref, ks_s), (v_ref, vs_s))):
        x = src[...]
        buf_s[j, head:head + tb, :] = x
        w = cw_ref[:, j * B_W:(j + 1) * B_W]
        y = w[B_CONV - 1:B_CONV] * x
        for tap in range(B_CONV - 1):
            off = head - (B_CONV - 1) + tap
            y = y + w[tap:tap + 1] * buf_s[j, off:off + tb, :]
        buf_s[j, 0:head, :] = buf_s[j, tb:tb + head, :]
        y = keep(_silu(y))
        for h in range(B_HEADS):
            sl = slice(h * HEAD_DIM, (h + 1) * HEAD_DIM)
            seg = y[:, sl]
            if j == 0:
                seg = seg * lax.rsqrt(jnp.sum(seg * seg, axis=-1, keepdims=True) + EPS) * (HEAD_DIM ** -0.5)
            elif j == 1:
                seg = seg * lax.rsqrt(jnp.sum(seg * seg, axis=-1, keepdims=True) + EPS)
            dst[h] = seg

    ba = ba_ref[...]
    bt_s[...] = keep(_sigmoid(ba))
    g_s[...] = keep(-jnp.exp(alog_ref[...]) * _softplus(ba + dtb_ref[...]))

    n = DN_PAIR
    ii = lax.broadcasted_iota(jnp.int32, (n, n), 0)
    jj = lax.broadcasted_iota(jnp.int32, (n, n), 1)
    same = (ii >= DN_CHUNK) == (jj >= DN_CHUNK)
    incl = same & (ii >= jj)
    strict = same & (ii > jj)
    tri = jnp.where(incl, 1.0, 0.0)
    eye = jnp.where(ii == jj, 1.0, 0.0)
    rowi = lax.broadcasted_iota(jnp.int32, (n, 1), 0)
    lane = lax.broadcasted_iota(jnp.int32, (n, n), 1)
    zeros_half = jnp.zeros((DN_CHUNK, HEAD_DIM), F32)

    def pair(p, carry):
        r0 = pl.multiple_of(p * n, n)
        gc = _dot(tri, g_s[pl.ds(r0, n), :], precision=HI)
        gct = gc.T
        bt = bt_s[pl.ds(r0, n), :]
        for h in range(B_HEADS):
            sl = slice(h * HEAD_DIM, (h + 1) * HEAD_DIM)
            q = qs_s[h, pl.ds(r0, n), :]
            k = ks_s[h, pl.ds(r0, n), :]
            v = vs_s[h, pl.ds(r0, n), :]
            gcol = gc[:, B_HEADS + h:B_HEADS + h + 1]
            grow = gct[B_HEADS + h:B_HEADS + h + 1, :]
            beta = bt[:, h:h + 1]
            decay = jnp.exp(jnp.where(incl, gcol - grow, NEG_INF))
            kb = k * beta
            vb = v * beta
            lmat = jnp.where(strict, _dot_nt(kb, k, precision=HI) * decay, 0.0)
            a = -lmat
            inv = eye + a
            for _ in range(int(math.log2(DN_CHUNK)) - 1):
                a = _dot(a, a, precision=HI)
                inv = inv + _dot(inv, a, precision=HI)
            egc = jnp.exp(gcol)
            u = _dot(inv, vb, precision=HI)
            w = _dot(inv, kb * egc, precision=HI)
            a_intra = _dot_nt(q, k, precision=HI) * decay
            q_dec = q * egc
            glast = jnp.where(rowi < DN_CHUNK, gc[DN_CHUNK - 1:DN_CHUNK, B_HEADS + h:B_HEADS + h + 1],
                              gc[n - 1:n, B_HEADS + h:B_HEADS + h + 1])
            kdt = (k * jnp.exp(glast - gcol)).T
            s = s_ref[h]
            outs = []
            for c in range(2):
                rs = slice(c * DN_CHUNK, (c + 1) * DN_CHUNK)
                v_new = u[rs] - _dot(w[rs], s, precision=HI)
                v_pad = jnp.concatenate([v_new, zeros_half] if c == 0 else [zeros_half, v_new], axis=0)
                outs.append(_dot(q_dec[rs], s, precision=HI) + _dot(a_intra[rs], v_pad, precision=HI))
                in_chunk = (lane < DN_CHUNK) if c == 0 else (lane >= DN_CHUNK)
                gl = jnp.exp(gc[(c + 1) * DN_CHUNK - 1:(c + 1) * DN_CHUNK, B_HEADS + h:B_HEADS + h + 1])
                s = s * gl + _dot(jnp.where(in_chunk, kdt, 0.0), v_pad, precision=HI)
            s_ref[h] = s
            o = jnp.concatenate(outs, axis=0)
            z = z_ref[pl.ds(r0, n), sl]
            o_ref[pl.ds(r0, n), sl] = (_rms(o, nw_ref[...]) * _silu(z)).astype(o_ref.dtype)
        return carry

    lax.fori_loop(0, tb // n, pair, 0)


def _delta(main, conv_state, conv_w, alog, dtb, nw, s0, l, *, nb, t_total, t_valid, tb):
    m = main.shape[0]
    nt = t_total // tb
    cq, ck, cv, cz, cba = COL_BQ // B_W, COL_BK // B_W, COL_BV // B_W, COL_BZ // B_W, COL_BA // HEAD_DIM
    row = lambda b, t: b * nt + t
    return pl.pallas_call(
        functools.partial(_delta_kernel, tb=tb, t_valid=t_valid, t_total=t_total),
        grid=(nb, nt),
        in_specs=[pl.BlockSpec((tb, B_W), lambda b, t: (row(b, t), cq)),
                  pl.BlockSpec((tb, B_W), lambda b, t: (row(b, t), ck)),
                  pl.BlockSpec((tb, B_W), lambda b, t: (row(b, t), cv)),
                  pl.BlockSpec((tb, B_W), lambda b, t: (row(b, t), cz)),
                  pl.BlockSpec((tb, HEAD_DIM), lambda b, t: (row(b, t), cba)),
                  pl.BlockSpec((None, SUBLANES, 3 * B_W), lambda b, t: (b, 0, 0)),
                  pl.BlockSpec((None, B_CONV, 3 * B_W), lambda b, t: (l, 0, 0)),
                  pl.BlockSpec((None, 1, HEAD_DIM), lambda b, t: (l, 0, 0)),
                  pl.BlockSpec((None, 1, HEAD_DIM), lambda b, t: (l, 0, 0)),
                  pl.BlockSpec((None, 1, HEAD_DIM), lambda b, t: (l, 0, 0)),
                  pl.BlockSpec((None, B_HEADS, HEAD_DIM, HEAD_DIM), lambda b, t: (b, 0, 0, 0))],
        out_specs=[pl.BlockSpec((tb, B_W), lambda b, t: (row(b, t), 0)),
                   pl.BlockSpec((None, B_HEADS, HEAD_DIM, HEAD_DIM), lambda b, t: (b, 0, 0, 0))],
        out_shape=[jax.ShapeDtypeStruct((m, B_W), BF16),
                   jax.ShapeDtypeStruct((nb, B_HEADS, HEAD_DIM, HEAD_DIM), F32)],
        scratch_shapes=[pltpu.VMEM((3, tb + SUBLANES, B_W), F32),
                        pltpu.VMEM((B_HEADS, tb, HEAD_DIM), F32),
                        pltpu.VMEM((B_HEADS, tb, HEAD_DIM), F32),
                        pltpu.VMEM((B_HEADS, tb, HEAD_DIM), F32),
                        pltpu.VMEM((tb, HEAD_DIM), F32),
                        pltpu.VMEM((tb, HEAD_DIM), F32)],
        compiler_params=_cparams("parallel", "arbitrary"), name="deltanet")(
            main, main, main, main, main, conv_state, conv_w, alog, dtb, nw, s0)


def _bias_from_rel(rel, tbl_ref, h):
    bias = jnp.full(rel.shape, tbl_ref[h], F32)
    for b in range(1, N_BUCKETS):
        bias = jnp.where(rel >= BUCKET_START[b], tbl_ref[b * C_HEADS + h], bias)
    return bias


def _lambda(lamp_ref, lam_init):
    lp = lamp_ref[...]
    s1 = jnp.sum(lp[0:1] * lp[1:2], axis=-1, keepdims=True)
    s2 = jnp.sum(lp[2:3] * lp[3:4], axis=-1, keepdims=True)
    return jnp.exp(s1) - jnp.exp(s2) + lam_init


def _split_halves(qh):
    lane = lax.broadcasted_iota(jnp.int32, qh.shape, 1)
    return jnp.where(lane < C_HALF, qh, 0.0), jnp.where(lane >= C_HALF, qh, 0.0)


def _softmax_update(m_ref, l_ref, acc_ref, idx, s, vb):
    m_old = m_ref[idx]
    m_new = jnp.maximum(m_old, jnp.max(s, axis=-1, keepdims=True))
    p = jnp.exp(s - m_new)
    a = jnp.exp(m_old - m_new)
    l_ref[idx] = a * l_ref[idx] + jnp.sum(p, axis=-1, keepdims=True)
    acc_ref[idx] = a * acc_ref[idx] + _dot(p.astype(BF16), vb)
    m_ref[idx] = m_new


def _attn_kernel(tbl_ref, q_ref, k_ref, v_ref, lamp_ref, nw_ref, o_ref, ko_ref, vo_ref,
                 kb_s, vb_s, bias_s, m_s, l_s, acc_s, *, tq, lam_init):
    h = pl.program_id(1)
    qi = pl.program_id(2)
    scale = C_HALF ** -0.5

    @pl.when(qi == 0)
    def _():
        kf = k_ref[...]
        vf = v_ref[...]
        ko_ref[...] = kf
        vo_ref[...] = vf
        kb_s[...] = kf.astype(BF16)
        vb_s[...] = vf.astype(BF16)
        r = lax.broadcasted_iota(jnp.int32, (tq, tq), 0) - lax.broadcasted_iota(jnp.int32, (tq, tq), 1)
        bias_s[0] = jnp.where(r >= 0, _bias_from_rel(r, tbl_ref, h), NEG_INF)
        bias_s[1] = _bias_from_rel(r + tq, tbl_ref, h)

    m_s[...] = jnp.full(m_s.shape, NEG_INF, F32)
    l_s[...] = jnp.zeros(l_s.shape, F32)
    acc_s[...] = jnp.zeros(acc_s.shape, F32)
    q1, q2 = _split_halves(q_ref[...] * scale)
    q1 = q1.astype(BF16)
    q2 = q2.astype(BF16)

    def block(kstart, bias):
        kb = kb_s[pl.ds(kstart, tq), :]
        vb = vb_s[pl.ds(kstart, tq), :]
        _softmax_update(m_s, l_s, acc_s, 0, _dot_nt(q1, kb) + bias, vb)
        _softmax_update(m_s, l_s, acc_s, 1, _dot_nt(q2, kb) + bias, vb)

    far_bias = tbl_ref[(N_BUCKETS - 1) * C_HEADS + h]

    def far(kj, carry):
        block(pl.multiple_of(kj * tq, tq), far_bias)
        return carry

    lax.fori_loop(0, jnp.maximum(qi - 1, 0), far, 0)

    @pl.when(qi >= 1)
    def _():
        block(pl.multiple_of((qi - 1) * tq, tq), bias_s[1])

    block(pl.multiple_of(qi * tq, tq), bias_s[0])

    lam = _lambda(lamp_ref, lam_init)
    o = acc_s[0] / l_s[0] - lam * (acc_s[1] / l_s[1])
    o_ref[...] = (_rms(o, nw_ref[...]) * (1.0 - lam_init)).astype(o_ref.dtype)


def _attn_prompt(main, tbl, lamp, nw, l, *, nb, t_total, tq, lam_init):
    assert tq > FAR_DISTANCE
    m = main.shape[0]
    nq = t_total // tq
    cq, ck, cv = COL_CQ // HEAD_DIM, COL_CK // HEAD_DIM, COL_CV // HEAD_DIM
    return pl.pallas_call(
        functools.partial(_attn_kernel, tq=tq, lam_init=lam_init),
        grid=(nb, C_HEADS, nq),
        in_specs=[pl.BlockSpec(memory_space=pltpu.SMEM),
                  pl.BlockSpec((tq, HEAD_DIM), lambda b, h, i: (b * nq + i, cq + h)),
                  pl.BlockSpec((t_total, HEAD_DIM), lambda b, h, i: (b, ck + h)),
                  pl.BlockSpec((t_total, HEAD_DIM), lambda b, h, i: (b, cv + h)),
                  pl.BlockSpec((None, 4, C_HALF), lambda b, h, i: (l, 0, 0)),
                  pl.BlockSpec((None, 1, HEAD_DIM), lambda b, h, i: (l, 0, 0))],
        out_specs=[pl.BlockSpec((tq, HEAD_DIM), lambda b, h, i: (b * nq + i, h)),
                   pl.BlockSpec((t_total, HEAD_DIM), lambda b, h, i: (b, h)),
                   pl.BlockSpec((t_total, HEAD_DIM), lambda b, h, i: (b, h))],
        out_shape=[jax.ShapeDtypeStruct((m, C_W), BF16),
                   jax.ShapeDtypeStruct((m, C_W), F32),
                   jax.ShapeDtypeStruct((m, C_W), F32)],
        scratch_shapes=[pltpu.VMEM((t_total, HEAD_DIM), BF16),
                        pltpu.VMEM((t_total, HEAD_DIM), BF16),
                        pltpu.VMEM((2, tq, tq), F32),
                        pltpu.VMEM((2, tq, 1), F32),
                        pltpu.VMEM((2, tq, 1), F32),
                        pltpu.VMEM((2, tq, HEAD_DIM), F32)],
        compiler_params=_cparams("parallel", "parallel", "arbitrary"), name="attn_prompt")(
            tbl, main, main, main, lamp, nw)


def _attn_dec_kernel(pt_ref, tbl_ref, q_ref, kn_ref, vn_ref, lamp_ref, nw_ref, *rest,
                     ts, pages, page, nsteps, lam_init):
    k_refs = rest[:pages]
    v_refs = rest[pages:2 * pages]
    o_ref = rest[2 * pages]
    m_s, l_s, acc_s, kn_s, vn_s = rest[2 * pages + 1:]
    s_id = pl.program_id(1)
    scale = C_HALF ** -0.5
    width = pages * page

    @pl.when(s_id == 0)
    def _():
        m_s[...] = jnp.full(m_s.shape, NEG_INF, F32)
        l_s[...] = jnp.zeros(l_s.shape, F32)
        acc_s[...] = jnp.zeros(acc_s.shape, F32)

    q = q_ref[...] * scale

    def stacked_q(h):
        q1, q2 = _split_halves(q[:, h * HEAD_DIM:(h + 1) * HEAD_DIM])
        return jnp.concatenate([q1, q2], axis=0).astype(BF16)

    def past(h, bias):
        sl = slice(h * HEAD_DIM, (h + 1) * HEAD_DIM)
        kcat = jnp.concatenate([r[:, sl].astype(BF16) for r in k_refs], axis=0)
        vcat = jnp.concatenate([r[:, sl].astype(BF16) for r in v_refs], axis=0)
        _softmax_update(m_s, l_s, acc_s, h, _dot_nt(stacked_q(h), kcat) + bias, vcat)

    @pl.when(s_id < nsteps - 1)
    def _():
        for h in range(C_HEADS):
            past(h, tbl_ref[(N_BUCKETS - 1) * C_HEADS + h])

    @pl.when(s_id == nsteps - 1)
    def _():
        tq_pos = lax.rem(lax.broadcasted_iota(jnp.int32, (2 * ts, width), 0), ts)
        rel_past = tq_pos + width - lax.broadcasted_iota(jnp.int32, (2 * ts, width), 1)
        rel_new = (lax.rem(lax.broadcasted_iota(jnp.int32, (2 * ts, page), 0), ts)
                   - lax.broadcasted_iota(jnp.int32, (2 * ts, page), 1))
        kn_s[...] = jnp.zeros(kn_s.shape, F32)
        vn_s[...] = jnp.zeros(vn_s.shape, F32)
        kn_s[0:ts, :] = kn_ref[...]
        vn_s[0:ts, :] = vn_ref[...]
        lam = _lambda(lamp_ref, lam_init)
        for h in range(C_HEADS):
            sl = slice(h * HEAD_DIM, (h + 1) * HEAD_DIM)
            past(h, _bias_from_rel(rel_past, tbl_ref, h))
            bias_new = jnp.where(rel_new >= 0, _bias_from_rel(rel_new, tbl_ref, h), NEG_INF)
            s_new = _dot_nt(stacked_q(h), kn_s[:, sl].astype(BF16)) + bias_new
            _softmax_update(m_s, l_s, acc_s, h, s_new, vn_s[:, sl].astype(BF16))
            on = acc_s[h] / l_s[h]
            o = on[0:ts] - lam * on[ts:2 * ts]
            o_ref[:, sl] = (_rms(o, nw_ref[...]) * (1.0 - lam_init)).astype(o_ref.dtype)


def _attn_sample(q_bm, kn_bm, vn_bm, cache_k, cache_v, page_table, tbl, lamp, nw, l, *, nb, ts, n_pool, lam_init):
    page = cache_k.shape[1]
    n_pages = page_table.shape[0] // nb
    pages = math.gcd(n_pages, 8)
    nsteps = n_pages // pages
    assert pages * page > FAR_DISTANCE + ts

    def kv_spec(i):
        return pl.BlockSpec((None, page, C_W),
                            lambda b, s, pt: (l * n_pool + pt[b * n_pages + s * pages + i], 0, 0))

    tok = pl.BlockSpec((ts, C_W), lambda b, s, pt: (b, 0))
    grid_spec = pltpu.PrefetchScalarGridSpec(
        num_scalar_prefetch=1, grid=(nb, nsteps),
        in_specs=[pl.BlockSpec(memory_space=pltpu.SMEM), tok, tok, tok,
                  pl.BlockSpec((None, 4, C_HALF), lambda b, s, pt: (l, 0, 0)),
                  pl.BlockSpec((None, 1, HEAD_DIM), lambda b, s, pt: (l, 0, 0))]
                 + [kv_spec(i) for i in range(pages)] + [kv_spec(i) for i in range(pages)],
        out_specs=pl.BlockSpec((ts, C_W), lambda b, s, pt: (b, 0)),
        scratch_shapes=[pltpu.VMEM((C_HEADS, 2 * ts, 1), F32),
                        pltpu.VMEM((C_HEADS, 2 * ts, 1), F32),
                        pltpu.VMEM((C_HEADS, 2 * ts, HEAD_DIM), F32),
                        pltpu.VMEM((page, C_W), F32),
                        pltpu.VMEM((page, C_W), F32)])
    return pl.pallas_call(
        functools.partial(_attn_dec_kernel, ts=ts, pages=pages, page=page, nsteps=nsteps, lam_init=lam_init),
        grid_spec=grid_spec,
        out_shape=jax.ShapeDtypeStruct((nb * ts, C_W), F32),
        compiler_params=_cparams("parallel", "arbitrary"), name="attn_sample")(
            page_table, tbl, q_bm, kn_bm, vn_bm, lamp, nw, *([cache_k] * pages), *([cache_v] * pages))


def _outproj_kernel(ob_ref, oc_ref, oa_ref, wb_ref, wc_ref, wa_ref, x_ref, nw_ref, xo_ref, ho_ref):
    acc = _dot(ob_ref[...], wb_ref[...]) + _dot(oc_ref[...], wc_ref[...]) + _dot(oa_ref[...], wa_ref[...])
    x = x_ref[...] + acc
    xo_ref[...] = x
    ho_ref[...] = _rms(x, nw_ref[...]).astype(ho_ref.dtype)


def _outproj(ob, oc, oa, w, x, nw, l, tm):
    m, d = x.shape
    return pl.pallas_call(
        _outproj_kernel, grid=(m // tm,),
        in_specs=[pl.BlockSpec((tm, B_W), lambda i: (i, 0)),
                  pl.BlockSpec((tm, C_W), lambda i: (i, 0)),
                  pl.BlockSpec((tm, A_W), lambda i: (i, 0)),
                  pl.BlockSpec((None, B_W, d), lambda i: (l, 0, 0)),
                  pl.BlockSpec((None, C_W, d), lambda i: (l, 1, 0)),
                  pl.BlockSpec((None, A_W, d), lambda i: (l, (B_W + C_W) // A_W, 0)),
                  pl.BlockSpec((tm, d), lambda i: (i, 0)),
                  pl.BlockSpec((None, 1, d), lambda i: (l, 0, 0))],
        out_specs=[pl.BlockSpec((tm, d), lambda i: (i, 0)), pl.BlockSpec((tm, d), lambda i: (i, 0))],
        out_shape=[jax.ShapeDtypeStruct((m, d), F32), jax.ShapeDtypeStruct((m, d), BF16)],
        compiler_params=_cparams("parallel"), name="outproj")(ob, oc, oa, w, w, w, x, nw)


def _ffn_up_kernel(h_ref, wg_ref, wu_ref, cw_ref, cb_ref, st_ref, o_ref, so_ref, buf_s, *, tm, stride, head, blocks):
    i = pl.program_id(1)

    @pl.when(i % blocks == 0)
    def _():
        buf_s[0:head, :] = st_ref[...]

    hn = h_ref[...]
    g = _dot(hn, wg_ref[...])
    u = _dot(hn, wu_ref[...])
    buf_s[head:head + tm, :] = g
    y = cw_ref[FFN_CONV - 1:FFN_CONV, :] * g + cb_ref[...]
    for tap in range(FFN_CONV - 1):
        off = head - (FFN_CONV - 1 - tap) * stride
        y = y + cw_ref[tap:tap + 1, :] * buf_s[off:off + tm, :]
    o_ref[...] = (_gelu(y) * u).astype(o_ref.dtype)
    last = buf_s[tm:tm + head, :]
    so_ref[...] = last
    buf_s[0:head, :] = last


def _ffn_up(hn, wg, wu, cw, cb, state, l, *, tm, tn, stride, blocks):
    m, d = hn.shape
    dff = wg.shape[-1]
    head = state.shape[1]
    nseq = state.shape[0]
    return pl.pallas_call(
        functools.partial(_ffn_up_kernel, tm=tm, stride=stride, head=head, blocks=blocks),
        grid=(dff // tn, m // tm),
        in_specs=[pl.BlockSpec((tm, d), lambda j, i: (i, 0)),
                  pl.BlockSpec((None, d, tn), lambda j, i: (l, 0, j)),
                  pl.BlockSpec((None, d, tn), lambda j, i: (l, 0, j)),
                  pl.BlockSpec((None, FFN_CONV, tn), lambda j, i: (l, 0, j)),
                  pl.BlockSpec((None, 1, tn), lambda j, i: (l, 0, j)),
                  pl.BlockSpec((None, head, tn), lambda j, i: (i // blocks, 0, j))],
        out_specs=[pl.BlockSpec((tm, tn), lambda j, i: (i, j)),
                   pl.BlockSpec((None, head, tn), lambda j, i: (i // blocks, 0, j))],
        out_shape=[jax.ShapeDtypeStruct((m, dff), BF16), jax.ShapeDtypeStruct((nseq, head, dff), F32)],
        scratch_shapes=[pltpu.VMEM((tm + head, tn), F32)],
        compiler_params=_cparams("parallel", "arbitrary"), name="ffn_up")(hn, wg, wu, cw, cb, state)


def _ffn_down_kernel(h_ref, w_ref, x_ref, o_ref):
    o_ref[...] = x_ref[...] + _dot(h_ref[...], w_ref[...])


def _ffn_down(hid, w, x, l, tm, tn):
    m, dff = hid.shape
    d = x.shape[1]
    return pl.pallas_call(
        _ffn_down_kernel, grid=(m // tm, d // tn),
        in_specs=[pl.BlockSpec((tm, dff), lambda i, j: (i, 0)),
                  pl.BlockSpec((None, dff, tn), lambda i, j: (l, 0, j)),
                  pl.BlockSpec((tm, tn), lambda i, j: (i, j))],
        out_specs=pl.BlockSpec((tm, tn), lambda i, j: (i, j)),
        out_shape=jax.ShapeDtypeStruct((m, d), F32),
        compiler_params=_cparams("parallel", "parallel"), name="ffn_down")(hid, w, x)


def _pick(total, want):
    t = min(total, want)
    while total % t:
        t //= 2
    return t


def kernel(x_prompt, x_sample, cache_k, cache_v, state_conv_b, state_delta, state_ffn_conv, page_table, attn_norm_w, w_in, a_ln_w, a_ln_b, a_ws, a_bs, b_conv_w, b_a_log, b_dt_bias, b_norm_w, c_lam_q1, c_lam_k1, c_lam_q2, c_lam_k2, c_norm_w, rel_bias, w_out, ffn_norm_w, w_gate, w_up, ffn_conv_w, ffn_conv_b, w_down, final_norm_w):
    bp, tp, d = x_prompt.shape
    bs, ts, _ = x_sample.shape
    depth = w_in.shape[0]
    dff = w_gate.shape[-1]
    n_pool, page = cache_k.shape[1], cache_k.shape[2]
    mp, ms = bp * tp, bs * ts

    o_au, o_av, o_qkv, o_z, o_beta = 0, A_W, 2 * A_W, 2 * A_W + 3 * B_W, 2 * A_W + 4 * B_W
    o_cq = o_beta + 2 * B_HEADS
    o_ck, o_cv = o_cq + C_W, o_cq + 2 * C_W
    w_main = jnp.concatenate([
        w_in[:, :, o_qkv:o_z], w_in[:, :, o_z:o_beta], w_in[:, :, o_cq:o_ck], w_in[:, :, o_beta:o_cq],
        jnp.zeros((depth, d, COL_AU - COL_BA - 2 * B_HEADS), w_in.dtype),
        w_in[:, :, o_au:o_qkv], w_in[:, :, o_ck:o_cv + C_W]], axis=-1).astype(BF16)
    w_out_r = jnp.concatenate([w_out[:, A_W:], w_out[:, :A_W]], axis=1).astype(BF16)
    wg = w_gate.astype(BF16)
    wu = w_up.astype(BF16)
    wd = w_down.astype(BF16)
    attn_nw = attn_norm_w.reshape(depth, 1, d)
    ffn_nw = ffn_norm_w.reshape(depth, 1, d)
    lnw = a_ln_w.reshape(depth, 1, A_W)
    lnb = a_ln_b.reshape(depth, 1, A_W)
    bcol_p = jnp.swapaxes(a_bs, 1, 2)
    tril = jnp.tril(jnp.ones((ts, ts), F32))
    assert ms <= A_CHUNK
    wmix_s = jnp.einsum('lhts,bc->lhtbsc', a_ws[:, :, :ts, :ts] * tril, jnp.eye(bs, dtype=F32)).reshape(
        depth, A_HEADS, ms, ms)
    wmix_s = jnp.pad(wmix_s, ((0, 0), (0, 0), (0, A_CHUNK - ms), (0, A_CHUNK - ms)))
    bcol_s = jnp.repeat(jnp.swapaxes(a_bs[:, :, :ts], 1, 2), bs, axis=1)
    pad_lanes = jnp.zeros((depth, 1, HEAD_DIM - 2 * B_HEADS), F32)
    alog = jnp.concatenate([jnp.zeros((depth, 1, B_HEADS), F32), b_a_log.reshape(depth, 1, B_HEADS), pad_lanes], -1)
    dtb = jnp.concatenate([jnp.zeros((depth, 1, B_HEADS), F32), b_dt_bias.reshape(depth, 1, B_HEADS), pad_lanes], -1)
    b_nw = b_norm_w.reshape(depth, 1, HEAD_DIM)
    c_nw = c_norm_w.reshape(depth, 1, HEAD_DIM)
    lamp = jnp.stack([c_lam_q1, c_lam_k1, c_lam_q2, c_lam_k2], axis=1)
    tbl = rel_bias.reshape(-1)
    ffn_cb = ffn_conv_b.reshape(depth, 1, dff)
    ck_flat = cache_k.reshape(depth * n_pool, page, C_W)
    cv_flat = cache_v.reshape(depth * n_pool, page, C_W)
    pt_flat = page_table.reshape(-1)

    conv_head = SUBLANES
    zero_conv_p = jnp.zeros((bp, conv_head, 3 * B_W), F32)
    zero_delta_p = jnp.zeros((bp, B_HEADS, HEAD_DIM, HEAD_DIM), F32)
    zero_ffn_p = jnp.zeros((bp, SUBLANES, dff), F32)
    ffn_head_s = max(SUBLANES, (FFN_CONV - 1) * bs)

    tm_in = _pick(mp, TM_IN)
    tm_out = _pick(mp, TM_OUT)
    tm_up = _pick(tp, TM_UP)
    tm_down = _pick(mp, TM_DOWN)
    tb_delta = _pick(tp, TB_DELTA)
    tq = _pick(tp, TQ)
    ts_pad_delta = DN_PAIR

    xp = x_prompt.reshape(mp, d)
    xs = jnp.swapaxes(x_sample, 0, 1).reshape(ms, d)

    def to_bm(a):
        return jnp.swapaxes(a.reshape(ts, bs, -1), 0, 1)

    def to_tm(a):
        return jnp.swapaxes(a, 0, 1).reshape(ms, -1)

    p_k, p_v, p_conv, p_delta, p_ffn = [], [], [], [], []
    s_k, s_v, s_conv, s_delta, s_ffn, s_av = [], [], [], [], [], []
    for l in range(depth):
        lam_init = 0.8 - 0.6 * math.exp(-0.3 * l)

        main = _inproj(xp, attn_nw, w_main, l, tm_in, 512)
        (out_a,) = _gmlp(main, lnw, lnb, a_ws, bcol_p, l, rows=4 * A_CHUNK, chunk=A_CHUNK, causal_mask=True,
                         want_v=False)
        out_b, delta_new = _delta(main, zero_conv_p, b_conv_w, alog, dtb, b_nw, zero_delta_p, l,
                                  nb=bp, t_total=tp, t_valid=tp, tb=tb_delta)
        out_c, k_new, v_new = _attn_prompt(main, tbl, lamp, c_nw, l, nb=bp, t_total=tp, tq=tq, lam_init=lam_init)
        xp, hn2 = _outproj(out_b, out_c, out_a, w_out_r, xp, ffn_nw, l, tm_out)
        hid, ffn_last = _ffn_up(hn2, wg, wu, ffn_conv_w, ffn_cb, zero_ffn_p, l, tm=tm_up, tn=512, stride=1,
                                blocks=tp // tm_up)
        xp = _ffn_down(hid, wd, xp, l, tm_down, 512)
        main3 = main.reshape(bp, tp, N_MAIN)
        p_k.append(k_new.reshape(bp, tp, C_HEADS, HEAD_DIM))
        p_v.append(v_new.reshape(bp, tp, C_HEADS, HEAD_DIM))
        p_conv.append(main3[:, tp - (B_CONV - 1):, COL_BQ:COL_BQ + 3 * B_W])
        p_delta.append(delta_new)
        p_ffn.append(ffn_last[:, SUBLANES - (FFN_CONV - 1):])

        main_s = _inproj(xs, attn_nw, w_main, l, ms, 512)
        out_a_s, va_s = _gmlp(main_s, lnw, lnb, wmix_s, bcol_s, l, rows=ms, chunk=A_CHUNK, causal_mask=False,
                              want_v=True)
        main_bm = to_bm(main_s)
        main_pad = jnp.pad(main_bm, ((0, 0), (0, ts_pad_delta - ts), (0, 0))).reshape(bs * ts_pad_delta, N_MAIN)
        conv_s = jnp.pad(state_conv_b[l], ((0, 0), (conv_head - (B_CONV - 1), 0), (0, 0)))
        out_b_s, delta_new_s = _delta(main_pad, conv_s, b_conv_w, alog, dtb, b_nw, state_delta[l], l,
                                      nb=bs, t_total=ts_pad_delta, t_valid=ts, tb=ts_pad_delta)
        out_b_s = to_tm(out_b_s.reshape(bs, ts_pad_delta, B_W)[:, :ts])
        q_bm = main_bm[:, :, COL_CQ:COL_CQ + C_W].reshape(ms, C_W)
        kn_bm = main_bm[:, :, COL_CK:COL_CK + C_W]
        vn_bm = main_bm[:, :, COL_CV:COL_CV + C_W]
        out_c_s = _attn_sample(q_bm, kn_bm.reshape(ms, C_W), vn_bm.reshape(ms, C_W), ck_flat, cv_flat, pt_flat,
                               tbl, lamp, c_nw, l, nb=bs, ts=ts, n_pool=n_pool, lam_init=lam_init)
        out_c_s = to_tm(out_c_s.reshape(bs, ts, C_W)).astype(BF16)
        xs, hn2_s = _outproj(out_b_s, out_c_s, out_a_s, w_out_r, xs, ffn_nw, l, ms)
        ffn_state_s = jnp.swapaxes(state_ffn_conv[l], 0, 1).reshape(1, (FFN_CONV - 1) * bs, dff)
        ffn_state_s = jnp.pad(ffn_state_s, ((0, 0), (ffn_head_s - (FFN_CONV - 1) * bs, 0), (0, 0)))
        hid_s, ffn_last_s = _ffn_up(hn2_s, wg, wu, ffn_conv_w, ffn_cb, ffn_state_s, l, tm=ms, tn=512, stride=bs,
                                    blocks=1)
        xs = _ffn_down(hid_s, wd, xs, l, ms, 512)
        s_k.append(kn_bm.reshape(bs, ts, C_HEADS, HEAD_DIM))
        s_v.append(vn_bm.reshape(bs, ts, C_HEADS, HEAD_DIM))
        s_conv.append(main_bm[:, ts - (B_CONV - 1):, COL_BQ:COL_BQ + 3 * B_W])
        s_delta.append(delta_new_s)
        s_ffn.append(jnp.swapaxes(ffn_last_s[0, ffn_head_s - (FFN_CONV - 1) * bs:].reshape(FFN_CONV - 1, bs, dff),
                                  0, 1))
        s_av.append(to_bm(va_s).reshape(bs, ts, A_HEADS, HEAD_DIM))

    y_prompt = _rmsnorm(xp, final_norm_w, _pick(mp, 512)).reshape(bp, tp, d)
    y_sample = jnp.swapaxes(_rmsnorm(xs, final_norm_w, ms).reshape(ts, bs, d), 0, 1)
    return (y_prompt, y_sample,
            jnp.stack(p_k), jnp.stack(p_v), jnp.stack(p_conv), jnp.stack(p_delta), jnp.stack(p_ffn),
            jnp.stack(s_k), jnp.stack(s_v), jnp.stack(s_conv), jnp.stack(s_delta), jnp.stack(s_ffn),
            jnp.stack(s_av))
```

```python
import functools
import math

import numpy as np
import jax
import jax.numpy as jnp
from jax import lax
from jax.experimental import pallas as pl
from jax.experimental.pallas import tpu as pltpu

F32 = jnp.float32
BF16 = jnp.bfloat16
HI = lax.Precision.HIGHEST

HEAD_DIM = 128
SUBLANES = 8
A_HEADS, B_HEADS, C_HEADS = 4, 6, 6
A_W, B_W, C_W = A_HEADS * HEAD_DIM, B_HEADS * HEAD_DIM, C_HEADS * HEAD_DIM
A_CHUNK = 128
B_CONV = 4
DN_CHUNK = 64
DN_PAIR = 2 * DN_CHUNK
C_HALF = HEAD_DIM // 2
N_BUCKETS = 32
MAX_DISTANCE = 128
FFN_CONV = 3
EPS = 1e-6
NEG_INF = -1e30
VMEM_LIMIT = 56 * 1024 * 1024
TM_IN, TM_OUT, TM_UP, TM_DOWN, TB_DELTA, TQ = 1024, 512, 1024, 1024, 256, 512
PAGES_PER_STEP = 8
LOG2E = math.log2(math.e)

COL_BQ, COL_BK, COL_BV, COL_BZ, COL_CQ = 0, B_W, 2 * B_W, 3 * B_W, 4 * B_W
COL_BA = COL_CQ + C_W
COL_AU = COL_BA + 2 * HEAD_DIM
COL_AV = COL_AU + A_W
COL_CK = COL_AV + A_W
COL_CV = COL_CK + C_W
N_MAIN = COL_CV + C_W


def _t5_bucket_starts():
    n = np.arange(4 * MAX_DISTANCE)
    max_exact = N_BUCKETS // 2
    nf = np.maximum(n, max_exact).astype(np.float64)
    large = max_exact + (np.log(nf / max_exact) / math.log(MAX_DISTANCE / max_exact)
                         * (N_BUCKETS - max_exact)).astype(np.int64)
    bucket = np.where(n < max_exact, n, np.minimum(large, N_BUCKETS - 1))
    return [int(n[bucket == b].min()) for b in range(N_BUCKETS)]


BUCKET_START = _t5_bucket_starts()
FAR_DISTANCE = BUCKET_START[-1]


def _cparams(*sem):
    return pltpu.CompilerParams(dimension_semantics=sem, vmem_limit_bytes=VMEM_LIMIT)


def _gelu(x):
    return 0.5 * x * (1.0 + jnp.tanh(math.sqrt(2.0 / math.pi) * (x + 0.044715 * (x * x * x))))


def _sigmoid(x):
    return 1.0 / (1.0 + jnp.exp(-x))


def _silu(x):
    return x * _sigmoid(x)


def _softplus(x):
    return jnp.maximum(x, 0.0) + jnp.log(1.0 + jnp.exp(-jnp.abs(x)))


def _rms(x, w):
    return x * lax.rsqrt(jnp.mean(x * x, axis=-1, keepdims=True) + EPS) * w


def _dot(a, b, **kw):
    return jnp.dot(a, b, preferred_element_type=F32, **kw)


def _dot_nt(a, b, **kw):
    return lax.dot_general(a, b, (((1,), (1,)), ((), ())), preferred_element_type=F32, **kw)


def _split2(x):
    hi = x.astype(BF16)
    return hi, (x - hi.astype(F32)).astype(BF16)


def _split3(x):
    hi = x.astype(BF16)
    r = x - hi.astype(F32)
    mid = r.astype(BF16)
    return hi, mid, (r - mid.astype(F32)).astype(BF16)


def _dot3(a_hl, b_hl):
    (ah, al), (bh, bl) = a_hl, b_hl
    return _dot(jnp.concatenate([ah, ah, al], axis=1), jnp.concatenate([bh, bl, bh], axis=0))


def _rmsnorm_kernel(x_ref, w_ref, o_ref):
    o_ref[...] = _rms(x_ref[...], w_ref[...]).astype(o_ref.dtype)


def _rmsnorm(x, w, tm):
    m, d = x.shape
    return pl.pallas_call(
        _rmsnorm_kernel, grid=(m // tm,),
        in_specs=[pl.BlockSpec((tm, d), lambda i: (i, 0)), pl.BlockSpec((1, d), lambda i: (0, 0))],
        out_specs=pl.BlockSpec((tm, d), lambda i: (i, 0)),
        out_shape=jax.ShapeDtypeStruct((m, d), F32),
        compiler_params=_cparams("parallel"), name="final_rmsnorm")(x, w.reshape(1, d))


def _inproj_kernel(x_ref, nw_ref, w_ref, o_ref, hn_s):
    @pl.when(pl.program_id(1) == 0)
    def _():
        hn_s[...] = _rms(x_ref[...], nw_ref[...]).astype(BF16)

    o_ref[...] = _dot(hn_s[...], w_ref[...])


def _inproj(x, nw, w, l, tm, tn):
    m, d = x.shape
    n = w.shape[-1]
    return pl.pallas_call(
        _inproj_kernel, grid=(m // tm, n // tn),
        in_specs=[pl.BlockSpec((tm, d), lambda i, j: (i, 0)),
                  pl.BlockSpec((None, 1, d), lambda i, j: (l, 0, 0)),
                  pl.BlockSpec((None, d, tn), lambda i, j: (l, 0, j))],
        out_specs=pl.BlockSpec((tm, tn), lambda i, j: (i, j)),
        out_shape=jax.ShapeDtypeStruct((m, n), F32),
        scratch_shapes=[pltpu.VMEM((tm, d), BF16)],
        compiler_params=_cparams("parallel", "arbitrary"), name="inproj")(x, nw, w)


def _gmlp_kernel(u_ref, v_ref, lnw_ref, lnb_ref, w_ref, bcol_ref, *out_refs, chunk, rows, causal_mask):
    o_ref = out_refs[0]
    nchunk = max(rows // chunk, 1)
    live = min(rows, chunk)
    if causal_mask:
        ii = lax.broadcasted_iota(jnp.int32, (chunk, chunk), 0)
        jj = lax.broadcasted_iota(jnp.int32, (chunk, chunk), 1)
        keep = ii >= jj
    for h in range(A_HEADS):
        sl = slice(h * HEAD_DIM, (h + 1) * HEAD_DIM)
        u = _gelu(u_ref[:, sl])
        v = _gelu(v_ref[:, sl])
        mu = jnp.mean(v, axis=-1, keepdims=True)
        vc = v - mu
        vn = vc * lax.rsqrt(jnp.mean(vc * vc, axis=-1, keepdims=True) + EPS) * lnw_ref[:, sl] + lnb_ref[:, sl]
        if len(out_refs) > 1:
            out_refs[1][:, sl] = vn
        w = w_ref[h]
        if causal_mask:
            w = jnp.where(keep, w, 0.0)
        wb = w.astype(BF16)
        vb = vn.astype(BF16)
        bcol = bcol_ref[0:live, h:h + 1]
        if live < chunk:
            vb = jnp.concatenate([vb, jnp.zeros((chunk - live, HEAD_DIM), BF16)], axis=0)
        for c in range(nchunk):
            rs = slice(c * live, (c + 1) * live)
            mixed = _dot(wb, vb[c * chunk:(c + 1) * chunk])[0:live] + bcol
            o_ref[rs, sl] = (u[rs] * mixed).astype(o_ref.dtype)


def _gmlp(main, lnw, lnb, wmix, bcol, l, *, rows, chunk, causal_mask, want_v):
    m = main.shape[0]
    live = min(rows, chunk)
    cu, cv = COL_AU // A_W, COL_AV // A_W
    out_shape = [jax.ShapeDtypeStruct((m, A_W), BF16)]
    out_specs = [pl.BlockSpec((rows, A_W), lambda i: (i, 0))]
    if want_v:
        out_shape.append(jax.ShapeDtypeStruct((m, A_W), F32))
        out_specs.append(pl.BlockSpec((rows, A_W), lambda i: (i, 0)))
    return pl.pallas_call(
        functools.partial(_gmlp_kernel, chunk=chunk, rows=rows, causal_mask=causal_mask),
        grid=(m // rows,),
        in_specs=[pl.BlockSpec((rows, A_W), lambda i: (i, cu)),
                  pl.BlockSpec((rows, A_W), lambda i: (i, cv)),
                  pl.BlockSpec((None, 1, A_W), lambda i: (l, 0, 0)),
                  pl.BlockSpec((None, 1, A_W), lambda i: (l, 0, 0)),
                  pl.BlockSpec((None, A_HEADS, chunk, chunk), lambda i: (l, 0, 0, 0)),
                  pl.BlockSpec((None, live, A_HEADS), lambda i: (l, 0, 0))],
        out_specs=out_specs, out_shape=out_shape,
        compiler_params=_cparams("parallel"), name="gmlp")(main, main, lnw, lnb, wmix, bcol)


def _delta_kernel(q_ref, k_ref, v_ref, z_ref, ba_ref, cs_ref, cw_ref, alog_ref, dtb_ref, nw_ref, s0_ref,
                  o_ref, s_ref, buf_s, qs_s, ks_s, vs_s, g_s, bt_s, gc_s, u_s, w_s, ai_s, qd_s, kd_s,
                  *, tb, t_valid, t_total):
    t = pl.program_id(1)
    head = SUBLANES

    @pl.when(t == 0)
    def _():
        s_ref[...] = s0_ref[...]
        for j in range(3):
            buf_s[j, 0:head, :] = cs_ref[:, j * B_W:(j + 1) * B_W]

    if t_valid < t_total:
        row = t * tb + lax.broadcasted_iota(jnp.int32, (tb, 1), 0)
        live = row < t_valid
    else:
        live = None

    def keep(x):
        return x if live is None else jnp.where(live, x, 0.0)

    for j, (src, dst) in enumerate(((q_ref, qs_s), (k_ref, ks_s), (v_ref, vs_s))):
        x = src[...]
        buf_s[j, head:head + tb, :] = x
        w = cw_ref[:, j * B_W:(j + 1) * B_W]
        y = w[B_CONV - 1:B_CONV] * x
        for tap in range(B_CONV - 1):
            off = head - (B_CONV - 1) + tap
            y = y + w[tap:tap + 1] * buf_s[j, off:off + tb, :]
        buf_s[j, 0:head, :] = buf_s[j, tb:tb + head, :]
        y = keep(_silu(y))
        for h in range(B_HEADS):
            sl = slice(h * HEAD_DIM, (h + 1) * HEAD_DIM)
            seg = y[:, sl]
            if j == 0:
                seg = seg * lax.rsqrt(jnp.sum(seg * seg, axis=-1, keepdims=True) + EPS) * (HEAD_DIM ** -0.5)
            elif j == 1:
                seg = seg * lax.rsqrt(jnp.sum(seg * seg, axis=-1, keepdims=True) + EPS)
            dst[h] = seg

    ba = ba_ref[...]
    bt_s[...] = keep(_sigmoid(ba))
    g_s[...] = keep(-jnp.exp(alog_ref[...]) * _softplus(ba + dtb_ref[...]))

    n = DN_PAIR
    ii = lax.broadcasted_iota(jnp.int32, (n, n), 0)
    jj = lax.broadcasted_iota(jnp.int32, (n, n), 1)
    same = (ii >= DN_CHUNK) == (jj >= DN_CHUNK)
    incl = same & (ii >= jj)
    strict = same & (ii > jj)
    tri = jnp.where(incl, 1.0, 0.0)
    eye = jnp.where(ii == jj, 1.0, 0.0)
    rowi = lax.broadcasted_iota(jnp.int32, (n, 1), 0)
    lane = lax.broadcasted_iota(jnp.int32, (n, n), 1)
    zeros_half = jnp.zeros((DN_CHUNK, HEAD_DIM), F32)

    tri3 = jnp.concatenate([tri.astype(BF16)] * 3, axis=1)

    heads = range(B_HEADS)
    gcol_of = lambda gc, h: gc[:, B_HEADS + h:B_HEADS + h + 1]

    def intra(p, carry):
        r0 = pl.multiple_of(p * n, n)
        rows = pl.ds(r0, n)
        gc = _dot(tri3, jnp.concatenate(_split3(g_s[rows, :]), axis=0))
        gct = gc.T
        bt = bt_s[rows, :]
        gc_s[rows, :] = gc
        q = [qs_s[h, rows, :] for h in heads]
        k = [ks_s[h, rows, :] for h in heads]
        decay = [jnp.exp(jnp.where(incl, gcol_of(gc, h) - gct[B_HEADS + h:B_HEADS + h + 1, :], NEG_INF))
                 for h in heads]
        kb = [k[h] * bt[:, h:h + 1] for h in heads]
        kq = [_dot_nt(jnp.concatenate([kb[h].astype(BF16), q[h].astype(BF16)], axis=0), k[h].astype(BF16))
              for h in heads]
        a = [-jnp.where(strict, kq[h][0:n] * decay[h], 0.0) for h in heads]
        pm = [eye + a[h] for h in heads]
        a_hl = [_split2(a[h]) for h in heads]
        a = [_dot3(a_hl[h], a_hl[h]) for h in heads]
        for _ in range(int(math.log2(DN_CHUNK)) - 2):
            a_hl = [_split2(a[h]) for h in heads]
            pm_hl = [_split2(pm[h]) for h in heads]
            prod = [_dot3(tuple(jnp.concatenate([x, y], axis=0) for x, y in zip(a_hl[h], pm_hl[h])), a_hl[h])
                    for h in heads]
            a = [prod[h][0:n] for h in heads]
            pm = [pm[h] + prod[h][n:2 * n] for h in heads]
        prod = [_dot3(_split2(pm[h]), _split2(a[h])) for h in heads]
        pm = [pm[h] + prod[h] for h in heads]
        egc = [jnp.exp(gcol_of(gc, h)) for h in heads]
        sol = [_dot3(_split2(pm[h]),
                     _split2(jnp.concatenate([vs_s[h, rows, :] * bt[:, h:h + 1], kb[h] * egc[h]], axis=1)))
               for h in heads]
        for h in heads:
            u_s[h, rows, :] = sol[h][:, 0:HEAD_DIM]
            w_s[h, rows, :] = sol[h][:, HEAD_DIM:2 * HEAD_DIM].astype(BF16)
            ai_s[h, rows, :] = (kq[h][n:2 * n] * decay[h]).astype(BF16)
            qd_s[h, rows, :] = (q[h] * egc[h]).astype(BF16)
            gcol = gcol_of(gc, h)
            glast = jnp.where(rowi < DN_CHUNK, gcol[DN_CHUNK - 1:DN_CHUNK], gcol[n - 1:n])
            kdt = (k[h] * jnp.exp(glast - gcol)).T
            kd_s[h, p, 0] = jnp.where(lane < DN_CHUNK, kdt, 0.0).astype(BF16)
            kd_s[h, p, 1] = jnp.where(lane >= DN_CHUNK, kdt, 0.0).astype(BF16)
        return carry

    def inter(p, carry):
        r0 = pl.multiple_of(p * n, n)
        s = [s_ref[h] for h in heads]
        outs = [[] for _ in heads]
        for c in range(2):
            rs = pl.ds(pl.multiple_of(r0 + c * DN_CHUNK, DN_CHUNK), DN_CHUNK)
            ws = [_dot(jnp.concatenate([w_s[h, rs, :], qd_s[h, rs, :]], axis=0), s[h].astype(BF16)) for h in heads]
            v_new = [u_s[h, rs, :] - ws[h][0:DN_CHUNK] for h in heads]
            v_pad = [jnp.concatenate([v_new[h], zeros_half] if c == 0 else [zeros_half, v_new[h]],
                                     axis=0).astype(BF16) for h in heads]
            av = [_dot(jnp.concatenate([ai_s[h, rs, :], kd_s[h, p, c]], axis=0), v_pad[h]) for h in heads]
            last = gc_s[pl.ds(pl.multiple_of(r0 + (c + 1) * DN_CHUNK - SUBLANES, SUBLANES), SUBLANES), :]
            for h in heads:
                outs[h].append(ws[h][DN_CHUNK:2 * DN_CHUNK] + av[h][0:DN_CHUNK])
                gl = jnp.exp(last[SUBLANES - 1:SUBLANES, B_HEADS + h:B_HEADS + h + 1])
                s[h] = s[h] * gl + av[h][DN_CHUNK:DN_CHUNK + HEAD_DIM]
        for h in heads:
            sl = slice(h * HEAD_DIM, (h + 1) * HEAD_DIM)
            s_ref[h] = s[h]
            o = jnp.concatenate(outs[h], axis=0)
            z = z_ref[pl.ds(r0, n), sl]
            o_ref[pl.ds(r0, n), sl] = (_rms(o, nw_ref[...]) * _silu(z)).astype(o_ref.dtype)
        return carry

    lax.fori_loop(0, tb // n, intra, 0)
    lax.fori_loop(0, tb // n, inter, 0)


def _delta(main, conv_state, conv_w, alog, dtb, nw, s0, l, *, nb, t_total, t_valid, tb):
    m = main.shape[0]
    nt = t_total // tb
    cq, ck, cv, cz, cba = COL_BQ // B_W, COL_BK // B_W, COL_BV // B_W, COL_BZ // B_W, COL_BA // HEAD_DIM
    row = lambda b, t: b * nt + t
    return pl.pallas_call(
        functools.partial(_delta_kernel, tb=tb, t_valid=t_valid, t_total=t_total),
        grid=(nb, nt),
        in_specs=[pl.BlockSpec((tb, B_W), lambda b, t: (row(b, t), cq)),
                  pl.BlockSpec((tb, B_W), lambda b, t: (row(b, t), ck)),
                  pl.BlockSpec((tb, B_W), lambda b, t: (row(b, t), cv)),
                  pl.BlockSpec((tb, B_W), lambda b, t: (row(b, t), cz)),
                  pl.BlockSpec((tb, HEAD_DIM), lambda b, t: (row(b, t), cba)),
                  pl.BlockSpec((None, SUBLANES, 3 * B_W), lambda b, t: (b, 0, 0)),
                  pl.BlockSpec((None, B_CONV, 3 * B_W), lambda b, t: (l, 0, 0)),
                  pl.BlockSpec((None, 1, HEAD_DIM), lambda b, t: (l, 0, 0)),
                  pl.BlockSpec((None, 1, HEAD_DIM), lambda b, t: (l, 0, 0)),
                  pl.BlockSpec((None, 1, HEAD_DIM), lambda b, t: (l, 0, 0)),
                  pl.BlockSpec((None, B_HEADS, HEAD_DIM, HEAD_DIM), lambda b, t: (b, 0, 0, 0))],
        out_specs=[pl.BlockSpec((tb, B_W), lambda b, t: (row(b, t), 0)),
                   pl.BlockSpec((None, B_HEADS, HEAD_DIM, HEAD_DIM), lambda b, t: (b, 0, 0, 0))],
        out_shape=[jax.ShapeDtypeStruct((m, B_W), BF16),
                   jax.ShapeDtypeStruct((nb, B_HEADS, HEAD_DIM, HEAD_DIM), F32)],
        scratch_shapes=[pltpu.VMEM((3, tb + SUBLANES, B_W), F32),
                        pltpu.VMEM((B_HEADS, tb, HEAD_DIM), F32),
                        pltpu.VMEM((B_HEADS, tb, HEAD_DIM), F32),
                        pltpu.VMEM((B_HEADS, tb, HEAD_DIM), F32),
                        pltpu.VMEM((tb, HEAD_DIM), F32),
                        pltpu.VMEM((tb, HEAD_DIM), F32),
                        pltpu.VMEM((tb, HEAD_DIM), F32),
                        pltpu.VMEM((B_HEADS, tb, HEAD_DIM), F32),
                        pltpu.VMEM((B_HEADS, tb, HEAD_DIM), BF16),
                        pltpu.VMEM((B_HEADS, tb, HEAD_DIM), BF16),
                        pltpu.VMEM((B_HEADS, tb, HEAD_DIM), BF16),
                        pltpu.VMEM((B_HEADS, tb // DN_PAIR, 2, HEAD_DIM, DN_PAIR), BF16)],
        compiler_params=_cparams("parallel", "arbitrary"), name="deltanet")(
            main, main, main, main, main, conv_state, conv_w, alog, dtb, nw, s0)


def _bias_from_rel(rel, tbl_ref, h):
    bias = jnp.full(rel.shape, tbl_ref[h], F32)
    for b in range(1, N_BUCKETS):
        bias = jnp.where(rel >= BUCKET_START[b], tbl_ref[b * C_HEADS + h], bias)
    return bias


def _lambda(lamp_ref, lam_init):
    lp = lamp_ref[...]
    s1 = jnp.sum(lp[0:1] * lp[1:2], axis=-1, keepdims=True)
    s2 = jnp.sum(lp[2:3] * lp[3:4], axis=-1, keepdims=True)
    return jnp.exp(s1) - jnp.exp(s2) + lam_init


def _split_halves(qh):
    lane = lax.broadcasted_iota(jnp.int32, qh.shape, 1)
    return jnp.where(lane < C_HALF, qh, 0.0), jnp.where(lane >= C_HALF, qh, 0.0)


def _softmax_update(m_ref, l_ref, acc_ref, idx, s, pv):
    chunks = [s[:, c * HEAD_DIM:(c + 1) * HEAD_DIM] for c in range(s.shape[1] // HEAD_DIM)]
    mx = chunks[0]
    for c in chunks[1:]:
        mx = jnp.maximum(mx, c)
    m_old = m_ref[idx]
    m_new = jnp.maximum(m_old, jnp.max(mx, axis=-1, keepdims=True))
    p = [jnp.exp2(c - m_new) for c in chunks]
    ps = p[0]
    for c in p[1:]:
        ps = ps + c
    a = jnp.exp2(m_old - m_new)
    l_ref[idx] = a * l_ref[idx] + jnp.sum(ps, axis=-1, keepdims=True)
    acc_ref[idx] = a * acc_ref[idx] + pv(jnp.concatenate([c.astype(BF16) for c in p], axis=1))
    m_ref[idx] = m_new


def _attn_kernel(tbl_ref, q_ref, k_ref, v_ref, lamp_ref, nw_ref, o_ref, ko_ref, vo_ref,
                 kb_s, vb_s, bias_s, m_s, l_s, acc_s, *, tq, lam_init):
    h = pl.program_id(0)
    b = pl.program_id(1)
    qi = pl.program_id(2)
    far_bias = tbl_ref[(N_BUCKETS - 1) * C_HEADS + h]

    @pl.when((qi == 0) & (b == 0))
    def _():
        r = lax.broadcasted_iota(jnp.int32, (tq, tq), 0) - lax.broadcasted_iota(jnp.int32, (tq, tq), 1)
        bias_s[0] = jnp.where(r >= 0, (_bias_from_rel(r, tbl_ref, h) - far_bias) * LOG2E, NEG_INF)
        bias_s[1] = (_bias_from_rel(r + tq, tbl_ref, h) - far_bias) * LOG2E

    @pl.when(qi == 0)
    def _():
        kf = k_ref[...]
        vf = v_ref[...]
        ko_ref[...] = kf
        vo_ref[...] = vf
        kb_s[...] = kf.astype(BF16)
        vb_s[...] = vf.astype(BF16)

    m_s[...] = jnp.full(m_s.shape, NEG_INF, F32)
    l_s[...] = jnp.zeros(l_s.shape, F32)
    acc_s[...] = jnp.zeros(acc_s.shape, F32)
    q1, q2 = _split_halves(q_ref[...] * (C_HALF ** -0.5 * LOG2E))
    q1 = q1.astype(BF16)
    q2 = q2.astype(BF16)

    def block(kstart, bias):
        kb = kb_s[pl.ds(kstart, tq), :]
        vb = vb_s[pl.ds(kstart, tq), :]
        for idx, qq in enumerate((q1, q2)):
            s = _dot_nt(qq, kb)
            if bias is not None:
                s = s + bias
            _softmax_update(m_s, l_s, acc_s, idx, s, lambda p: _dot(p, vb))

    def far(kj, carry):
        block(pl.multiple_of(kj * tq, tq), None)
        return carry

    lax.fori_loop(0, jnp.maximum(qi - 1, 0), far, 0)

    @pl.when(qi >= 1)
    def _():
        block(pl.multiple_of((qi - 1) * tq, tq), bias_s[1])

    block(pl.multiple_of(qi * tq, tq), bias_s[0])

    lam = _lambda(lamp_ref, lam_init)
    o = acc_s[0] / l_s[0] - lam * (acc_s[1] / l_s[1])
    o_ref[...] = (_rms(o, nw_ref[...]) * (1.0 - lam_init)).astype(o_ref.dtype)


def _attn_prompt(main, tbl, lamp, nw, l, *, nb, t_total, tq, lam_init):
    assert tq > FAR_DISTANCE
    m = main.shape[0]
    nq = t_total // tq
    cq, ck, cv = COL_CQ // HEAD_DIM, COL_CK // HEAD_DIM, COL_CV // HEAD_DIM
    return pl.pallas_call(
        functools.partial(_attn_kernel, tq=tq, lam_init=lam_init),
        grid=(C_HEADS, nb, nq),
        in_specs=[pl.BlockSpec(memory_space=pltpu.SMEM),
                  pl.BlockSpec((tq, HEAD_DIM), lambda h, b, i: (b * nq + i, cq + h)),
                  pl.BlockSpec((t_total, HEAD_DIM), lambda h, b, i: (b, ck + h)),
                  pl.BlockSpec((t_total, HEAD_DIM), lambda h, b, i: (b, cv + h)),
                  pl.BlockSpec((None, 4, C_HALF), lambda h, b, i: (l, 0, 0)),
                  pl.BlockSpec((None, 1, HEAD_DIM), lambda h, b, i: (l, 0, 0))],
        out_specs=[pl.BlockSpec((tq, HEAD_DIM), lambda h, b, i: (b * nq + i, h)),
                   pl.BlockSpec((None, None, t_total, HEAD_DIM), lambda h, b, i: (b, h, 0, 0)),
                   pl.BlockSpec((None, None, t_total, HEAD_DIM), lambda h, b, i: (b, h, 0, 0))],
        out_shape=[jax.ShapeDtypeStruct((m, C_W), BF16),
                   jax.ShapeDtypeStruct((nb, C_HEADS, t_total, HEAD_DIM), F32),
                   jax.ShapeDtypeStruct((nb, C_HEADS, t_total, HEAD_DIM), F32)],
        scratch_shapes=[pltpu.VMEM((t_total, HEAD_DIM), BF16),
                        pltpu.VMEM((t_total, HEAD_DIM), BF16),
                        pltpu.VMEM((2, tq, tq), F32),
                        pltpu.VMEM((2, tq, HEAD_DIM), F32),
                        pltpu.VMEM((2, tq, HEAD_DIM), F32),
                        pltpu.VMEM((2, tq, HEAD_DIM), F32)],
        compiler_params=_cparams("arbitrary", "arbitrary", "arbitrary"), name="attn_prompt")(
            tbl, main, main, main, lamp, nw)


def _attn_dec_kernel(pt_ref, tbl_ref, q_ref, kn_ref, vn_ref, lamp_ref, nw_ref, *rest,
                     ts, pages, page, nsteps, lam_init):
    k_refs = rest[:pages]
    v_refs = rest[pages:2 * pages]
    o_ref = rest[2 * pages]
    m_s, l_s, acc_s, kn_s, vn_s = rest[2 * pages + 1:]
    s_id = pl.program_id(1)
    width = pages * page
    rows = 2 * ts

    @pl.when(s_id == 0)
    def _():
        m_s[...] = jnp.full(m_s.shape, NEG_INF, F32)
        l_s[...] = jnp.zeros(l_s.shape, F32)
        acc_s[...] = jnp.zeros(acc_s.shape, F32)

    q = q_ref[...] * (C_HALF ** -0.5 * LOG2E)
    qq = []
    for h in range(C_HEADS):
        q1, q2 = _split_halves(q[:, h * HEAD_DIM:(h + 1) * HEAD_DIM])
        qq.append(jnp.concatenate([q1, q2], axis=0).astype(BF16))

    def update(keys, vals, bias):
        s = jnp.concatenate([_dot_nt(qq[h], keys(h)) for h in range(C_HEADS)], axis=0)
        if bias is not None:
            s = s + bias

        def pv(p):
            return jnp.concatenate([_dot(p[h * rows:(h + 1) * rows], vals(h)) for h in range(C_HEADS)], axis=0)

        _softmax_update(m_s, l_s, acc_s, slice(None), s, pv)

    def past_keys(h):
        return jnp.concatenate([r[h].astype(BF16) for r in k_refs], axis=0)

    def past_vals(h):
        return jnp.concatenate([r[h].astype(BF16) for r in v_refs], axis=0)

    @pl.when(s_id < nsteps - 1)
    def _():
        update(past_keys, past_vals, None)

    @pl.when(s_id == nsteps - 1)
    def _():
        tq_pos = lax.rem(lax.broadcasted_iota(jnp.int32, (rows, width), 0), ts)
        rel_past = tq_pos + width - lax.broadcasted_iota(jnp.int32, (rows, width), 1)
        rel_new = (lax.rem(lax.broadcasted_iota(jnp.int32, (rows, page), 0), ts)
                   - lax.broadcasted_iota(jnp.int32, (rows, page), 1))
        bias_past, bias_new = [], []
        for h in range(C_HEADS):
            far_bias = tbl_ref[(N_BUCKETS - 1) * C_HEADS + h]
            bias_past.append((_bias_from_rel(rel_past, tbl_ref, h) - far_bias) * LOG2E)
            bias_new.append(jnp.where(rel_new >= 0, (_bias_from_rel(rel_new, tbl_ref, h) - far_bias) * LOG2E, NEG_INF))
        update(past_keys, past_vals, jnp.concatenate(bias_past, axis=0))
        kn_s[...] = jnp.zeros(kn_s.shape, F32)
        vn_s[...] = jnp.zeros(vn_s.shape, F32)
        kn_s[0:ts, :] = kn_ref[...]
        vn_s[0:ts, :] = vn_ref[...]
        update(lambda h: kn_s[:, h * HEAD_DIM:(h + 1) * HEAD_DIM].astype(BF16),
               lambda h: vn_s[:, h * HEAD_DIM:(h + 1) * HEAD_DIM].astype(BF16),
               jnp.concatenate(bias_new, axis=0))
        lam = _lambda(lamp_ref, lam_init)
        on = acc_s[...] / l_s[...]
        for h in range(C_HEADS):
            o = on[h * rows:h * rows + ts] - lam * on[h * rows + ts:(h + 1) * rows]
            o_ref[:, h * HEAD_DIM:(h + 1) * HEAD_DIM] = (_rms(o, nw_ref[...]) * (1.0 - lam_init)).astype(o_ref.dtype)


def _attn_sample(q_bm, kn_bm, vn_bm, cache_k, cache_v, page_table, tbl, lamp, nw, l, *, nb, ts, n_pool, lam_init):
    page = cache_k.shape[2]
    n_pages = page_table.shape[0] // nb
    pages = math.gcd(n_pages, PAGES_PER_STEP)
    nsteps = n_pages // pages
    assert pages * page > FAR_DISTANCE + ts

    def kv_spec(i):
        return pl.BlockSpec((None, C_HEADS, page, HEAD_DIM),
                            lambda b, s, pt: (l * n_pool + pt[b * n_pages + s * pages + i], 0, 0, 0))

    tok = pl.BlockSpec((ts, C_W), lambda b, s, pt: (b, 0))
    grid_spec = pltpu.PrefetchScalarGridSpec(
        num_scalar_prefetch=1, grid=(nb, nsteps),
        in_specs=[pl.BlockSpec(memory_space=pltpu.SMEM), tok, tok, tok,
                  pl.BlockSpec((None, 4, C_HALF), lambda b, s, pt: (l, 0, 0)),
                  pl.BlockSpec((None, 1, HEAD_DIM), lambda b, s, pt: (l, 0, 0))]
                 + [kv_spec(i) for i in range(pages)] + [kv_spec(i) for i in range(pages)],
        out_specs=pl.BlockSpec((ts, C_W), lambda b, s, pt: (b, 0)),
        scratch_shapes=[pltpu.VMEM((C_HEADS * 2 * ts, HEAD_DIM), F32),
                        pltpu.VMEM((C_HEADS * 2 * ts, HEAD_DIM), F32),
                        pltpu.VMEM((C_HEADS * 2 * ts, HEAD_DIM), F32),
                        pltpu.VMEM((page, C_W), F32),
                        pltpu.VMEM((page, C_W), F32)])
    return pl.pallas_call(
        functools.partial(_attn_dec_kernel, ts=ts, pages=pages, page=page, nsteps=nsteps, lam_init=lam_init),
        grid_spec=grid_spec,
        out_shape=jax.ShapeDtypeStruct((nb * ts, C_W), F32),
        compiler_params=_cparams("parallel", "arbitrary"), name="attn_sample")(
            page_table, tbl, q_bm, kn_bm, vn_bm, lamp, nw, *([cache_k] * pages), *([cache_v] * pages))


def _outproj_kernel(ob_ref, oc_ref, oa_ref, wb_ref, wc_ref, wa_ref, x_ref, nw_ref, xo_ref, ho_ref):
    acc = _dot(ob_ref[...], wb_ref[...]) + _dot(oc_ref[...], wc_ref[...]) + _dot(oa_ref[...], wa_ref[...])
    x = x_ref[...] + acc
    xo_ref[...] = x
    ho_ref[...] = _rms(x, nw_ref[...]).astype(ho_ref.dtype)


def _outproj(ob, oc, oa, w, x, nw, l, tm):
    m, d = x.shape
    return pl.pallas_call(
        _outproj_kernel, grid=(m // tm,),
        in_specs=[pl.BlockSpec((tm, B_W), lambda i: (i, 0)),
                  pl.BlockSpec((tm, C_W), lambda i: (i, 0)),
                  pl.BlockSpec((tm, A_W), lambda i: (i, 0)),
                  pl.BlockSpec((None, B_W, d), lambda i: (l, 0, 0)),
                  pl.BlockSpec((None, C_W, d), lambda i: (l, 1, 0)),
                  pl.BlockSpec((None, A_W, d), lambda i: (l, (B_W + C_W) // A_W, 0)),
                  pl.BlockSpec((tm, d), lambda i: (i, 0)),
                  pl.BlockSpec((None, 1, d), lambda i: (l, 0, 0))],
        out_specs=[pl.BlockSpec((tm, d), lambda i: (i, 0)), pl.BlockSpec((tm, d), lambda i: (i, 0))],
        out_shape=[jax.ShapeDtypeStruct((m, d), F32), jax.ShapeDtypeStruct((m, d), BF16)],
        compiler_params=_cparams("parallel"), name="outproj")(ob, oc, oa, w, w, w, x, nw)


def _ffn_up_kernel(h_ref, wg_ref, wu_ref, cw_ref, cb_ref, st_ref, o_ref, so_ref, buf_s, *, tm, stride, head, blocks):
    i = pl.program_id(1)

    @pl.when(i % blocks == 0)
    def _():
        buf_s[0:head, :] = st_ref[...]

    hn = h_ref[...]
    g = _dot(hn, wg_ref[...])
    u = _dot(hn, wu_ref[...])
    buf_s[head:head + tm, :] = g
    y = cw_ref[FFN_CONV - 1:FFN_CONV, :] * g + cb_ref[...]
    for tap in range(FFN_CONV - 1):
        off = head - (FFN_CONV - 1 - tap) * stride
        y = y + cw_ref[tap:tap + 1, :] * buf_s[off:off + tm, :]
    o_ref[...] = (_gelu(y) * u).astype(o_ref.dtype)
    last = buf_s[tm:tm + head, :]
    so_ref[...] = last
    buf_s[0:head, :] = last


def _ffn_up(hn, wg, wu, cw, cb, state, l, *, tm, tn, stride, blocks):
    m, d = hn.shape
    dff = wg.shape[-1]
    head = state.shape[1]
    nseq = state.shape[0]
    return pl.pallas_call(
        functools.partial(_ffn_up_kernel, tm=tm, stride=stride, head=head, blocks=blocks),
        grid=(dff // tn, m // tm),
        in_specs=[pl.BlockSpec((tm, d), lambda j, i: (i, 0)),
                  pl.BlockSpec((None, d, tn), lambda j, i: (l, 0, j)),
                  pl.BlockSpec((None, d, tn), lambda j, i: (l, 0, j)),
                  pl.BlockSpec((None, FFN_CONV, tn), lambda j, i: (l, 0, j)),
                  pl.BlockSpec((None, 1, tn), lambda j, i: (l, 0, j)),
                  pl.BlockSpec((None, head, tn), lambda j, i: (i // blocks, 0, j))],
        out_specs=[pl.BlockSpec((tm, tn), lambda j, i: (i, j)),
                   pl.BlockSpec((None, head, tn), lambda j, i: (i // blocks, 0, j))],
        out_shape=[jax.ShapeDtypeStruct((m, dff), BF16), jax.ShapeDtypeStruct((nseq, head, dff), F32)],
        scratch_shapes=[pltpu.VMEM((tm + head, tn), F32)],
        compiler_params=_cparams("parallel", "arbitrary"), name="ffn_up")(hn, wg, wu, cw, cb, state)


def _ffn_down_kernel(h_ref, w_ref, x_ref, o_ref):
    o_ref[...] = x_ref[...] + _dot(h_ref[...], w_ref[...])


def _ffn_down(hid, w, x, l, tm, tn):
    m, dff = hid.shape
    d = x.shape[1]
    return pl.pallas_call(
        _ffn_down_kernel, grid=(m // tm, d // tn),
        in_specs=[pl.BlockSpec((tm, dff), lambda i, j: (i, 0)),
                  pl.BlockSpec((None, dff, tn), lambda i, j: (l, 0, j)),
                  pl.BlockSpec((tm, tn), lambda i, j: (i, j))],
        out_specs=pl.BlockSpec((tm, tn), lambda i, j: (i, j)),
        out_shape=jax.ShapeDtypeStruct((m, d), F32),
        compiler_params=_cparams("parallel", "parallel"), name="ffn_down")(hid, w, x)


def _pick(total, want):
    t = min(total, want)
    while total % t:
        t //= 2
    return t


def kernel(x_prompt, x_sample, cache_k, cache_v, state_conv_b, state_delta, state_ffn_conv, page_table, attn_norm_w, w_in, a_ln_w, a_ln_b, a_ws, a_bs, b_conv_w, b_a_log, b_dt_bias, b_norm_w, c_lam_q1, c_lam_k1, c_lam_q2, c_lam_k2, c_norm_w, rel_bias, w_out, ffn_norm_w, w_gate, w_up, ffn_conv_w, ffn_conv_b, w_down, final_norm_w):
    bp, tp, d = x_prompt.shape
    bs, ts, _ = x_sample.shape
    depth = w_in.shape[0]
    dff = w_gate.shape[-1]
    n_pool, page = cache_k.shape[1], cache_k.shape[2]
    mp, ms = bp * tp, bs * ts

    o_au, o_av, o_qkv, o_z, o_beta = 0, A_W, 2 * A_W, 2 * A_W + 3 * B_W, 2 * A_W + 4 * B_W
    o_cq = o_beta + 2 * B_HEADS
    o_ck, o_cv = o_cq + C_W, o_cq + 2 * C_W
    w_main = jnp.concatenate([
        w_in[:, :, o_qkv:o_z], w_in[:, :, o_z:o_beta], w_in[:, :, o_cq:o_ck], w_in[:, :, o_beta:o_cq],
        jnp.zeros((depth, d, COL_AU - COL_BA - 2 * B_HEADS), w_in.dtype),
        w_in[:, :, o_au:o_qkv], w_in[:, :, o_ck:o_cv + C_W]], axis=-1).astype(BF16)
    w_out_r = jnp.concatenate([w_out[:, A_W:], w_out[:, :A_W]], axis=1).astype(BF16)
    wg = w_gate.astype(BF16)
    wu = w_up.astype(BF16)
    wd = w_down.astype(BF16)
    attn_nw = attn_norm_w.reshape(depth, 1, d)
    ffn_nw = ffn_norm_w.reshape(depth, 1, d)
    lnw = a_ln_w.reshape(depth, 1, A_W)
    lnb = a_ln_b.reshape(depth, 1, A_W)
    bcol_p = jnp.swapaxes(a_bs, 1, 2)
    tril = jnp.tril(jnp.ones((ts, ts), F32))
    assert ms <= A_CHUNK
    wmix_s = jnp.einsum('lhts,bc->lhtbsc', a_ws[:, :, :ts, :ts] * tril, jnp.eye(bs, dtype=F32)).reshape(
        depth, A_HEADS, ms, ms)
    wmix_s = jnp.pad(wmix_s, ((0, 0), (0, 0), (0, A_CHUNK - ms), (0, A_CHUNK - ms)))
    bcol_s = jnp.repeat(jnp.swapaxes(a_bs[:, :, :ts], 1, 2), bs, axis=1)
    pad_lanes = jnp.zeros((depth, 1, HEAD_DIM - 2 * B_HEADS), F32)
    alog = jnp.concatenate([jnp.zeros((depth, 1, B_HEADS), F32), b_a_log.reshape(depth, 1, B_HEADS), pad_lanes], -1)
    dtb = jnp.concatenate([jnp.zeros((depth, 1, B_HEADS), F32), b_dt_bias.reshape(depth, 1, B_HEADS), pad_lanes], -1)
    b_nw = b_norm_w.reshape(depth, 1, HEAD_DIM)
    c_nw = c_norm_w.reshape(depth, 1, HEAD_DIM)
    lamp = jnp.stack([c_lam_q1, c_lam_k1, c_lam_q2, c_lam_k2], axis=1)
    tbl = rel_bias.reshape(-1)
    ffn_cb = ffn_conv_b.reshape(depth, 1, dff)
    ck_flat = jnp.swapaxes(cache_k, 2, 3).reshape(depth * n_pool, C_HEADS, page, HEAD_DIM)
    cv_flat = jnp.swapaxes(cache_v, 2, 3).reshape(depth * n_pool, C_HEADS, page, HEAD_DIM)
    pt_flat = page_table.reshape(-1)

    conv_head = SUBLANES
    zero_conv_p = jnp.zeros((bp, conv_head, 3 * B_W), F32)
    zero_delta_p = jnp.zeros((bp, B_HEADS, HEAD_DIM, HEAD_DIM), F32)
    zero_ffn_p = jnp.zeros((bp, SUBLANES, dff), F32)
    ffn_head_s = max(SUBLANES, (FFN_CONV - 1) * bs)

    tm_in = _pick(mp, TM_IN)
    tm_out = _pick(mp, TM_OUT)
    tm_up = _pick(tp, TM_UP)
    tm_down = _pick(mp, TM_DOWN)
    tb_delta = _pick(tp, TB_DELTA)
    tq = _pick(tp, TQ)
    ts_pad_delta = DN_PAIR

    xp = x_prompt.reshape(mp, d)
    xs = jnp.swapaxes(x_sample, 0, 1).reshape(ms, d)

    def to_bm(a):
        return jnp.swapaxes(a.reshape(ts, bs, -1), 0, 1)

    def to_tm(a):
        return jnp.swapaxes(a, 0, 1).reshape(ms, -1)

    p_k, p_v, p_conv, p_delta, p_ffn = [], [], [], [], []
    s_k, s_v, s_conv, s_delta, s_ffn, s_av = [], [], [], [], [], []
    for l in range(depth):
        lam_init = 0.8 - 0.6 * math.exp(-0.3 * l)

        main = _inproj(xp, attn_nw, w_main, l, tm_in, 512)
        (out_a,) = _gmlp(main, lnw, lnb, a_ws, bcol_p, l, rows=4 * A_CHUNK, chunk=A_CHUNK, causal_mask=True,
                         want_v=False)
        out_b, delta_new = _delta(main, zero_conv_p, b_conv_w, alog, dtb, b_nw, zero_delta_p, l,
                                  nb=bp, t_total=tp, t_valid=tp, tb=tb_delta)
        out_c, k_new, v_new = _attn_prompt(main, tbl, lamp, c_nw, l, nb=bp, t_total=tp, tq=tq, lam_init=lam_init)
        xp, hn2 = _outproj(out_b, out_c, out_a, w_out_r, xp, ffn_nw, l, tm_out)
        hid, ffn_last = _ffn_up(hn2, wg, wu, ffn_conv_w, ffn_cb, zero_ffn_p, l, tm=tm_up, tn=512, stride=1,
                                blocks=tp // tm_up)
        xp = _ffn_down(hid, wd, xp, l, tm_down, 512)
        main3 = main.reshape(bp, tp, N_MAIN)
        p_k.append(jnp.swapaxes(k_new, 1, 2))
        p_v.append(jnp.swapaxes(v_new, 1, 2))
        p_conv.append(main3[:, tp - (B_CONV - 1):, COL_BQ:COL_BQ + 3 * B_W])
        p_delta.append(delta_new)
        p_ffn.append(ffn_last[:, SUBLANES - (FFN_CONV - 1):])

        main_s = _inproj(xs, attn_nw, w_main, l, ms, 512)
        out_a_s, va_s = _gmlp(main_s, lnw, lnb, wmix_s, bcol_s, l, rows=ms, chunk=A_CHUNK, causal_mask=False,
                              want_v=True)
        main_bm = to_bm(main_s)
        main_pad = jnp.pad(main_bm, ((0, 0), (0, ts_pad_delta - ts), (0, 0))).reshape(bs * ts_pad_delta, N_MAIN)
        conv_s = jnp.pad(state_conv_b[l], ((0, 0), (conv_head - (B_CONV - 1), 0), (0, 0)))
        out_b_s, delta_new_s = _delta(main_pad, conv_s, b_conv_w, alog, dtb, b_nw, state_delta[l], l,
                                      nb=bs, t_total=ts_pad_delta, t_valid=ts, tb=ts_pad_delta)
        out_b_s = to_tm(out_b_s.reshape(bs, ts_pad_delta, B_W)[:, :ts])
        q_bm = main_bm[:, :, COL_CQ:COL_CQ + C_W].reshape(ms, C_W)
        kn_bm = main_bm[:, :, COL_CK:COL_CK + C_W]
        vn_bm = main_bm[:, :, COL_CV:COL_CV + C_W]
        out_c_s = _attn_sample(q_bm, kn_bm.reshape(ms, C_W), vn_bm.reshape(ms, C_W), ck_flat, cv_flat, pt_flat,
                               tbl, lamp, c_nw, l, nb=bs, ts=ts, n_pool=n_pool, lam_init=lam_init)
        out_c_s = to_tm(out_c_s.reshape(bs, ts, C_W)).astype(BF16)
        xs, hn2_s = _outproj(out_b_s, out_c_s, out_a_s, w_out_r, xs, ffn_nw, l, ms)
        ffn_state_s = jnp.swapaxes(state_ffn_conv[l], 0, 1).reshape(1, (FFN_CONV - 1) * bs, dff)
        ffn_state_s = jnp.pad(ffn_state_s, ((0, 0), (ffn_head_s - (FFN_CONV - 1) * bs, 0), (0, 0)))
        hid_s, ffn_last_s = _ffn_up(hn2_s, wg, wu, ffn_conv_w, ffn_cb, ffn_state_s, l, tm=ms, tn=512, stride=bs,
                                    blocks=1)
        xs = _ffn_down(hid_s, wd, xs, l, ms, 512)
        s_k.append(kn_bm.reshape(bs, ts, C_HEADS, HEAD_DIM))
        s_v.append(vn_bm.reshape(bs, ts, C_HEADS, HEAD_DIM))
        s_conv.append(main_bm[:, ts - (B_CONV - 1):, COL_BQ:COL_BQ + 3 * B_W])
        s_delta.append(delta_new_s)
        s_ffn.append(jnp.swapaxes(ffn_last_s[0, ffn_head_s - (FFN_CONV - 1) * bs:].reshape(FFN_CONV - 1, bs, dff),
                                  0, 1))
        s_av.append(to_bm(va_s).reshape(bs, ts, A_HEADS, HEAD_DIM))

    y_prompt = _rmsnorm(xp, final_norm_w, _pick(mp, 512)).reshape(bp, tp, d)
    y_sample = jnp.swapaxes(_rmsnorm(xs, final_norm_w, ms).reshape(ts, bs, d), 0, 1)
    return (y_prompt, y_sample,
            jnp.stack(p_k), jnp.stack(p_v), jnp.stack(p_conv), jnp.stack(p_delta), jnp.stack(p_ffn),
            jnp.stack(s_k), jnp.stack(s_v), jnp.stack(s_conv), jnp.stack(s_delta), jnp.stack(s_ffn),
            jnp.stack(s_av))
```

```python
import functools
import math

import numpy as np
import jax
import jax.numpy as jnp
from jax import lax
from jax.experimental import pallas as pl
from jax.experimental.pallas import tpu as pltpu

F32 = jnp.float32
BF16 = jnp.bfloat16
HI = lax.Precision.HIGHEST

HEAD_DIM = 128
SUBLANES = 8
A_HEADS, B_HEADS, C_HEADS = 4, 6, 6
A_W, B_W, C_W = A_HEADS * HEAD_DIM, B_HEADS * HEAD_DIM, C_HEADS * HEAD_DIM
A_CHUNK = 128
B_CONV = 4
DN_CHUNK = 64
DN_PAIR = 2 * DN_CHUNK
C_HALF = HEAD_DIM // 2
N_BUCKETS = 32
MAX_DISTANCE = 128
FFN_CONV = 3
EPS = 1e-6
NEG_INF = -1e30
VMEM_LIMIT = 56 * 1024 * 1024
TM_IN, TM_OUT, TM_UP, TM_DOWN, TB_DELTA, TQ = 1024, 512, 1024, 1024, 256, 512
PAGES_PER_STEP = 16
LOG2E = math.log2(math.e)

COL_BQ, COL_BK, COL_BV, COL_BZ, COL_CQ = 0, B_W, 2 * B_W, 3 * B_W, 4 * B_W
COL_BA = COL_CQ + C_W
COL_AU = COL_BA + 2 * HEAD_DIM
COL_AV = COL_AU + A_W
COL_CK = COL_AV + A_W
COL_CV = COL_CK + C_W
N_MAIN = COL_CV + C_W


def _t5_bucket_starts():
    n = np.arange(4 * MAX_DISTANCE)
    max_exact = N_BUCKETS // 2
    nf = np.maximum(n, max_exact).astype(np.float64)
    large = max_exact + (np.log(nf / max_exact) / math.log(MAX_DISTANCE / max_exact)
                         * (N_BUCKETS - max_exact)).astype(np.int64)
    bucket = np.where(n < max_exact, n, np.minimum(large, N_BUCKETS - 1))
    return [int(n[bucket == b].min()) for b in range(N_BUCKETS)]


BUCKET_START = _t5_bucket_starts()
FAR_DISTANCE = BUCKET_START[-1]


def _cparams(*sem):
    return pltpu.CompilerParams(dimension_semantics=sem, vmem_limit_bytes=VMEM_LIMIT)


def _gelu(x):
    return 0.5 * x * (1.0 + jnp.tanh(math.sqrt(2.0 / math.pi) * (x + 0.044715 * (x * x * x))))


def _sigmoid(x):
    return 1.0 / (1.0 + jnp.exp(-x))


def _silu(x):
    return x * _sigmoid(x)


def _softplus(x):
    return jnp.maximum(x, 0.0) + jnp.log(1.0 + jnp.exp(-jnp.abs(x)))


def _rms(x, w):
    return x * lax.rsqrt(jnp.mean(x * x, axis=-1, keepdims=True) + EPS) * w


def _dot(a, b, **kw):
    return jnp.dot(a, b, preferred_element_type=F32, **kw)


def _dot_nt(a, b, **kw):
    return lax.dot_general(a, b, (((1,), (1,)), ((), ())), preferred_element_type=F32, **kw)


def _split2(x):
    hi = x.astype(BF16)
    return hi, (x - hi.astype(F32)).astype(BF16)


def _split3(x):
    hi = x.astype(BF16)
    r = x - hi.astype(F32)
    mid = r.astype(BF16)
    return hi, mid, (r - mid.astype(F32)).astype(BF16)


def _dot3(a_hl, b_hl):
    (ah, al), (bh, bl) = a_hl, b_hl
    return _dot(jnp.concatenate([ah, ah, al], axis=1), jnp.concatenate([bh, bl, bh], axis=0))


def _rmsnorm_kernel(x_ref, w_ref, o_ref):
    o_ref[...] = _rms(x_ref[...], w_ref[...]).astype(o_ref.dtype)


def _rmsnorm(x, w, tm):
    m, d = x.shape
    return pl.pallas_call(
        _rmsnorm_kernel, grid=(m // tm,),
        in_specs=[pl.BlockSpec((tm, d), lambda i: (i, 0)), pl.BlockSpec((1, d), lambda i: (0, 0))],
        out_specs=pl.BlockSpec((tm, d), lambda i: (i, 0)),
        out_shape=jax.ShapeDtypeStruct((m, d), F32),
        compiler_params=_cparams("parallel"), name="final_rmsnorm")(x, w.reshape(1, d))


def _inproj_kernel(x_ref, nw_ref, w_ref, o_ref, hn_s):
    @pl.when(pl.program_id(1) == 0)
    def _():
        hn_s[...] = _rms(x_ref[...], nw_ref[...]).astype(BF16)

    o_ref[...] = _dot_nt(hn_s[...], w_ref[...])


def _inproj(x, nw, w, l, tm, tn):
    m, d = x.shape
    n = w.shape[0]
    return pl.pallas_call(
        _inproj_kernel, grid=(m // tm, n // tn),
        in_specs=[pl.BlockSpec((tm, d), lambda i, j: (i, 0)),
                  pl.BlockSpec((None, 1, d), lambda i, j: (l, 0, 0)),
                  pl.BlockSpec((tn, d), lambda i, j: (j, l))],
        out_specs=pl.BlockSpec((tm, tn), lambda i, j: (i, j)),
        out_shape=jax.ShapeDtypeStruct((m, n), F32),
        scratch_shapes=[pltpu.VMEM((tm, d), BF16)],
        compiler_params=_cparams("parallel", "arbitrary"), name="inproj")(x, nw, w)


def _gmlp_kernel(u_ref, v_ref, lnw_ref, lnb_ref, w_ref, bcol_ref, *out_refs, chunk, rows, causal_mask):
    o_ref = out_refs[0]
    nchunk = max(rows // chunk, 1)
    live = min(rows, chunk)
    if causal_mask:
        ii = lax.broadcasted_iota(jnp.int32, (chunk, chunk), 0)
        jj = lax.broadcasted_iota(jnp.int32, (chunk, chunk), 1)
        keep = ii >= jj
    for h in range(A_HEADS):
        sl = slice(h * HEAD_DIM, (h + 1) * HEAD_DIM)
        u = _gelu(u_ref[:, sl])
        v = _gelu(v_ref[:, sl])
        mu = jnp.mean(v, axis=-1, keepdims=True)
        vc = v - mu
        vn = vc * lax.rsqrt(jnp.mean(vc * vc, axis=-1, keepdims=True) + EPS) * lnw_ref[:, sl] + lnb_ref[:, sl]
        if len(out_refs) > 1:
            out_refs[1][:, sl] = vn
        w = w_ref[h]
        if causal_mask:
            w = jnp.where(keep, w, 0.0)
        wb = w.astype(BF16)
        vb = vn.astype(BF16)
        bcol = bcol_ref[0:live, h:h + 1]
        if live < chunk:
            vb = jnp.concatenate([vb, jnp.zeros((chunk - live, HEAD_DIM), BF16)], axis=0)
        for c in range(nchunk):
            rs = slice(c * live, (c + 1) * live)
            mixed = _dot(wb, vb[c * chunk:(c + 1) * chunk])[0:live] + bcol
            o_ref[rs, sl] = (u[rs] * mixed).astype(o_ref.dtype)


def _gmlp(main, lnw, lnb, wmix, bcol, l, *, rows, chunk, causal_mask, want_v):
    m = main.shape[0]
    live = min(rows, chunk)
    cu, cv = COL_AU // A_W, COL_AV // A_W
    out_shape = [jax.ShapeDtypeStruct((m, A_W), BF16)]
    out_specs = [pl.BlockSpec((rows, A_W), lambda i: (i, 0))]
    if want_v:
        out_shape.append(jax.ShapeDtypeStruct((m, A_W), F32))
        out_specs.append(pl.BlockSpec((rows, A_W), lambda i: (i, 0)))
    return pl.pallas_call(
        functools.partial(_gmlp_kernel, chunk=chunk, rows=rows, causal_mask=causal_mask),
        grid=(m // rows,),
        in_specs=[pl.BlockSpec((rows, A_W), lambda i: (i, cu)),
                  pl.BlockSpec((rows, A_W), lambda i: (i, cv)),
                  pl.BlockSpec((None, 1, A_W), lambda i: (l, 0, 0)),
                  pl.BlockSpec((None, 1, A_W), lambda i: (l, 0, 0)),
                  pl.BlockSpec((None, A_HEADS, chunk, chunk), lambda i: (l, 0, 0, 0)),
                  pl.BlockSpec((None, live, A_HEADS), lambda i: (l, 0, 0))],
        out_specs=out_specs, out_shape=out_shape,
        compiler_params=_cparams("parallel"), name="gmlp")(main, main, lnw, lnb, wmix, bcol)


def _delta_kernel(q_ref, k_ref, v_ref, z_ref, ba_ref, cs_ref, cw_ref, alog_ref, dtb_ref, nw_ref, s0_ref,
                  o_ref, s_ref, buf_s, qs_s, ks_s, vs_s, g_s, bt_s, gc_s, u_s, w_s, ai_s, qd_s, kd_s,
                  *, tb, t_valid, t_total):
    t = pl.program_id(1)
    head = SUBLANES

    @pl.when(t == 0)
    def _():
        s_ref[...] = s0_ref[...]
        for j in range(3):
            buf_s[j, 0:head, :] = cs_ref[:, j * B_W:(j + 1) * B_W]

    if t_valid < t_total:
        row = t * tb + lax.broadcasted_iota(jnp.int32, (tb, 1), 0)
        live = row < t_valid
    else:
        live = None

    def keep(x):
        return x if live is None else jnp.where(live, x, 0.0)

    for j, (src, dst) in enumerate(((q_ref, qs_s), (k_ref, ks_s), (v_ref, vs_s))):
        x = src[...]
        buf_s[j, head:head + tb, :] = x
        w = cw_ref[:, j * B_W:(j + 1) * B_W]
        y = w[B_CONV - 1:B_CONV] * x
        for tap in range(B_CONV - 1):
            off = head - (B_CONV - 1) + tap
            y = y + w[tap:tap + 1] * buf_s[j, off:off + tb, :]
        buf_s[j, 0:head, :] = buf_s[j, tb:tb + head, :]
        y = keep(_silu(y))
        for h in range(B_HEADS):
            sl = slice(h * HEAD_DIM, (h + 1) * HEAD_DIM)
            seg = y[:, sl]
            if j == 0:
                seg = seg * lax.rsqrt(jnp.sum(seg * seg, axis=-1, keepdims=True) + EPS) * (HEAD_DIM ** -0.5)
            elif j == 1:
                seg = seg * lax.rsqrt(jnp.sum(seg * seg, axis=-1, keepdims=True) + EPS)
            dst[h] = seg

    ba = ba_ref[...]
    bt_s[...] = keep(_sigmoid(ba))
    g_s[...] = keep(-jnp.exp(alog_ref[...]) * _softplus(ba + dtb_ref[...]))

    n = DN_PAIR
    ii = lax.broadcasted_iota(jnp.int32, (n, n), 0)
    jj = lax.broadcasted_iota(jnp.int32, (n, n), 1)
    same = (ii >= DN_CHUNK) == (jj >= DN_CHUNK)
    incl = same & (ii >= jj)
    strict = same & (ii > jj)
    tri = jnp.where(incl, 1.0, 0.0)
    eye = jnp.where(ii == jj, 1.0, 0.0)
    rowi = lax.broadcasted_iota(jnp.int32, (n, 1), 0)
    lane = lax.broadcasted_iota(jnp.int32, (n, n), 1)
    zeros_half = jnp.zeros((DN_CHUNK, HEAD_DIM), F32)

    tri3 = jnp.concatenate([tri.astype(BF16)] * 3, axis=1)

    heads = range(B_HEADS)
    gcol_of = lambda gc, h: gc[:, B_HEADS + h:B_HEADS + h + 1]

    def intra(p, carry):
        r0 = pl.multiple_of(p * n, n)
        rows = pl.ds(r0, n)
        gc = _dot(tri3, jnp.concatenate(_split3(g_s[rows, :]), axis=0))
        gct = gc.T
        bt = bt_s[rows, :]
        gc_s[rows, :] = gc
        q = [qs_s[h, rows, :] for h in heads]
        k = [ks_s[h, rows, :] for h in heads]
        decay = [jnp.exp(jnp.where(incl, gcol_of(gc, h) - gct[B_HEADS + h:B_HEADS + h + 1, :], NEG_INF))
                 for h in heads]
        kb = [k[h] * bt[:, h:h + 1] for h in heads]
        kq = [_dot_nt(jnp.concatenate([kb[h].astype(BF16), q[h].astype(BF16)], axis=0), k[h].astype(BF16))
              for h in heads]
        a = [-jnp.where(strict, kq[h][0:n] * decay[h], 0.0) for h in heads]
        pm = [eye + a[h] for h in heads]
        a_hl = [_split2(a[h]) for h in heads]
        a = [_dot3(a_hl[h], a_hl[h]) for h in heads]
        for _ in range(int(math.log2(DN_CHUNK)) - 2):
            a_hl = [_split2(a[h]) for h in heads]
            pm_hl = [_split2(pm[h]) for h in heads]
            prod = [_dot3(tuple(jnp.concatenate([x, y], axis=0) for x, y in zip(a_hl[h], pm_hl[h])), a_hl[h])
                    for h in heads]
            a = [prod[h][0:n] for h in heads]
            pm = [pm[h] + prod[h][n:2 * n] for h in heads]
        prod = [_dot3(_split2(pm[h]), _split2(a[h])) for h in heads]
        pm = [pm[h] + prod[h] for h in heads]
        egc = [jnp.exp(gcol_of(gc, h)) for h in heads]
        sol = [_dot3(_split2(pm[h]),
                     _split2(jnp.concatenate([vs_s[h, rows, :] * bt[:, h:h + 1], kb[h] * egc[h]], axis=1)))
               for h in heads]
        for h in heads:
            u_s[h, rows, :] = sol[h][:, 0:HEAD_DIM]
            w_s[h, rows, :] = sol[h][:, HEAD_DIM:2 * HEAD_DIM].astype(BF16)
            ai_s[h, rows, :] = (kq[h][n:2 * n] * decay[h]).astype(BF16)
            qd_s[h, rows, :] = (q[h] * egc[h]).astype(BF16)
            gcol = gcol_of(gc, h)
            glast = jnp.where(rowi < DN_CHUNK, gcol[DN_CHUNK - 1:DN_CHUNK], gcol[n - 1:n])
            kdt = (k[h] * jnp.exp(glast - gcol)).T
            kd_s[h, p, 0] = jnp.where(lane < DN_CHUNK, kdt, 0.0).astype(BF16)
            kd_s[h, p, 1] = jnp.where(lane >= DN_CHUNK, kdt, 0.0).astype(BF16)
        return carry

    def inter(p, carry):
        r0 = pl.multiple_of(p * n, n)
        s = [s_ref[h] for h in heads]
        outs = [[] for _ in heads]
        for c in range(2):
            rs = pl.ds(pl.multiple_of(r0 + c * DN_CHUNK, DN_CHUNK), DN_CHUNK)
            ws = [_dot(jnp.concatenate([w_s[h, rs, :], qd_s[h, rs, :]], axis=0), s[h].astype(BF16)) for h in heads]
            v_new = [u_s[h, rs, :] - ws[h][0:DN_CHUNK] for h in heads]
            v_pad = [jnp.concatenate([v_new[h], zeros_half] if c == 0 else [zeros_half, v_new[h]],
                                     axis=0).astype(BF16) for h in heads]
            av = [_dot(jnp.concatenate([ai_s[h, rs, :], kd_s[h, p, c]], axis=0), v_pad[h]) for h in heads]
            last = gc_s[pl.ds(pl.multiple_of(r0 + (c + 1) * DN_CHUNK - SUBLANES, SUBLANES), SUBLANES), :]
            for h in heads:
                outs[h].append(ws[h][DN_CHUNK:2 * DN_CHUNK] + av[h][0:DN_CHUNK])
                gl = jnp.exp(last[SUBLANES - 1:SUBLANES, B_HEADS + h:B_HEADS + h + 1])
                s[h] = s[h] * gl + av[h][DN_CHUNK:DN_CHUNK + HEAD_DIM]
        for h in heads:
            sl = slice(h * HEAD_DIM, (h + 1) * HEAD_DIM)
            s_ref[h] = s[h]
            o = jnp.concatenate(outs[h], axis=0)
            z = z_ref[pl.ds(r0, n), sl]
            o_ref[pl.ds(r0, n), sl] = (_rms(o, nw_ref[...]) * _silu(z)).astype(o_ref.dtype)
        return carry

    lax.fori_loop(0, tb // n, intra, 0)
    lax.fori_loop(0, tb // n, inter, 0)


def _delta(main, conv_state, conv_w, alog, dtb, nw, s0, l, *, nb, t_total, t_valid, tb):
    m = main.shape[0]
    nt = t_total // tb
    cq, ck, cv, cz, cba = COL_BQ // B_W, COL_BK // B_W, COL_BV // B_W, COL_BZ // B_W, COL_BA // HEAD_DIM
    row = lambda b, t: b * nt + t
    return pl.pallas_call(
        functools.partial(_delta_kernel, tb=tb, t_valid=t_valid, t_total=t_total),
        grid=(nb, nt),
        in_specs=[pl.BlockSpec((tb, B_W), lambda b, t: (row(b, t), cq)),
                  pl.BlockSpec((tb, B_W), lambda b, t: (row(b, t), ck)),
                  pl.BlockSpec((tb, B_W), lambda b, t: (row(b, t), cv)),
                  pl.BlockSpec((tb, B_W), lambda b, t: (row(b, t), cz)),
                  pl.BlockSpec((tb, HEAD_DIM), lambda b, t: (row(b, t), cba)),
                  pl.BlockSpec((None, SUBLANES, 3 * B_W), lambda b, t: (b, 0, 0)),
                  pl.BlockSpec((None, B_CONV, 3 * B_W), lambda b, t: (l, 0, 0)),
                  pl.BlockSpec((None, 1, HEAD_DIM), lambda b, t: (l, 0, 0)),
                  pl.BlockSpec((None, 1, HEAD_DIM), lambda b, t: (l, 0, 0)),
                  pl.BlockSpec((None, 1, HEAD_DIM), lambda b, t: (l, 0, 0)),
                  pl.BlockSpec((None, B_HEADS, HEAD_DIM, HEAD_DIM), lambda b, t: (b, 0, 0, 0))],
        out_specs=[pl.BlockSpec((tb, B_W), lambda b, t: (row(b, t), 0)),
                   pl.BlockSpec((None, B_HEADS, HEAD_DIM, HEAD_DIM), lambda b, t: (b, 0, 0, 0))],
        out_shape=[jax.ShapeDtypeStruct((m, B_W), BF16),
                   jax.ShapeDtypeStruct((nb, B_HEADS, HEAD_DIM, HEAD_DIM), F32)],
        scratch_shapes=[pltpu.VMEM((3, tb + SUBLANES, B_W), F32),
                        pltpu.VMEM((B_HEADS, tb, HEAD_DIM), F32),
                        pltpu.VMEM((B_HEADS, tb, HEAD_DIM), F32),
                        pltpu.VMEM((B_HEADS, tb, HEAD_DIM), F32),
                        pltpu.VMEM((tb, HEAD_DIM), F32),
                        pltpu.VMEM((tb, HEAD_DIM), F32),
                        pltpu.VMEM((tb, HEAD_DIM), F32),
                        pltpu.VMEM((B_HEADS, tb, HEAD_DIM), F32),
                        pltpu.VMEM((B_HEADS, tb, HEAD_DIM), BF16),
                        pltpu.VMEM((B_HEADS, tb, HEAD_DIM), BF16),
                        pltpu.VMEM((B_HEADS, tb, HEAD_DIM), BF16),
                        pltpu.VMEM((B_HEADS, tb // DN_PAIR, 2, HEAD_DIM, DN_PAIR), BF16)],
        compiler_params=_cparams("parallel", "arbitrary"), name="deltanet")(
            main, main, main, main, main, conv_state, conv_w, alog, dtb, nw, s0)


def _bias_from_rel(rel, tbl_ref, h):
    bias = jnp.full(rel.shape, tbl_ref[h], F32)
    for b in range(1, N_BUCKETS):
        bias = jnp.where(rel >= BUCKET_START[b], tbl_ref[b * C_HEADS + h], bias)
    return bias


def _lambda(lamp_ref, lam_init):
    lp = lamp_ref[...]
    s1 = jnp.sum(lp[0:1] * lp[1:2], axis=-1, keepdims=True)
    s2 = jnp.sum(lp[2:3] * lp[3:4], axis=-1, keepdims=True)
    return jnp.exp(s1) - jnp.exp(s2) + lam_init


def _split_halves(qh):
    lane = lax.broadcasted_iota(jnp.int32, qh.shape, 1)
    return jnp.where(lane < C_HALF, qh, 0.0), jnp.where(lane >= C_HALF, qh, 0.0)


def _softmax_update(m_ref, l_ref, acc_ref, idx, s, pv):
    chunks = [s[:, c * HEAD_DIM:(c + 1) * HEAD_DIM] for c in range(s.shape[1] // HEAD_DIM)]
    mx = chunks[0]
    for c in chunks[1:]:
        mx = jnp.maximum(mx, c)
    m_old = m_ref[idx]
    m_new = jnp.maximum(m_old, jnp.max(mx, axis=-1, keepdims=True))
    p = [jnp.exp2(c - m_new) for c in chunks]
    ps = p[0]
    for c in p[1:]:
        ps = ps + c
    a = jnp.exp2(m_old - m_new)
    l_ref[idx] = a * l_ref[idx] + jnp.sum(ps, axis=-1, keepdims=True)
    acc_ref[idx] = a * acc_ref[idx] + pv(jnp.concatenate([c.astype(BF16) for c in p], axis=1))
    m_ref[idx] = m_new


def _attn_kernel(tbl_ref, q_ref, k_ref, v_ref, lamp_ref, nw_ref, *rest, tq, lam_init):
    o_ref, ko_ref, vo_ref, kb_s, vb_s, bias_s, m_s, l_s, acc_s = rest[-9:]
    h = pl.program_id(0)
    b = pl.program_id(1)
    qi = pl.program_id(2)
    far_bias = tbl_ref[(N_BUCKETS - 1) * C_HEADS + h]

    @pl.when((qi == 0) & (b == 0))
    def _():
        r = lax.broadcasted_iota(jnp.int32, (tq, tq), 0) - lax.broadcasted_iota(jnp.int32, (tq, tq), 1)
        bias_s[0] = jnp.where(r >= 0, (_bias_from_rel(r, tbl_ref, h) - far_bias) * LOG2E, NEG_INF)
        bias_s[1] = (_bias_from_rel(r + tq, tbl_ref, h) - far_bias) * LOG2E

    @pl.when(qi == 0)
    def _():
        kf = k_ref[...]
        vf = v_ref[...]
        ko_ref[...] = kf
        vo_ref[...] = vf
        kb_s[...] = kf.astype(BF16)
        vb_s[...] = vf.astype(BF16)

    m_s[...] = jnp.full(m_s.shape, NEG_INF, F32)
    l_s[...] = jnp.zeros(l_s.shape, F32)
    acc_s[...] = jnp.zeros(acc_s.shape, F32)
    q1, q2 = _split_halves(q_ref[...] * (C_HALF ** -0.5 * LOG2E))
    q1 = q1.astype(BF16)
    q2 = q2.astype(BF16)

    def block(kstart, bias):
        kb = kb_s[pl.ds(kstart, tq), :]
        vb = vb_s[pl.ds(kstart, tq), :]
        scores = [_dot_nt(qq, kb) for qq in (q1, q2)]
        for idx, s in enumerate(scores):
            if bias is not None:
                s = s + bias
            _softmax_update(m_s, l_s, acc_s, idx, s, lambda p: _dot(p, vb))

    def far(kj, carry):
        block(pl.multiple_of(kj * tq, tq), None)
        return carry

    lax.fori_loop(0, jnp.maximum(qi - 1, 0), far, 0)

    @pl.when(qi >= 1)
    def _():
        block(pl.multiple_of((qi - 1) * tq, tq), bias_s[1])

    block(pl.multiple_of(qi * tq, tq), bias_s[0])

    lam = _lambda(lamp_ref, lam_init)
    o = acc_s[0] / l_s[0] - lam * (acc_s[1] / l_s[1])
    o_ref[...] = (_rms(o, nw_ref[...]) * (1.0 - lam_init)).astype(o_ref.dtype)


def _attn_prompt(main, tbl, lamp, nw, l, kv_stacks, *, depth, nb, t_total, tq, lam_init):
    assert tq > FAR_DISTANCE
    m = main.shape[0]
    nq = t_total // tq
    cq, ck, cv = COL_CQ // HEAD_DIM, COL_CK // HEAD_DIM, COL_CV // HEAD_DIM
    kv_shape = jax.ShapeDtypeStruct((depth, nb, C_HEADS, t_total, HEAD_DIM), F32)
    kv_spec = pl.BlockSpec((None, None, None, t_total, HEAD_DIM), lambda h, b, i: (l, b, h, 0, 0))
    in_specs = [pl.BlockSpec(memory_space=pltpu.SMEM),
                pl.BlockSpec((tq, HEAD_DIM), lambda h, b, i: (b * nq + i, cq + h)),
                pl.BlockSpec((t_total, HEAD_DIM), lambda h, b, i: (b, ck + h)),
                pl.BlockSpec((t_total, HEAD_DIM), lambda h, b, i: (b, cv + h)),
                pl.BlockSpec((None, 4, C_HALF), lambda h, b, i: (l, 0, 0)),
                pl.BlockSpec((None, 1, HEAD_DIM), lambda h, b, i: (l, 0, 0))]
    args = [tbl, main, main, main, lamp, nw]
    aliases = {}
    if kv_stacks is not None:
        aliases = {len(args): 1, len(args) + 1: 2}
        in_specs += [pl.BlockSpec(memory_space=pl.ANY)] * 2
        args += list(kv_stacks)
    return pl.pallas_call(
        functools.partial(_attn_kernel, tq=tq, lam_init=lam_init),
        grid=(C_HEADS, nb, nq),
        in_specs=in_specs,
        out_specs=[pl.BlockSpec((tq, HEAD_DIM), lambda h, b, i: (b * nq + i, h)), kv_spec, kv_spec],
        out_shape=[jax.ShapeDtypeStruct((m, C_W), BF16), kv_shape, kv_shape],
        input_output_aliases=aliases,
        scratch_shapes=[pltpu.VMEM((t_total, HEAD_DIM), BF16),
                        pltpu.VMEM((t_total, HEAD_DIM), BF16),
                        pltpu.VMEM((2, tq, tq), F32),
                        pltpu.VMEM((2, tq, HEAD_DIM), F32),
                        pltpu.VMEM((2, tq, HEAD_DIM), F32),
                        pltpu.VMEM((2, tq, HEAD_DIM), F32)],
        compiler_params=_cparams("arbitrary", "arbitrary", "arbitrary"), name="attn_prompt")(*args)


def _attn_dec_kernel(pt_ref, tbl_ref, q_ref, kn_ref, vn_ref, lamp_ref, nw_ref, *rest,
                     ts, pages, page, nsteps, lam_init):
    k_refs = rest[:pages]
    v_refs = rest[pages:2 * pages]
    o_ref = rest[2 * pages]
    m_s, l_s, acc_s, kn_s, vn_s = rest[2 * pages + 1:]
    s_id = pl.program_id(1)
    width = pages * page
    rows = 2 * ts

    @pl.when(s_id == 0)
    def _():
        m_s[...] = jnp.full(m_s.shape, NEG_INF, F32)
        l_s[...] = jnp.zeros(l_s.shape, F32)
        acc_s[...] = jnp.zeros(acc_s.shape, F32)

    q = q_ref[...] * (C_HALF ** -0.5 * LOG2E)
    qq = []
    for h in range(C_HEADS):
        q1, q2 = _split_halves(q[:, h * HEAD_DIM:(h + 1) * HEAD_DIM])
        qq.append(jnp.concatenate([q1, q2], axis=0).astype(BF16))

    def update(keys, vals, bias):
        s = jnp.concatenate([_dot_nt(qq[h], keys(h)) for h in range(C_HEADS)], axis=0)
        if bias is not None:
            s = s + bias

        def pv(p):
            return jnp.concatenate([_dot(p[h * rows:(h + 1) * rows], vals(h)) for h in range(C_HEADS)], axis=0)

        _softmax_update(m_s, l_s, acc_s, slice(None), s, pv)

    def past_keys(h):
        return jnp.concatenate([r[h].astype(BF16) for r in k_refs], axis=0)

    def past_vals(h):
        return jnp.concatenate([r[h].astype(BF16) for r in v_refs], axis=0)

    @pl.when(s_id < nsteps - 1)
    def _():
        update(past_keys, past_vals, None)

    @pl.when(s_id == nsteps - 1)
    def _():
        tq_pos = lax.rem(lax.broadcasted_iota(jnp.int32, (rows, width), 0), ts)
        rel_past = tq_pos + width - lax.broadcasted_iota(jnp.int32, (rows, width), 1)
        rel_new = (lax.rem(lax.broadcasted_iota(jnp.int32, (rows, page), 0), ts)
                   - lax.broadcasted_iota(jnp.int32, (rows, page), 1))
        bias_past, bias_new = [], []
        for h in range(C_HEADS):
            far_bias = tbl_ref[(N_BUCKETS - 1) * C_HEADS + h]
            bias_past.append((_bias_from_rel(rel_past, tbl_ref, h) - far_bias) * LOG2E)
            bias_new.append(jnp.where(rel_new >= 0, (_bias_from_rel(rel_new, tbl_ref, h) - far_bias) * LOG2E, NEG_INF))
        update(past_keys, past_vals, jnp.concatenate(bias_past, axis=0))
        kn_s[...] = jnp.zeros(kn_s.shape, F32)
        vn_s[...] = jnp.zeros(vn_s.shape, F32)
        kn_s[0:ts, :] = kn_ref[...]
        vn_s[0:ts, :] = vn_ref[...]
        update(lambda h: kn_s[:, h * HEAD_DIM:(h + 1) * HEAD_DIM].astype(BF16),
               lambda h: vn_s[:, h * HEAD_DIM:(h + 1) * HEAD_DIM].astype(BF16),
               jnp.concatenate(bias_new, axis=0))
        lam = _lambda(lamp_ref, lam_init)
        on = acc_s[...] / l_s[...]
        for h in range(C_HEADS):
            o = on[h * rows:h * rows + ts] - lam * on[h * rows + ts:(h + 1) * rows]
            o_ref[:, h * HEAD_DIM:(h + 1) * HEAD_DIM] = (_rms(o, nw_ref[...]) * (1.0 - lam_init)).astype(o_ref.dtype)


def _attn_sample(q_bm, kn_bm, vn_bm, cache_k, cache_v, page_table, tbl, lamp, nw, l, *, nb, ts, n_pool, lam_init):
    page = cache_k.shape[2]
    n_pages = page_table.shape[0] // nb
    pages = math.gcd(n_pages, PAGES_PER_STEP)
    nsteps = n_pages // pages
    assert pages * page > FAR_DISTANCE + ts

    def kv_spec(i):
        return pl.BlockSpec((None, C_HEADS, page, HEAD_DIM),
                            lambda b, s, pt: (l * n_pool + pt[b * n_pages + s * pages + i], 0, 0, 0))

    tok = pl.BlockSpec((ts, C_W), lambda b, s, pt: (b, 0))
    grid_spec = pltpu.PrefetchScalarGridSpec(
        num_scalar_prefetch=1, grid=(nb, nsteps),
        in_specs=[pl.BlockSpec(memory_space=pltpu.SMEM), tok, tok, tok,
                  pl.BlockSpec((None, 4, C_HALF), lambda b, s, pt: (l, 0, 0)),
                  pl.BlockSpec((None, 1, HEAD_DIM), lambda b, s, pt: (l, 0, 0))]
                 + [kv_spec(i) for i in range(pages)] + [kv_spec(i) for i in range(pages)],
        out_specs=pl.BlockSpec((ts, C_W), lambda b, s, pt: (b, 0)),
        scratch_shapes=[pltpu.VMEM((C_HEADS * 2 * ts, HEAD_DIM), F32),
                        pltpu.VMEM((C_HEADS * 2 * ts, HEAD_DIM), F32),
                        pltpu.VMEM((C_HEADS * 2 * ts, HEAD_DIM), F32),
                        pltpu.VMEM((page, C_W), F32),
                        pltpu.VMEM((page, C_W), F32)])
    return pl.pallas_call(
        functools.partial(_attn_dec_kernel, ts=ts, pages=pages, page=page, nsteps=nsteps, lam_init=lam_init),
        grid_spec=grid_spec,
        out_shape=jax.ShapeDtypeStruct((nb * ts, C_W), F32),
        compiler_params=_cparams("parallel", "arbitrary"), name="attn_sample")(
            page_table, tbl, q_bm, kn_bm, vn_bm, lamp, nw, *([cache_k] * pages), *([cache_v] * pages))


def _outproj_kernel(ob_ref, oc_ref, oa_ref, wb_ref, wc_ref, wa_ref, x_ref, nw_ref, xo_ref, ho_ref):
    acc = _dot(ob_ref[...], wb_ref[...]) + _dot(oc_ref[...], wc_ref[...]) + _dot(oa_ref[...], wa_ref[...])
    x = x_ref[...] + acc
    xo_ref[...] = x
    ho_ref[...] = _rms(x, nw_ref[...]).astype(ho_ref.dtype)


def _outproj(ob, oc, oa, w, x, nw, l, tm):
    m, d = x.shape
    return pl.pallas_call(
        _outproj_kernel, grid=(m // tm,),
        in_specs=[pl.BlockSpec((tm, B_W), lambda i: (i, 0)),
                  pl.BlockSpec((tm, C_W), lambda i: (i, 0)),
                  pl.BlockSpec((tm, A_W), lambda i: (i, 0)),
                  pl.BlockSpec((None, B_W, d), lambda i: (l, 0, 0)),
                  pl.BlockSpec((None, C_W, d), lambda i: (l, 1, 0)),
                  pl.BlockSpec((None, A_W, d), lambda i: (l, (B_W + C_W) // A_W, 0)),
                  pl.BlockSpec((tm, d), lambda i: (i, 0)),
                  pl.BlockSpec((None, 1, d), lambda i: (l, 0, 0))],
        out_specs=[pl.BlockSpec((tm, d), lambda i: (i, 0)), pl.BlockSpec((tm, d), lambda i: (i, 0))],
        out_shape=[jax.ShapeDtypeStruct((m, d), F32), jax.ShapeDtypeStruct((m, d), BF16)],
        compiler_params=_cparams("parallel"), name="outproj")(ob, oc, oa, w, w, w, x, nw)


def _ffn_up_kernel(h_ref, wg_ref, wu_ref, cw_ref, cb_ref, st_ref, o_ref, so_ref, buf_s, wg_s, wu_s,
                   *, tm, stride, head, blocks):
    i = pl.program_id(1)

    @pl.when(i == 0)
    def _():
        wg_s[...] = wg_ref[...].astype(BF16)
        wu_s[...] = wu_ref[...].astype(BF16)

    @pl.when(i % blocks == 0)
    def _():
        buf_s[0:head, :] = st_ref[...]

    hn = h_ref[...]
    g = _dot(hn, wg_s[...])
    u = _dot(hn, wu_s[...])
    buf_s[head:head + tm, :] = g
    y = cw_ref[FFN_CONV - 1:FFN_CONV, :] * g + cb_ref[...]
    for tap in range(FFN_CONV - 1):
        off = head - (FFN_CONV - 1 - tap) * stride
        y = y + cw_ref[tap:tap + 1, :] * buf_s[off:off + tm, :]
    o_ref[...] = (_gelu(y) * u).astype(o_ref.dtype)
    last = buf_s[tm:tm + head, :]
    so_ref[...] = last
    buf_s[0:head, :] = last


def _ffn_up(hn, wg, wu, cw, cb, state, l, *, tm, tn, stride, blocks):
    m, d = hn.shape
    dff = wg.shape[-1]
    head = state.shape[1]
    nseq = state.shape[0]
    return pl.pallas_call(
        functools.partial(_ffn_up_kernel, tm=tm, stride=stride, head=head, blocks=blocks),
        grid=(dff // tn, m // tm),
        in_specs=[pl.BlockSpec((tm, d), lambda j, i: (i, 0)),
                  pl.BlockSpec((None, d, tn), lambda j, i: (l, 0, j)),
                  pl.BlockSpec((None, d, tn), lambda j, i: (l, 0, j)),
                  pl.BlockSpec((None, FFN_CONV, tn), lambda j, i: (l, 0, j)),
                  pl.BlockSpec((None, 1, tn), lambda j, i: (l, 0, j)),
                  pl.BlockSpec((None, head, tn), lambda j, i: (i // blocks, 0, j))],
        out_specs=[pl.BlockSpec((tm, tn), lambda j, i: (i, j)),
                   pl.BlockSpec((None, head, tn), lambda j, i: (i // blocks, 0, j))],
        out_shape=[jax.ShapeDtypeStruct((m, dff), BF16), jax.ShapeDtypeStruct((nseq, head, dff), F32)],
        scratch_shapes=[pltpu.VMEM((tm + head, tn), F32), pltpu.VMEM((d, tn), BF16), pltpu.VMEM((d, tn), BF16)],
        compiler_params=_cparams("parallel", "arbitrary"), name="ffn_up")(hn, wg, wu, cw, cb, state)


def _ffn_down_kernel(h_ref, w_ref, x_ref, o_ref):
    o_ref[...] = x_ref[...] + _dot(h_ref[...], w_ref[...])


def _ffn_down(hid, w, x, l, tm, tn):
    m, dff = hid.shape
    d = x.shape[1]
    return pl.pallas_call(
        _ffn_down_kernel, grid=(m // tm, d // tn),
        in_specs=[pl.BlockSpec((tm, dff), lambda i, j: (i, 0)),
                  pl.BlockSpec((None, dff, tn), lambda i, j: (l, 0, j)),
                  pl.BlockSpec((tm, tn), lambda i, j: (i, j))],
        out_specs=pl.BlockSpec((tm, tn), lambda i, j: (i, j)),
        out_shape=jax.ShapeDtypeStruct((m, d), F32),
        compiler_params=_cparams("parallel", "parallel"), name="ffn_down")(hid, w, x)


def _pick(total, want):
    t = min(total, want)
    while total % t:
        t //= 2
    return t


def kernel(x_prompt, x_sample, cache_k, cache_v, state_conv_b, state_delta, state_ffn_conv, page_table, attn_norm_w, w_in, a_ln_w, a_ln_b, a_ws, a_bs, b_conv_w, b_a_log, b_dt_bias, b_norm_w, c_lam_q1, c_lam_k1, c_lam_q2, c_lam_k2, c_norm_w, rel_bias, w_out, ffn_norm_w, w_gate, w_up, ffn_conv_w, ffn_conv_b, w_down, final_norm_w):
    bp, tp, d = x_prompt.shape
    bs, ts, _ = x_sample.shape
    depth = w_in.shape[0]
    dff = w_gate.shape[-1]
    n_pool, page = cache_k.shape[1], cache_k.shape[2]
    mp, ms = bp * tp, bs * ts

    o_au, o_av, o_qkv, o_z, o_beta = 0, A_W, 2 * A_W, 2 * A_W + 3 * B_W, 2 * A_W + 4 * B_W
    o_cq = o_beta + 2 * B_HEADS
    o_ck, o_cv = o_cq + C_W, o_cq + 2 * C_W
    w_t = jnp.transpose(w_in, (2, 0, 1))
    w_main = jnp.concatenate([
        w_t[o_qkv:o_z], w_t[o_z:o_beta], w_t[o_cq:o_ck], w_t[o_beta:o_cq],
        jnp.zeros((COL_AU - COL_BA - 2 * B_HEADS, depth, d), w_in.dtype),
        w_t[o_au:o_qkv], w_t[o_ck:o_cv + C_W]], axis=0).astype(BF16).reshape(N_MAIN, depth * d)
    w_out_r = jnp.concatenate([w_out[:, A_W:], w_out[:, :A_W]], axis=1).astype(BF16)
    wg, wu = w_gate, w_up
    wd = w_down.astype(BF16)
    attn_nw = attn_norm_w.reshape(depth, 1, d)
    ffn_nw = ffn_norm_w.reshape(depth, 1, d)
    lnw = a_ln_w.reshape(depth, 1, A_W)
    lnb = a_ln_b.reshape(depth, 1, A_W)
    bcol_p = jnp.swapaxes(a_bs, 1, 2)
    tril = jnp.tril(jnp.ones((ts, ts), F32))
    assert ms <= A_CHUNK
    wmix_s = jnp.einsum('lhts,bc->lhtbsc', a_ws[:, :, :ts, :ts] * tril, jnp.eye(bs, dtype=F32)).reshape(
        depth, A_HEADS, ms, ms)
    wmix_s = jnp.pad(wmix_s, ((0, 0), (0, 0), (0, A_CHUNK - ms), (0, A_CHUNK - ms)))
    bcol_s = jnp.repeat(jnp.swapaxes(a_bs[:, :, :ts], 1, 2), bs, axis=1)
    pad_lanes = jnp.zeros((depth, 1, HEAD_DIM - 2 * B_HEADS), F32)
    alog = jnp.concatenate([jnp.zeros((depth, 1, B_HEADS), F32), b_a_log.reshape(depth, 1, B_HEADS), pad_lanes], -1)
    dtb = jnp.concatenate([jnp.zeros((depth, 1, B_HEADS), F32), b_dt_bias.reshape(depth, 1, B_HEADS), pad_lanes], -1)
    b_nw = b_norm_w.reshape(depth, 1, HEAD_DIM)
    c_nw = c_norm_w.reshape(depth, 1, HEAD_DIM)
    lamp = jnp.stack([c_lam_q1, c_lam_k1, c_lam_q2, c_lam_k2], axis=1)
    tbl = rel_bias.reshape(-1)
    ffn_cb = ffn_conv_b.reshape(depth, 1, dff)
    ck_flat = jnp.swapaxes(cache_k, 2, 3).reshape(depth * n_pool, C_HEADS, page, HEAD_DIM)
    cv_flat = jnp.swapaxes(cache_v, 2, 3).reshape(depth * n_pool, C_HEADS, page, HEAD_DIM)
    pt_flat = page_table.reshape(-1)

    conv_head = SUBLANES
    zero_conv_p = jnp.zeros((bp, conv_head, 3 * B_W), F32)
    zero_delta_p = jnp.zeros((bp, B_HEADS, HEAD_DIM, HEAD_DIM), F32)
    zero_ffn_p = jnp.zeros((bp, SUBLANES, dff), F32)
    ffn_head_s = max(SUBLANES, (FFN_CONV - 1) * bs)

    tm_in = _pick(mp, TM_IN)
    tm_out = _pick(mp, TM_OUT)
    tm_up = _pick(tp, TM_UP)
    tm_down = _pick(mp, TM_DOWN)
    tb_delta = _pick(tp, TB_DELTA)
    tq = _pick(tp, TQ)
    ts_pad_delta = DN_PAIR

    xp = x_prompt.reshape(mp, d)
    xs = jnp.swapaxes(x_sample, 0, 1).reshape(ms, d)

    def to_bm(a):
        return jnp.swapaxes(a.reshape(ts, bs, -1), 0, 1)

    def to_tm(a):
        return jnp.swapaxes(a, 0, 1).reshape(ms, -1)

    kv_stacks = None
    p_conv, p_delta, p_ffn = [], [], []
    s_k, s_v, s_conv, s_delta, s_ffn, s_av = [], [], [], [], [], []
    for l in range(depth):
        lam_init = 0.8 - 0.6 * math.exp(-0.3 * l)

        main = _inproj(xp, attn_nw, w_main, l, tm_in, 512)
        (out_a,) = _gmlp(main, lnw, lnb, a_ws, bcol_p, l, rows=4 * A_CHUNK, chunk=A_CHUNK, causal_mask=True,
                         want_v=False)
        out_b, delta_new = _delta(main, zero_conv_p, b_conv_w, alog, dtb, b_nw, zero_delta_p, l,
                                  nb=bp, t_total=tp, t_valid=tp, tb=tb_delta)
        out_c, *kv_stacks = _attn_prompt(main, tbl, lamp, c_nw, l, kv_stacks, depth=depth, nb=bp, t_total=tp, tq=tq,
                                         lam_init=lam_init)
        xp, hn2 = _outproj(out_b, out_c, out_a, w_out_r, xp, ffn_nw, l, tm_out)
        hid, ffn_last = _ffn_up(hn2, wg, wu, ffn_conv_w, ffn_cb, zero_ffn_p, l, tm=tm_up, tn=512, stride=1,
                                blocks=tp // tm_up)
        xp = _ffn_down(hid, wd, xp, l, tm_down, 512)
        main3 = main.reshape(bp, tp, N_MAIN)
        p_conv.append(main3[:, tp - (B_CONV - 1):, COL_BQ:COL_BQ + 3 * B_W])
        p_delta.append(delta_new)
        p_ffn.append(ffn_last[:, SUBLANES - (FFN_CONV - 1):])

        main_s = _inproj(xs, attn_nw, w_main, l, ms, 512)
        out_a_s, va_s = _gmlp(main_s, lnw, lnb, wmix_s, bcol_s, l, rows=ms, chunk=A_CHUNK, causal_mask=False,
                              want_v=True)
        main_bm = to_bm(main_s)
        main_pad = jnp.pad(main_bm, ((0, 0), (0, ts_pad_delta - ts), (0, 0))).reshape(bs * ts_pad_delta, N_MAIN)
        conv_s = jnp.pad(state_conv_b[l], ((0, 0), (conv_head - (B_CONV - 1), 0), (0, 0)))
        out_b_s, delta_new_s = _delta(main_pad, conv_s, b_conv_w, alog, dtb, b_nw, state_delta[l], l,
                                      nb=bs, t_total=ts_pad_delta, t_valid=ts, tb=ts_pad_delta)
        out_b_s = to_tm(out_b_s.reshape(bs, ts_pad_delta, B_W)[:, :ts])
        q_bm = main_bm[:, :, COL_CQ:COL_CQ + C_W].reshape(ms, C_W)
        kn_bm = main_bm[:, :, COL_CK:COL_CK + C_W]
        vn_bm = main_bm[:, :, COL_CV:COL_CV + C_W]
        out_c_s = _attn_sample(q_bm, kn_bm.reshape(ms, C_W), vn_bm.reshape(ms, C_W), ck_flat, cv_flat, pt_flat,
                               tbl, lamp, c_nw, l, nb=bs, ts=ts, n_pool=n_pool, lam_init=lam_init)
        out_c_s = to_tm(out_c_s.reshape(bs, ts, C_W)).astype(BF16)
        xs, hn2_s = _outproj(out_b_s, out_c_s, out_a_s, w_out_r, xs, ffn_nw, l, ms)
        ffn_state_s = jnp.swapaxes(state_ffn_conv[l], 0, 1).reshape(1, (FFN_CONV - 1) * bs, dff)
        ffn_state_s = jnp.pad(ffn_state_s, ((0, 0), (ffn_head_s - (FFN_CONV - 1) * bs, 0), (0, 0)))
        hid_s, ffn_last_s = _ffn_up(hn2_s, wg, wu, ffn_conv_w, ffn_cb, ffn_state_s, l, tm=ms, tn=512, stride=bs,
                                    blocks=1)
        xs = _ffn_down(hid_s, wd, xs, l, ms, 512)
        s_k.append(kn_bm.reshape(bs, ts, C_HEADS, HEAD_DIM))
        s_v.append(vn_bm.reshape(bs, ts, C_HEADS, HEAD_DIM))
        s_conv.append(main_bm[:, ts - (B_CONV - 1):, COL_BQ:COL_BQ + 3 * B_W])
        s_delta.append(delta_new_s)
        s_ffn.append(jnp.swapaxes(ffn_last_s[0, ffn_head_s - (FFN_CONV - 1) * bs:].reshape(FFN_CONV - 1, bs, dff),
                                  0, 1))
        s_av.append(to_bm(va_s).reshape(bs, ts, A_HEADS, HEAD_DIM))

    y_prompt = _rmsnorm(xp, final_norm_w, _pick(mp, 512)).reshape(bp, tp, d)
    y_sample = jnp.swapaxes(_rmsnorm(xs, final_norm_w, ms).reshape(ts, bs, d), 0, 1)
    k_prompt, v_prompt = (jnp.swapaxes(a, 2, 3) for a in kv_stacks)
    return (y_prompt, y_sample,
            k_prompt, v_prompt, jnp.stack(p_conv), jnp.stack(p_delta), jnp.stack(p_ffn),
            jnp.stack(s_k), jnp.stack(s_v), jnp.stack(s_conv), jnp.stack(s_delta), jnp.stack(s_ffn),
            jnp.stack(s_av))
```

```python
import functools
import math

import numpy as np
import jax
import jax.numpy as jnp
from jax import lax
from jax.experimental import pallas as pl
from jax.experimental.pallas import tpu as pltpu

F32 = jnp.float32
BF16 = jnp.bfloat16
HI = lax.Precision.HIGHEST

HEAD_DIM = 128
SUBLANES = 8
A_HEADS, B_HEADS, C_HEADS = 4, 6, 6
A_W, B_W, C_W = A_HEADS * HEAD_DIM, B_HEADS * HEAD_DIM, C_HEADS * HEAD_DIM
A_CHUNK = 128
B_CONV = 4
DN_CHUNK = 64
DN_PAIR = 2 * DN_CHUNK
C_HALF = HEAD_DIM // 2
N_BUCKETS = 32
MAX_DISTANCE = 128
FFN_CONV = 3
EPS = 1e-6
NEG_INF = -1e30
VMEM_LIMIT = 56 * 1024 * 1024
TM_IN, TM_OUT, TM_UP, TM_DOWN, TB_DELTA, TQ = 1024, 512, 1024, 1024, 256, 512
PAGES_PER_STEP = 16
LOG2E = math.log2(math.e)

COL_BQ, COL_BK, COL_BV, COL_BZ, COL_CQ = 0, B_W, 2 * B_W, 3 * B_W, 4 * B_W
COL_BA = COL_CQ + C_W
COL_AU = COL_BA + 2 * HEAD_DIM
COL_AV = COL_AU + A_W
COL_CK = COL_AV + A_W
COL_CV = COL_CK + C_W
N_MAIN = COL_CV + C_W


def _t5_bucket_starts():
    n = np.arange(4 * MAX_DISTANCE)
    max_exact = N_BUCKETS // 2
    nf = np.maximum(n, max_exact).astype(np.float64)
    large = max_exact + (np.log(nf / max_exact) / math.log(MAX_DISTANCE / max_exact)
                         * (N_BUCKETS - max_exact)).astype(np.int64)
    bucket = np.where(n < max_exact, n, np.minimum(large, N_BUCKETS - 1))
    return [int(n[bucket == b].min()) for b in range(N_BUCKETS)]


BUCKET_START = _t5_bucket_starts()
FAR_DISTANCE = BUCKET_START[-1]


def _cparams(*sem):
    return pltpu.CompilerParams(dimension_semantics=sem, vmem_limit_bytes=VMEM_LIMIT)


def _gelu(x):
    return 0.5 * x * (1.0 + jnp.tanh(math.sqrt(2.0 / math.pi) * (x + 0.044715 * (x * x * x))))


def _sigmoid(x):
    return 1.0 / (1.0 + jnp.exp(-x))


def _silu(x):
    return x * _sigmoid(x)


def _softplus(x):
    return jnp.maximum(x, 0.0) + jnp.log(1.0 + jnp.exp(-jnp.abs(x)))


def _rms(x, w):
    return x * lax.rsqrt(jnp.mean(x * x, axis=-1, keepdims=True) + EPS) * w


def _dot(a, b, **kw):
    return jnp.dot(a, b, preferred_element_type=F32, **kw)


def _dot_nt(a, b, **kw):
    return lax.dot_general(a, b, (((1,), (1,)), ((), ())), preferred_element_type=F32, **kw)


def _split2(x):
    hi = x.astype(BF16)
    return hi, (x - hi.astype(F32)).astype(BF16)


def _split3(x):
    hi = x.astype(BF16)
    r = x - hi.astype(F32)
    mid = r.astype(BF16)
    return hi, mid, (r - mid.astype(F32)).astype(BF16)


def _dot3(a_hl, b_hl):
    (ah, al), (bh, bl) = a_hl, b_hl
    return _dot(jnp.concatenate([ah, ah, al], axis=1), jnp.concatenate([bh, bl, bh], axis=0))


def _rmsnorm_kernel(x_ref, w_ref, o_ref):
    o_ref[...] = _rms(x_ref[...], w_ref[...]).astype(o_ref.dtype)


def _rmsnorm(x, w, tm):
    m, d = x.shape
    return pl.pallas_call(
        _rmsnorm_kernel, grid=(m // tm,),
        in_specs=[pl.BlockSpec((tm, d), lambda i: (i, 0)), pl.BlockSpec((1, d), lambda i: (0, 0))],
        out_specs=pl.BlockSpec((tm, d), lambda i: (i, 0)),
        out_shape=jax.ShapeDtypeStruct((m, d), F32),
        compiler_params=_cparams("parallel"), name="final_rmsnorm")(x, w.reshape(1, d))


def _inproj_kernel(x_ref, nw_ref, w_ref, o_ref, hn_s):
    @pl.when(pl.program_id(1) == 0)
    def _():
        hn_s[...] = _rms(x_ref[...], nw_ref[...]).astype(BF16)

    o_ref[...] = _dot(hn_s[...], w_ref[...])


def _inproj(x, nw, w, l, tm, tn):
    m, d = x.shape
    n = w.shape[-1]
    return pl.pallas_call(
        _inproj_kernel, grid=(m // tm, n // tn),
        in_specs=[pl.BlockSpec((tm, d), lambda i, j: (i, 0)),
                  pl.BlockSpec((None, 1, d), lambda i, j: (l, 0, 0)),
                  pl.BlockSpec((None, d, tn), lambda i, j: (l, 0, j))],
        out_specs=pl.BlockSpec((tm, tn), lambda i, j: (i, j)),
        out_shape=jax.ShapeDtypeStruct((m, n), F32),
        scratch_shapes=[pltpu.VMEM((tm, d), BF16)],
        compiler_params=_cparams("parallel", "arbitrary"), name="inproj")(x, nw, w)


def _gmlp_kernel(u_ref, v_ref, lnw_ref, lnb_ref, w_ref, bcol_ref, *out_refs, chunk, rows, causal_mask):
    o_ref = out_refs[0]
    nchunk = max(rows // chunk, 1)
    live = min(rows, chunk)
    if causal_mask:
        ii = lax.broadcasted_iota(jnp.int32, (chunk, chunk), 0)
        jj = lax.broadcasted_iota(jnp.int32, (chunk, chunk), 1)
        keep = ii >= jj
    for h in range(A_HEADS):
        sl = slice(h * HEAD_DIM, (h + 1) * HEAD_DIM)
        u = _gelu(u_ref[:, sl])
        v = _gelu(v_ref[:, sl])
        mu = jnp.mean(v, axis=-1, keepdims=True)
        vc = v - mu
        vn = vc * lax.rsqrt(jnp.mean(vc * vc, axis=-1, keepdims=True) + EPS) * lnw_ref[:, sl] + lnb_ref[:, sl]
        if len(out_refs) > 1:
            out_refs[1][:, sl] = vn
        w = w_ref[h]
        if causal_mask:
            w = jnp.where(keep, w, 0.0)
        wb = w.astype(BF16)
        vb = vn.astype(BF16)
        bcol = bcol_ref[0:live, h:h + 1]
        if live < chunk:
            vb = jnp.concatenate([vb, jnp.zeros((chunk - live, HEAD_DIM), BF16)], axis=0)
        for c in range(nchunk):
            rs = slice(c * live, (c + 1) * live)
            mixed = _dot(wb, vb[c * chunk:(c + 1) * chunk])[0:live] + bcol
            o_ref[rs, sl] = (u[rs] * mixed).astype(o_ref.dtype)


def _gmlp(main, lnw, lnb, wmix, bcol, l, *, rows, chunk, causal_mask, want_v):
    m = main.shape[0]
    live = min(rows, chunk)
    cu, cv = COL_AU // A_W, COL_AV // A_W
    out_shape = [jax.ShapeDtypeStruct((m, A_W), BF16)]
    out_specs = [pl.BlockSpec((rows, A_W), lambda i: (i, 0))]
    if want_v:
        out_shape.append(jax.ShapeDtypeStruct((m, A_W), F32))
        out_specs.append(pl.BlockSpec((rows, A_W), lambda i: (i, 0)))
    return pl.pallas_call(
        functools.partial(_gmlp_kernel, chunk=chunk, rows=rows, causal_mask=causal_mask),
        grid=(m // rows,),
        in_specs=[pl.BlockSpec((rows, A_W), lambda i: (i, cu)),
                  pl.BlockSpec((rows, A_W), lambda i: (i, cv)),
                  pl.BlockSpec((None, 1, A_W), lambda i: (l, 0, 0)),
                  pl.BlockSpec((None, 1, A_W), lambda i: (l, 0, 0)),
                  pl.BlockSpec((None, A_HEADS, chunk, chunk), lambda i: (l, 0, 0, 0)),
                  pl.BlockSpec((None, live, A_HEADS), lambda i: (l, 0, 0))],
        out_specs=out_specs, out_shape=out_shape,
        compiler_params=_cparams("parallel"), name="gmlp")(main, main, lnw, lnb, wmix, bcol)


def _delta_kernel(q_ref, k_ref, v_ref, z_ref, ba_ref, cs_ref, cw_ref, alog_ref, dtb_ref, nw_ref, s0_ref,
                  o_ref, s_ref, buf_s, qs_s, ks_s, vs_s, g_s, bt_s, gc_s, u_s, w_s, ai_s, qd_s, kd_s,
                  *, tb, t_valid, t_total):
    t = pl.program_id(1)
    head = SUBLANES

    @pl.when(t == 0)
    def _():
        s_ref[...] = s0_ref[...]
        for j in range(3):
            buf_s[j, 0:head, :] = cs_ref[:, j * B_W:(j + 1) * B_W]

    if t_valid < t_total:
        row = t * tb + lax.broadcasted_iota(jnp.int32, (tb, 1), 0)
        live = row < t_valid
    else:
        live = None

    def keep(x):
        return x if live is None else jnp.where(live, x, 0.0)

    for j, (src, dst) in enumerate(((q_ref, qs_s), (k_ref, ks_s), (v_ref, vs_s))):
        x = src[...]
        buf_s[j, head:head + tb, :] = x
        w = cw_ref[:, j * B_W:(j + 1) * B_W]
        y = w[B_CONV - 1:B_CONV] * x
        for tap in range(B_CONV - 1):
            off = head - (B_CONV - 1) + tap
            y = y + w[tap:tap + 1] * buf_s[j, off:off + tb, :]
        buf_s[j, 0:head, :] = buf_s[j, tb:tb + head, :]
        y = keep(_silu(y))
        for h in range(B_HEADS):
            sl = slice(h * HEAD_DIM, (h + 1) * HEAD_DIM)
            seg = y[:, sl]
            if j == 0:
                seg = seg * lax.rsqrt(jnp.sum(seg * seg, axis=-1, keepdims=True) + EPS) * (HEAD_DIM ** -0.5)
            elif j == 1:
                seg = seg * lax.rsqrt(jnp.sum(seg * seg, axis=-1, keepdims=True) + EPS)
            dst[h] = seg

    ba = ba_ref[...]
    bt_s[...] = keep(_sigmoid(ba))
    g_s[...] = keep(-jnp.exp(alog_ref[...]) * _softplus(ba + dtb_ref[...]))

    n = DN_PAIR
    ii = lax.broadcasted_iota(jnp.int32, (n, n), 0)
    jj = lax.broadcasted_iota(jnp.int32, (n, n), 1)
    same = (ii >= DN_CHUNK) == (jj >= DN_CHUNK)
    incl = same & (ii >= jj)
    strict = same & (ii > jj)
    tri = jnp.where(incl, 1.0, 0.0)
    eye = jnp.where(ii == jj, 1.0, 0.0)
    rowi = lax.broadcasted_iota(jnp.int32, (n, 1), 0)
    lane = lax.broadcasted_iota(jnp.int32, (n, n), 1)
    zeros_half = jnp.zeros((DN_CHUNK, HEAD_DIM), F32)

    tri3 = jnp.concatenate([tri.astype(BF16)] * 3, axis=1)

    heads = range(B_HEADS)
    gcol_of = lambda gc, h: gc[:, B_HEADS + h:B_HEADS + h + 1]

    def intra(p, carry):
        r0 = pl.multiple_of(p * n, n)
        rows = pl.ds(r0, n)
        gc = _dot(tri3, jnp.concatenate(_split3(g_s[rows, :]), axis=0))
        gct = gc.T
        bt = bt_s[rows, :]
        gc_s[rows, :] = gc
        q = [qs_s[h, rows, :] for h in heads]
        k = [ks_s[h, rows, :] for h in heads]
        decay = [jnp.exp(jnp.where(incl, gcol_of(gc, h) - gct[B_HEADS + h:B_HEADS + h + 1, :], NEG_INF))
                 for h in heads]
        kb = [k[h] * bt[:, h:h + 1] for h in heads]
        kq = [_dot_nt(jnp.concatenate([kb[h].astype(BF16), q[h].astype(BF16)], axis=0), k[h].astype(BF16))
              for h in heads]
        a = [-jnp.where(strict, kq[h][0:n] * decay[h], 0.0) for h in heads]
        pm = [eye + a[h] for h in heads]
        a_hl = [_split2(a[h]) for h in heads]
        a = [_dot3(a_hl[h], a_hl[h]) for h in heads]
        for _ in range(int(math.log2(DN_CHUNK)) - 2):
            a_hl = [_split2(a[h]) for h in heads]
            pm_hl = [_split2(pm[h]) for h in heads]
            prod = [_dot3(tuple(jnp.concatenate([x, y], axis=0) for x, y in zip(a_hl[h], pm_hl[h])), a_hl[h])
                    for h in heads]
            a = [prod[h][0:n] for h in heads]
            pm = [pm[h] + prod[h][n:2 * n] for h in heads]
        prod = [_dot3(_split2(pm[h]), _split2(a[h])) for h in heads]
        pm = [pm[h] + prod[h] for h in heads]
        egc = [jnp.exp(gcol_of(gc, h)) for h in heads]
        sol = [_dot3(_split2(pm[h]),
                     _split2(jnp.concatenate([vs_s[h, rows, :] * bt[:, h:h + 1], kb[h] * egc[h]], axis=1)))
               for h in heads]
        for h in heads:
            u_s[h, rows, :] = sol[h][:, 0:HEAD_DIM]
            w_s[h, rows, :] = sol[h][:, HEAD_DIM:2 * HEAD_DIM].astype(BF16)
            ai_s[h, rows, :] = (kq[h][n:2 * n] * decay[h]).astype(BF16)
            qd_s[h, rows, :] = (q[h] * egc[h]).astype(BF16)
            gcol = gcol_of(gc, h)
            glast = jnp.where(rowi < DN_CHUNK, gcol[DN_CHUNK - 1:DN_CHUNK], gcol[n - 1:n])
            kdt = (k[h] * jnp.exp(glast - gcol)).T
            kd_s[h, p, 0] = jnp.where(lane < DN_CHUNK, kdt, 0.0).astype(BF16)
            kd_s[h, p, 1] = jnp.where(lane >= DN_CHUNK, kdt, 0.0).astype(BF16)
        return carry

    def inter(p, carry):
        r0 = pl.multiple_of(p * n, n)
        s = [s_ref[h] for h in heads]
        outs = [[] for _ in heads]
        for c in range(2):
            rs = pl.ds(pl.multiple_of(r0 + c * DN_CHUNK, DN_CHUNK), DN_CHUNK)
            ws = [_dot(jnp.concatenate([w_s[h, rs, :], qd_s[h, rs, :]], axis=0), s[h].astype(BF16)) for h in heads]
            v_new = [u_s[h, rs, :] - ws[h][0:DN_CHUNK] for h in heads]
            v_pad = [jnp.concatenate([v_new[h], zeros_half] if c == 0 else [zeros_half, v_new[h]],
                                     axis=0).astype(BF16) for h in heads]
            av = [_dot(jnp.concatenate([ai_s[h, rs, :], kd_s[h, p, c]], axis=0), v_pad[h]) for h in heads]
            last = gc_s[pl.ds(pl.multiple_of(r0 + (c + 1) * DN_CHUNK - SUBLANES, SUBLANES), SUBLANES), :]
            for h in heads:
                outs[h].append(ws[h][DN_CHUNK:2 * DN_CHUNK] + av[h][0:DN_CHUNK])
                gl = jnp.exp(last[SUBLANES - 1:SUBLANES, B_HEADS + h:B_HEADS + h + 1])
                s[h] = s[h] * gl + av[h][DN_CHUNK:DN_CHUNK + HEAD_DIM]
        for h in heads:
            sl = slice(h * HEAD_DIM, (h + 1) * HEAD_DIM)
            s_ref[h] = s[h]
            o = jnp.concatenate(outs[h], axis=0)
            z = z_ref[pl.ds(r0, n), sl]
            o_ref[pl.ds(r0, n), sl] = (_rms(o, nw_ref[...]) * _silu(z)).astype(o_ref.dtype)
        return carry

    lax.fori_loop(0, tb // n, intra, 0)
    lax.fori_loop(0, tb // n, inter, 0)


def _delta(main, conv_state, conv_w, alog, dtb, nw, s0, l, *, nb, t_total, t_valid, tb):
    m = main.shape[0]
    nt = t_total // tb
    cq, ck, cv, cz, cba = COL_BQ // B_W, COL_BK // B_W, COL_BV // B_W, COL_BZ // B_W, COL_BA // HEAD_DIM
    row = lambda b, t: b * nt + t
    return pl.pallas_call(
        functools.partial(_delta_kernel, tb=tb, t_valid=t_valid, t_total=t_total),
        grid=(nb, nt),
        in_specs=[pl.BlockSpec((tb, B_W), lambda b, t: (row(b, t), cq)),
                  pl.BlockSpec((tb, B_W), lambda b, t: (row(b, t), ck)),
                  pl.BlockSpec((tb, B_W), lambda b, t: (row(b, t), cv)),
                  pl.BlockSpec((tb, B_W), lambda b, t: (row(b, t), cz)),
                  pl.BlockSpec((tb, HEAD_DIM), lambda b, t: (row(b, t), cba)),
                  pl.BlockSpec((None, SUBLANES, 3 * B_W), lambda b, t: (b, 0, 0)),
                  pl.BlockSpec((None, B_CONV, 3 * B_W), lambda b, t: (l, 0, 0)),
                  pl.BlockSpec((None, 1, HEAD_DIM), lambda b, t: (l, 0, 0)),
                  pl.BlockSpec((None, 1, HEAD_DIM), lambda b, t: (l, 0, 0)),
                  pl.BlockSpec((None, 1, HEAD_DIM), lambda b, t: (l, 0, 0)),
                  pl.BlockSpec((None, B_HEADS, HEAD_DIM, HEAD_DIM), lambda b, t: (b, 0, 0, 0))],
        out_specs=[pl.BlockSpec((tb, B_W), lambda b, t: (row(b, t), 0)),
                   pl.BlockSpec((None, B_HEADS, HEAD_DIM, HEAD_DIM), lambda b, t: (b, 0, 0, 0))],
        out_shape=[jax.ShapeDtypeStruct((m, B_W), BF16),
                   jax.ShapeDtypeStruct((nb, B_HEADS, HEAD_DIM, HEAD_DIM), F32)],
        scratch_shapes=[pltpu.VMEM((3, tb + SUBLANES, B_W), F32),
                        pltpu.VMEM((B_HEADS, tb, HEAD_DIM), F32),
                        pltpu.VMEM((B_HEADS, tb, HEAD_DIM), F32),
                        pltpu.VMEM((B_HEADS, tb, HEAD_DIM), F32),
                        pltpu.VMEM((tb, HEAD_DIM), F32),
                        pltpu.VMEM((tb, HEAD_DIM), F32),
                        pltpu.VMEM((tb, HEAD_DIM), F32),
                        pltpu.VMEM((B_HEADS, tb, HEAD_DIM), F32),
                        pltpu.VMEM((B_HEADS, tb, HEAD_DIM), BF16),
                        pltpu.VMEM((B_HEADS, tb, HEAD_DIM), BF16),
                        pltpu.VMEM((B_HEADS, tb, HEAD_DIM), BF16),
                        pltpu.VMEM((B_HEADS, tb // DN_PAIR, 2, HEAD_DIM, DN_PAIR), BF16)],
        compiler_params=_cparams("parallel", "arbitrary"), name="deltanet")(
            main, main, main, main, main, conv_state, conv_w, alog, dtb, nw, s0)


def _bias_from_rel(rel, tbl_ref, h):
    bias = jnp.full(rel.shape, tbl_ref[h], F32)
    for b in range(1, N_BUCKETS):
        bias = jnp.where(rel >= BUCKET_START[b], tbl_ref[b * C_HEADS + h], bias)
    return bias


def _lambda(lamp_ref, lam_init):
    lp = lamp_ref[...]
    s1 = jnp.sum(lp[0:1] * lp[1:2], axis=-1, keepdims=True)
    s2 = jnp.sum(lp[2:3] * lp[3:4], axis=-1, keepdims=True)
    return jnp.exp(s1) - jnp.exp(s2) + lam_init


def _split_halves(qh):
    lane = lax.broadcasted_iota(jnp.int32, qh.shape, 1)
    return jnp.where(lane < C_HALF, qh, 0.0), jnp.where(lane >= C_HALF, qh, 0.0)


def _softmax_update(m_ref, l_ref, acc_ref, idx, s, pv):
    chunks = [s[:, c * HEAD_DIM:(c + 1) * HEAD_DIM] for c in range(s.shape[1] // HEAD_DIM)]
    mx = chunks[0]
    for c in chunks[1:]:
        mx = jnp.maximum(mx, c)
    m_old = m_ref[idx]
    m_new = jnp.maximum(m_old, jnp.max(mx, axis=-1, keepdims=True))
    p = [jnp.exp2(c - m_new) for c in chunks]
    ps = p[0]
    for c in p[1:]:
        ps = ps + c
    a = jnp.exp2(m_old - m_new)
    l_ref[idx] = a * l_ref[idx] + jnp.sum(ps, axis=-1, keepdims=True)
    acc_ref[idx] = a * acc_ref[idx] + pv(jnp.concatenate([c.astype(BF16) for c in p], axis=1))
    m_ref[idx] = m_new


def _prompt_attn_step(h, b, qi, tbl_ref, q_ref, k_ref, v_ref, lamp_ref, nw_ref, o_ref, ko_ref, vo_ref,
                      kb_s, vb_s, bias_s, m_s, l_s, acc_s, *, tq, lam_init):
    far_bias = tbl_ref[(N_BUCKETS - 1) * C_HEADS + h]

    @pl.when((qi == 0) & (b == 0))
    def _():
        r = lax.broadcasted_iota(jnp.int32, (tq, tq), 0) - lax.broadcasted_iota(jnp.int32, (tq, tq), 1)
        bias_s[0] = jnp.where(r >= 0, (_bias_from_rel(r, tbl_ref, h) - far_bias) * LOG2E, NEG_INF)
        bias_s[1] = (_bias_from_rel(r + tq, tbl_ref, h) - far_bias) * LOG2E

    @pl.when(qi == 0)
    def _():
        kf = k_ref[...]
        vf = v_ref[...]
        ko_ref[...] = kf
        vo_ref[...] = vf
        kb_s[...] = kf.astype(BF16)
        vb_s[...] = vf.astype(BF16)

    m_s[...] = jnp.full(m_s.shape, NEG_INF, F32)
    l_s[...] = jnp.zeros(l_s.shape, F32)
    acc_s[...] = jnp.zeros(acc_s.shape, F32)
    q1, q2 = _split_halves(q_ref[...] * (C_HALF ** -0.5 * LOG2E))
    q1 = q1.astype(BF16)
    q2 = q2.astype(BF16)

    def block(kstart, bias):
        kb = kb_s[pl.ds(kstart, tq), :]
        vb = vb_s[pl.ds(kstart, tq), :]
        scores = [_dot_nt(qq, kb) for qq in (q1, q2)]
        for idx, s in enumerate(scores):
            if bias is not None:
                s = s + bias
            _softmax_update(m_s, l_s, acc_s, idx, s, lambda p: _dot(p, vb))

    def far(kj, carry):
        block(pl.multiple_of(kj * tq, tq), None)
        return carry

    lax.fori_loop(0, jnp.maximum(qi - 1, 0), far, 0)

    @pl.when(qi >= 1)
    def _():
        block(pl.multiple_of((qi - 1) * tq, tq), bias_s[1])

    block(pl.multiple_of(qi * tq, tq), bias_s[0])

    lam = _lambda(lamp_ref, lam_init)
    o = acc_s[0] / l_s[0] - lam * (acc_s[1] / l_s[1])
    o_ref[...] = (_rms(o, nw_ref[...]) * (1.0 - lam_init)).astype(o_ref.dtype)


N_PROMPT_IN, N_SAMPLE_IN, N_PROMPT_SCRATCH = 5, 3, 6


def _attn_kernel(pt_ref, tbl_ref, *refs, n_alias, pages, nb, nq, steps_s, chunks_s, tq, ts, page, lam_init):
    del pt_ref
    q_ref, k_ref, v_ref, lamp_ref, nw_ref = refs[:N_PROMPT_IN]
    refs = refs[N_PROMPT_IN + n_alias:]
    qs_ref, kn_ref, vn_ref = refs[:N_SAMPLE_IN]
    k_refs = refs[N_SAMPLE_IN:N_SAMPLE_IN + pages]
    v_refs = refs[N_SAMPLE_IN + pages:N_SAMPLE_IN + 2 * pages]
    o_ref, ko_ref, vo_ref, os_ref = refs[N_SAMPLE_IN + 2 * pages:N_SAMPLE_IN + 2 * pages + 4]
    scratch = refs[N_SAMPLE_IN + 2 * pages + 4:]
    h, b, qi = pl.program_id(0), pl.program_id(1), pl.program_id(2)
    _prompt_attn_step(h, b, qi, tbl_ref, q_ref, k_ref, v_ref, lamp_ref, nw_ref, o_ref, ko_ref, vo_ref,
                      *scratch[:N_PROMPT_SCRATCH], tq=tq, lam_init=lam_init)
    step = (h * nb + b) * nq + qi

    @pl.when(step < steps_s)
    def _():
        _sample_attn_step(lax.rem(step, chunks_s), chunks_s, tbl_ref, qs_ref, kn_ref, vn_ref, lamp_ref, nw_ref,
                          k_refs, v_refs, os_ref, *scratch[N_PROMPT_SCRATCH:], ts=ts, page=page, lam_init=lam_init)


def _attention(main, q_bm, kn_bm, vn_bm, cache_k, cache_v, page_table, tbl, lamp, nw, l, kv_stacks, *,
               depth, nb, t_total, tq, nb_s, ts, n_pool, lam_init):
    assert tq > FAR_DISTANCE
    m = main.shape[0]
    nq = t_total // tq
    page = cache_k.shape[2]
    n_pages = page_table.shape[0] // nb_s
    pages = math.gcd(n_pages, PAGES_PER_STEP)
    chunks_s = n_pages // pages
    steps_s = nb_s * chunks_s
    assert pages * page > FAR_DISTANCE + ts
    assert steps_s <= C_HEADS * nb * nq
    cq, ck, cv = COL_CQ // HEAD_DIM, COL_CK // HEAD_DIM, COL_CV // HEAD_DIM

    def s_step(h, b, i):
        return jnp.minimum((h * nb + b) * nq + i, steps_s - 1)

    def kv_page(k):
        def index(h, b, i, pt):
            s = s_step(h, b, i)
            return (l * n_pool + pt[(s // chunks_s) * n_pages + (s % chunks_s) * pages + k], 0, 0, 0)
        return pl.BlockSpec((None, C_HEADS, page, HEAD_DIM), index)

    tok = pl.BlockSpec((ts, C_W), lambda h, b, i, pt: (s_step(h, b, i) // chunks_s, 0))
    kv_shape = jax.ShapeDtypeStruct((depth, nb, C_HEADS, t_total, HEAD_DIM), F32)
    kv_spec = pl.BlockSpec((None, None, None, t_total, HEAD_DIM), lambda h, b, i, pt: (l, b, h, 0, 0))
    in_specs = [pl.BlockSpec(memory_space=pltpu.SMEM),
                pl.BlockSpec((tq, HEAD_DIM), lambda h, b, i, pt: (b * nq + i, cq + h)),
                pl.BlockSpec((t_total, HEAD_DIM), lambda h, b, i, pt: (b, ck + h)),
                pl.BlockSpec((t_total, HEAD_DIM), lambda h, b, i, pt: (b, cv + h)),
                pl.BlockSpec((None, 4, C_HALF), lambda h, b, i, pt: (l, 0, 0)),
                pl.BlockSpec((None, 1, HEAD_DIM), lambda h, b, i, pt: (l, 0, 0))]
    args = [tbl, main, main, main, lamp, nw]
    aliases = {}
    if kv_stacks is not None:
        aliases = {1 + len(args): 1, 2 + len(args): 2}
        in_specs += [pl.BlockSpec(memory_space=pl.ANY)] * 2
        args += list(kv_stacks)
    in_specs += [tok, tok, tok] + [kv_page(k) for k in range(pages)] + [kv_page(k) for k in range(pages)]
    args += [q_bm, kn_bm, vn_bm] + [cache_k] * pages + [cache_v] * pages
    rows_s = C_HEADS * 2 * ts
    grid_spec = pltpu.PrefetchScalarGridSpec(
        num_scalar_prefetch=1, grid=(C_HEADS, nb, nq), in_specs=in_specs,
        out_specs=[pl.BlockSpec((tq, HEAD_DIM), lambda h, b, i, pt: (b * nq + i, h)), kv_spec, kv_spec, tok],
        scratch_shapes=[pltpu.VMEM((t_total, HEAD_DIM), BF16),
                        pltpu.VMEM((t_total, HEAD_DIM), BF16),
                        pltpu.VMEM((2, tq, tq), F32),
                        pltpu.VMEM((2, tq, HEAD_DIM), F32),
                        pltpu.VMEM((2, tq, HEAD_DIM), F32),
                        pltpu.VMEM((2, tq, HEAD_DIM), F32),
                        pltpu.VMEM((rows_s, HEAD_DIM), F32),
                        pltpu.VMEM((rows_s, HEAD_DIM), F32),
                        pltpu.VMEM((rows_s, HEAD_DIM), F32),
                        pltpu.VMEM((page, C_W), F32),
                        pltpu.VMEM((page, C_W), F32)])
    return pl.pallas_call(
        functools.partial(_attn_kernel, n_alias=len(aliases), pages=pages, nb=nb, nq=nq, steps_s=steps_s,
                          chunks_s=chunks_s, tq=tq, ts=ts, page=page, lam_init=lam_init),
        grid_spec=grid_spec,
        out_shape=[jax.ShapeDtypeStruct((m, C_W), BF16), kv_shape, kv_shape,
                   jax.ShapeDtypeStruct((nb_s * ts, C_W), F32)],
        input_output_aliases=aliases,
        compiler_params=_cparams("arbitrary", "arbitrary", "arbitrary"), name="attention")(page_table, *args)


def _sample_attn_step(s_id, nsteps, tbl_ref, q_ref, kn_ref, vn_ref, lamp_ref, nw_ref, k_refs, v_refs, o_ref,
                      m_s, l_s, acc_s, kn_s, vn_s, *, ts, page, lam_init):
    width = len(k_refs) * page
    rows = 2 * ts

    @pl.when(s_id == 0)
    def _():
        m_s[...] = jnp.full(m_s.shape, NEG_INF, F32)
        l_s[...] = jnp.zeros(l_s.shape, F32)
        acc_s[...] = jnp.zeros(acc_s.shape, F32)

    q = q_ref[...] * (C_HALF ** -0.5 * LOG2E)
    qq = []
    for h in range(C_HEADS):
        q1, q2 = _split_halves(q[:, h * HEAD_DIM:(h + 1) * HEAD_DIM])
        qq.append(jnp.concatenate([q1, q2], axis=0).astype(BF16))

    def update(keys, vals, bias):
        s = jnp.concatenate([_dot_nt(qq[h], keys(h)) for h in range(C_HEADS)], axis=0)
        if bias is not None:
            s = s + bias

        def pv(p):
            return jnp.concatenate([_dot(p[h * rows:(h + 1) * rows], vals(h)) for h in range(C_HEADS)], axis=0)

        _softmax_update(m_s, l_s, acc_s, slice(None), s, pv)

    def past_keys(h):
        return jnp.concatenate([r[h].astype(BF16) for r in k_refs], axis=0)

    def past_vals(h):
        return jnp.concatenate([r[h].astype(BF16) for r in v_refs], axis=0)

    @pl.when(s_id < nsteps - 1)
    def _():
        update(past_keys, past_vals, None)

    @pl.when(s_id == nsteps - 1)
    def _():
        tq_pos = lax.rem(lax.broadcasted_iota(jnp.int32, (rows, width), 0), ts)
        rel_past = tq_pos + width - lax.broadcasted_iota(jnp.int32, (rows, width), 1)
        rel_new = (lax.rem(lax.broadcasted_iota(jnp.int32, (rows, page), 0), ts)
                   - lax.broadcasted_iota(jnp.int32, (rows, page), 1))
        bias_past, bias_new = [], []
        for h in range(C_HEADS):
            far_bias = tbl_ref[(N_BUCKETS - 1) * C_HEADS + h]
            bias_past.append((_bias_from_rel(rel_past, tbl_ref, h) - far_bias) * LOG2E)
            bias_new.append(jnp.where(rel_new >= 0, (_bias_from_rel(rel_new, tbl_ref, h) - far_bias) * LOG2E, NEG_INF))
        update(past_keys, past_vals, jnp.concatenate(bias_past, axis=0))
        kn_s[...] = jnp.zeros(kn_s.shape, F32)
        vn_s[...] = jnp.zeros(vn_s.shape, F32)
        kn_s[0:ts, :] = kn_ref[...]
        vn_s[0:ts, :] = vn_ref[...]
        update(lambda h: kn_s[:, h * HEAD_DIM:(h + 1) * HEAD_DIM].astype(BF16),
               lambda h: vn_s[:, h * HEAD_DIM:(h + 1) * HEAD_DIM].astype(BF16),
               jnp.concatenate(bias_new, axis=0))
        lam = _lambda(lamp_ref, lam_init)
        on = acc_s[...] / l_s[...]
        for h in range(C_HEADS):
            o = on[h * rows:h * rows + ts] - lam * on[h * rows + ts:(h + 1) * rows]
            o_ref[:, h * HEAD_DIM:(h + 1) * HEAD_DIM] = (_rms(o, nw_ref[...]) * (1.0 - lam_init)).astype(o_ref.dtype)


def _outproj_kernel(ob_ref, oc_ref, oa_ref, wb_ref, wc_ref, wa_ref, x_ref, nw_ref, xo_ref, ho_ref):
    acc = _dot(ob_ref[...], wb_ref[...]) + _dot(oc_ref[...], wc_ref[...]) + _dot(oa_ref[...], wa_ref[...])
    x = x_ref[...] + acc
    xo_ref[...] = x
    ho_ref[...] = _rms(x, nw_ref[...]).astype(ho_ref.dtype)


def _outproj(ob, oc, oa, w, x, nw, l, tm):
    m, d = x.shape
    return pl.pallas_call(
        _outproj_kernel, grid=(m // tm,),
        in_specs=[pl.BlockSpec((tm, B_W), lambda i: (i, 0)),
                  pl.BlockSpec((tm, C_W), lambda i: (i, 0)),
                  pl.BlockSpec((tm, A_W), lambda i: (i, 0)),
                  pl.BlockSpec((None, B_W, d), lambda i: (l, 0, 0)),
                  pl.BlockSpec((None, C_W, d), lambda i: (l, 1, 0)),
                  pl.BlockSpec((None, A_W, d), lambda i: (l, (B_W + C_W) // A_W, 0)),
                  pl.BlockSpec((tm, d), lambda i: (i, 0)),
                  pl.BlockSpec((None, 1, d), lambda i: (l, 0, 0))],
        out_specs=[pl.BlockSpec((tm, d), lambda i: (i, 0)), pl.BlockSpec((tm, d), lambda i: (i, 0))],
        out_shape=[jax.ShapeDtypeStruct((m, d), F32), jax.ShapeDtypeStruct((m, d), BF16)],
        compiler_params=_cparams("parallel"), name="outproj")(ob, oc, oa, w, w, w, x, nw)


def _ffn_up_kernel(h_ref, wg_ref, wu_ref, cw_ref, cb_ref, st_ref, o_ref, so_ref, buf_s, wg_s, wu_s,
                   *, tm, stride, head, blocks):
    i = pl.program_id(1)

    @pl.when(i == 0)
    def _():
        wg_s[...] = wg_ref[...].astype(BF16)
        wu_s[...] = wu_ref[...].astype(BF16)

    @pl.when(i % blocks == 0)
    def _():
        buf_s[0:head, :] = st_ref[...]

    hn = h_ref[...]
    g = _dot(hn, wg_s[...])
    u = _dot(hn, wu_s[...])
    buf_s[head:head + tm, :] = g
    y = cw_ref[FFN_CONV - 1:FFN_CONV, :] * g + cb_ref[...]
    for tap in range(FFN_CONV - 1):
        off = head - (FFN_CONV - 1 - tap) * stride
        y = y + cw_ref[tap:tap + 1, :] * buf_s[off:off + tm, :]
    o_ref[...] = (_gelu(y) * u).astype(o_ref.dtype)
    last = buf_s[tm:tm + head, :]
    so_ref[...] = last
    buf_s[0:head, :] = last


def _ffn_up(hn, wg, wu, cw, cb, state, l, *, tm, tn, stride, blocks):
    m, d = hn.shape
    dff = wg.shape[-1]
    head = state.shape[1]
    nseq = state.shape[0]
    return pl.pallas_call(
        functools.partial(_ffn_up_kernel, tm=tm, stride=stride, head=head, blocks=blocks),
        grid=(dff // tn, m // tm),
        in_specs=[pl.BlockSpec((tm, d), lambda j, i: (i, 0)),
                  pl.BlockSpec((None, d, tn), lambda j, i: (l, 0, j)),
                  pl.BlockSpec((None, d, tn), lambda j, i: (l, 0, j)),
                  pl.BlockSpec((None, FFN_CONV, tn), lambda j, i: (l, 0, j)),
                  pl.BlockSpec((None, 1, tn), lambda j, i: (l, 0, j)),
                  pl.BlockSpec((None, head, tn), lambda j, i: (i // blocks, 0, j))],
        out_specs=[pl.BlockSpec((tm, tn), lambda j, i: (i, j)),
                   pl.BlockSpec((None, head, tn), lambda j, i: (i // blocks, 0, j))],
        out_shape=[jax.ShapeDtypeStruct((m, dff), BF16), jax.ShapeDtypeStruct((nseq, head, dff), F32)],
        scratch_shapes=[pltpu.VMEM((tm + head, tn), F32), pltpu.VMEM((d, tn), BF16), pltpu.VMEM((d, tn), BF16)],
        compiler_params=_cparams("parallel", "arbitrary"), name="ffn_up")(hn, wg, wu, cw, cb, state)


def _ffn_down_kernel(h_ref, w_ref, x_ref, o_ref):
    o_ref[...] = x_ref[...] + _dot(h_ref[...], w_ref[...])


def _ffn_down(hid, w, x, l, tm, tn):
    m, dff = hid.shape
    d = x.shape[1]
    return pl.pallas_call(
        _ffn_down_kernel, grid=(m // tm, d // tn),
        in_specs=[pl.BlockSpec((tm, dff), lambda i, j: (i, 0)),
                  pl.BlockSpec((None, dff, tn), lambda i, j: (l, 0, j)),
                  pl.BlockSpec((tm, tn), lambda i, j: (i, j))],
        out_specs=pl.BlockSpec((tm, tn), lambda i, j: (i, j)),
        out_shape=jax.ShapeDtypeStruct((m, d), F32),
        compiler_params=_cparams("parallel", "parallel"), name="ffn_down")(hid, w, x)


def _pick(total, want):
    t = min(total, want)
    while total % t:
        t //= 2
    return t


def kernel(x_prompt, x_sample, cache_k, cache_v, state_conv_b, state_delta, state_ffn_conv, page_table, attn_norm_w, w_in, a_ln_w, a_ln_b, a_ws, a_bs, b_conv_w, b_a_log, b_dt_bias, b_norm_w, c_lam_q1, c_lam_k1, c_lam_q2, c_lam_k2, c_norm_w, rel_bias, w_out, ffn_norm_w, w_gate, w_up, ffn_conv_w, ffn_conv_b, w_down, final_norm_w):
    bp, tp, d = x_prompt.shape
    bs, ts, _ = x_sample.shape
    depth = w_in.shape[0]
    dff = w_gate.shape[-1]
    n_pool, page = cache_k.shape[1], cache_k.shape[2]
    mp, ms = bp * tp, bs * ts

    o_au, o_av, o_qkv, o_z, o_beta = 0, A_W, 2 * A_W, 2 * A_W + 3 * B_W, 2 * A_W + 4 * B_W
    o_cq = o_beta + 2 * B_HEADS
    o_ck, o_cv = o_cq + C_W, o_cq + 2 * C_W
    w_main = jnp.concatenate([
        w_in[:, :, o_qkv:o_z], w_in[:, :, o_z:o_beta], w_in[:, :, o_cq:o_ck], w_in[:, :, o_beta:o_cq],
        jnp.zeros((depth, d, COL_AU - COL_BA - 2 * B_HEADS), w_in.dtype),
        w_in[:, :, o_au:o_qkv], w_in[:, :, o_ck:o_cv + C_W]], axis=-1).astype(BF16)
    w_out_r = jnp.concatenate([w_out[:, A_W:], w_out[:, :A_W]], axis=1).astype(BF16)
    wg, wu = w_gate, w_up
    wd = w_down.astype(BF16)
    attn_nw = attn_norm_w.reshape(depth, 1, d)
    ffn_nw = ffn_norm_w.reshape(depth, 1, d)
    lnw = a_ln_w.reshape(depth, 1, A_W)
    lnb = a_ln_b.reshape(depth, 1, A_W)
    bcol_p = jnp.swapaxes(a_bs, 1, 2)
    tril = jnp.tril(jnp.ones((ts, ts), F32))
    assert ms <= A_CHUNK
    wmix_s = jnp.einsum('lhts,bc->lhtbsc', a_ws[:, :, :ts, :ts] * tril, jnp.eye(bs, dtype=F32)).reshape(
        depth, A_HEADS, ms, ms)
    wmix_s = jnp.pad(wmix_s, ((0, 0), (0, 0), (0, A_CHUNK - ms), (0, A_CHUNK - ms)))
    bcol_s = jnp.repeat(jnp.swapaxes(a_bs[:, :, :ts], 1, 2), bs, axis=1)
    pad_lanes = jnp.zeros((depth, 1, HEAD_DIM - 2 * B_HEADS), F32)
    alog = jnp.concatenate([jnp.zeros((depth, 1, B_HEADS), F32), b_a_log.reshape(depth, 1, B_HEADS), pad_lanes], -1)
    dtb = jnp.concatenate([jnp.zeros((depth, 1, B_HEADS), F32), b_dt_bias.reshape(depth, 1, B_HEADS), pad_lanes], -1)
    b_nw = b_norm_w.reshape(depth, 1, HEAD_DIM)
    c_nw = c_norm_w.reshape(depth, 1, HEAD_DIM)
    lamp = jnp.stack([c_lam_q1, c_lam_k1, c_lam_q2, c_lam_k2], axis=1)
    tbl = rel_bias.reshape(-1)
    ffn_cb = ffn_conv_b.reshape(depth, 1, dff)
    ck_flat = jnp.swapaxes(cache_k, 2, 3).reshape(depth * n_pool, C_HEADS, page, HEAD_DIM)
    cv_flat = jnp.swapaxes(cache_v, 2, 3).reshape(depth * n_pool, C_HEADS, page, HEAD_DIM)
    pt_flat = page_table.reshape(-1)

    conv_head = SUBLANES
    zero_conv_p = jnp.zeros((bp, conv_head, 3 * B_W), F32)
    zero_delta_p = jnp.zeros((bp, B_HEADS, HEAD_DIM, HEAD_DIM), F32)
    zero_ffn_p = jnp.zeros((bp, SUBLANES, dff), F32)
    ffn_head_s = max(SUBLANES, (FFN_CONV - 1) * bs)

    tm_in = _pick(mp, TM_IN)
    tm_out = _pick(mp, TM_OUT)
    tm_up = _pick(tp, TM_UP)
    tm_down = _pick(mp, TM_DOWN)
    tb_delta = _pick(tp, TB_DELTA)
    tq = _pick(tp, TQ)
    ts_pad_delta = DN_PAIR

    xp = x_prompt.reshape(mp, d)
    xs = jnp.swapaxes(x_sample, 0, 1).reshape(ms, d)

    def to_bm(a):
        return jnp.swapaxes(a.reshape(ts, bs, -1), 0, 1)

    def to_tm(a):
        return jnp.swapaxes(a, 0, 1).reshape(ms, -1)

    kv_stacks = None
    p_conv, p_delta, p_ffn = [], [], []
    s_k, s_v, s_conv, s_delta, s_ffn, s_av = [], [], [], [], [], []
    for l in range(depth):
        lam_init = 0.8 - 0.6 * math.exp(-0.3 * l)

        main = _inproj(xp, attn_nw, w_main, l, tm_in, 512)
        main_s = _inproj(xs, attn_nw, w_main, l, ms, 512)
        main_bm = to_bm(main_s)
        q_bm = main_bm[:, :, COL_CQ:COL_CQ + C_W].reshape(ms, C_W)
        kn_bm = main_bm[:, :, COL_CK:COL_CK + C_W]
        vn_bm = main_bm[:, :, COL_CV:COL_CV + C_W]
        out_c, k_stack, v_stack, out_c_s = _attention(
            main, q_bm, kn_bm.reshape(ms, C_W), vn_bm.reshape(ms, C_W), ck_flat, cv_flat, pt_flat, tbl, lamp, c_nw,
            l, kv_stacks, depth=depth, nb=bp, t_total=tp, tq=tq, nb_s=bs, ts=ts, n_pool=n_pool, lam_init=lam_init)
        kv_stacks = (k_stack, v_stack)

        (out_a,) = _gmlp(main, lnw, lnb, a_ws, bcol_p, l, rows=4 * A_CHUNK, chunk=A_CHUNK, causal_mask=True,
                         want_v=False)
        out_b, delta_new = _delta(main, zero_conv_p, b_conv_w, alog, dtb, b_nw, zero_delta_p, l,
                                  nb=bp, t_total=tp, t_valid=tp, tb=tb_delta)
        xp, hn2 = _outproj(out_b, out_c, out_a, w_out_r, xp, ffn_nw, l, tm_out)
        hid, ffn_last = _ffn_up(hn2, wg, wu, ffn_conv_w, ffn_cb, zero_ffn_p, l, tm=tm_up, tn=512, stride=1,
                                blocks=tp // tm_up)
        xp = _ffn_down(hid, wd, xp, l, tm_down, 512)
        main3 = main.reshape(bp, tp, N_MAIN)
        p_conv.append(main3[:, tp - (B_CONV - 1):, COL_BQ:COL_BQ + 3 * B_W])
        p_delta.append(delta_new)
        p_ffn.append(ffn_last[:, SUBLANES - (FFN_CONV - 1):])

        out_a_s, va_s = _gmlp(main_s, lnw, lnb, wmix_s, bcol_s, l, rows=ms, chunk=A_CHUNK, causal_mask=False,
                              want_v=True)
        main_pad = jnp.pad(main_bm, ((0, 0), (0, ts_pad_delta - ts), (0, 0))).reshape(bs * ts_pad_delta, N_MAIN)
        conv_s = jnp.pad(state_conv_b[l], ((0, 0), (conv_head - (B_CONV - 1), 0), (0, 0)))
        out_b_s, delta_new_s = _delta(main_pad, conv_s, b_conv_w, alog, dtb, b_nw, state_delta[l], l,
                                      nb=bs, t_total=ts_pad_delta, t_valid=ts, tb=ts_pad_delta)
        out_b_s = to_tm(out_b_s.reshape(bs, ts_pad_delta, B_W)[:, :ts])
        out_c_s = to_tm(out_c_s.reshape(bs, ts, C_W)).astype(BF16)
        xs, hn2_s = _outproj(out_b_s, out_c_s, out_a_s, w_out_r, xs, ffn_nw, l, ms)
        ffn_state_s = jnp.swapaxes(state_ffn_conv[l], 0, 1).reshape(1, (FFN_CONV - 1) * bs, dff)
        ffn_state_s = jnp.pad(ffn_state_s, ((0, 0), (ffn_head_s - (FFN_CONV - 1) * bs, 0), (0, 0)))
        hid_s, ffn_last_s = _ffn_up(hn2_s, wg, wu, ffn_conv_w, ffn_cb, ffn_state_s, l, tm=ms, tn=512, stride=bs,
                                    blocks=1)
        xs = _ffn_down(hid_s, wd, xs, l, ms, 512)
        s_k.append(kn_bm.reshape(bs, ts, C_HEADS, HEAD_DIM))
        s_v.append(vn_bm.reshape(bs, ts, C_HEADS, HEAD_DIM))
        s_conv.append(main_bm[:, ts - (B_CONV - 1):, COL_BQ:COL_BQ + 3 * B_W])
        s_delta.append(delta_new_s)
        s_ffn.append(jnp.swapaxes(ffn_last_s[0, ffn_head_s - (FFN_CONV - 1) * bs:].reshape(FFN_CONV - 1, bs, dff),
                                  0, 1))
        s_av.append(to_bm(va_s).reshape(bs, ts, A_HEADS, HEAD_DIM))

    y_prompt = _rmsnorm(xp, final_norm_w, _pick(mp, 512)).reshape(bp, tp, d)
    y_sample = jnp.swapaxes(_rmsnorm(xs, final_norm_w, ms).reshape(ts, bs, d), 0, 1)
    k_prompt, v_prompt = (jnp.swapaxes(a, 2, 3) for a in kv_stacks)
    return (y_prompt, y_sample,
            k_prompt, v_prompt, jnp.stack(p_conv), jnp.stack(p_delta), jnp.stack(p_ffn),
            jnp.stack(s_k), jnp.stack(s_v), jnp.stack(s_conv), jnp.stack(s_delta), jnp.stack(s_ffn),
            jnp.stack(s_av))
```

```python
import functools
import math

import numpy as np
import jax
import jax.numpy as jnp
from jax import lax
from jax.experimental import pallas as pl
from jax.experimental.pallas import tpu as pltpu

F32 = jnp.float32
BF16 = jnp.bfloat16
HI = lax.Precision.HIGHEST

HEAD_DIM = 128
SUBLANES = 8
A_HEADS, B_HEADS, C_HEADS = 4, 6, 6
A_W, B_W, C_W = A_HEADS * HEAD_DIM, B_HEADS * HEAD_DIM, C_HEADS * HEAD_DIM
A_CHUNK = 128
B_CONV = 4
DN_CHUNK = 64
DN_PAIR = 2 * DN_CHUNK
C_HALF = HEAD_DIM // 2
N_BUCKETS = 32
MAX_DISTANCE = 128
FFN_CONV = 3
EPS = 1e-6
NEG_INF = -1e30
VMEM_LIMIT = 56 * 1024 * 1024
TM_IN, TM_OUT, TM_UP, TM_DOWN, TB_DELTA, TQ = 1024, 512, 1024, 1024, 256, 512
PAGES_PER_STEP = 16
LOG2E = math.log2(math.e)

COL_BQ, COL_BK, COL_BV, COL_BZ, COL_CQ = 0, B_W, 2 * B_W, 3 * B_W, 4 * B_W
COL_BA = COL_CQ + C_W
COL_AU = COL_BA + 2 * HEAD_DIM
COL_AV = COL_AU + A_W
COL_CK = COL_AV + A_W
COL_CV = COL_CK + C_W
N_MAIN = COL_CV + C_W


def _t5_bucket_starts():
    n = np.arange(4 * MAX_DISTANCE)
    max_exact = N_BUCKETS // 2
    nf = np.maximum(n, max_exact).astype(np.float64)
    large = max_exact + (np.log(nf / max_exact) / math.log(MAX_DISTANCE / max_exact)
                         * (N_BUCKETS - max_exact)).astype(np.int64)
    bucket = np.where(n < max_exact, n, np.minimum(large, N_BUCKETS - 1))
    return [int(n[bucket == b].min()) for b in range(N_BUCKETS)]


BUCKET_START = _t5_bucket_starts()
FAR_DISTANCE = BUCKET_START[-1]


def _cparams(*sem):
    return pltpu.CompilerParams(dimension_semantics=sem, vmem_limit_bytes=VMEM_LIMIT)


def _gelu(x):
    return 0.5 * x * (1.0 + jnp.tanh(math.sqrt(2.0 / math.pi) * (x + 0.044715 * (x * x * x))))


def _sigmoid(x):
    return 1.0 / (1.0 + jnp.exp(-x))


def _silu(x):
    return x * _sigmoid(x)


def _softplus(x):
    return jnp.maximum(x, 0.0) + jnp.log(1.0 + jnp.exp(-jnp.abs(x)))


def _rms(x, w):
    return x * lax.rsqrt(jnp.mean(x * x, axis=-1, keepdims=True) + EPS) * w


def _dot(a, b, **kw):
    return jnp.dot(a, b, preferred_element_type=F32, **kw)


def _dot_nt(a, b, **kw):
    return lax.dot_general(a, b, (((1,), (1,)), ((), ())), preferred_element_type=F32, **kw)


def _split2(x):
    hi = x.astype(BF16)
    return hi, (x - hi.astype(F32)).astype(BF16)


def _split3(x):
    hi = x.astype(BF16)
    r = x - hi.astype(F32)
    mid = r.astype(BF16)
    return hi, mid, (r - mid.astype(F32)).astype(BF16)


def _dot3(a_hl, b_hl):
    (ah, al), (bh, bl) = a_hl, b_hl
    return _dot(jnp.concatenate([ah, ah, al], axis=1), jnp.concatenate([bh, bl, bh], axis=0))


def _rmsnorm_kernel(x_ref, w_ref, o_ref):
    o_ref[...] = _rms(x_ref[...], w_ref[...]).astype(o_ref.dtype)


def _rmsnorm(x, w, tm):
    m, d = x.shape
    return pl.pallas_call(
        _rmsnorm_kernel, grid=(m // tm,),
        in_specs=[pl.BlockSpec((tm, d), lambda i: (i, 0)), pl.BlockSpec((1, d), lambda i: (0, 0))],
        out_specs=pl.BlockSpec((tm, d), lambda i: (i, 0)),
        out_shape=jax.ShapeDtypeStruct((m, d), F32),
        compiler_params=_cparams("parallel"), name="final_rmsnorm")(x, w.reshape(1, d))


def _inproj_kernel(x_ref, nw_ref, *refs, starts):
    w_refs, o_ref, hn_s = refs[:-2], refs[-2], refs[-1]
    j = pl.program_id(1)

    @pl.when(j == 0)
    def _():
        hn_s[...] = _rms(x_ref[...], nw_ref[...]).astype(BF16)

    for k, w_ref in enumerate(w_refs):
        @pl.when((j >= starts[k]) & (j < starts[k + 1]))
        def _():
            o_ref[...] = _dot(hn_s[...], w_ref[...])


def _inproj(x, nw, ws, l, tm, tn):
    m, d = x.shape
    starts = [0]
    for w in ws:
        starts.append(starts[-1] + w.shape[-1] // tn)

    def w_spec(k):
        lo, last = starts[k], starts[k + 1] - starts[k] - 1
        return pl.BlockSpec((None, d, tn), lambda i, j: (l, 0, jnp.clip(j - lo, 0, last)))

    return pl.pallas_call(
        functools.partial(_inproj_kernel, starts=tuple(starts)), grid=(m // tm, starts[-1]),
        in_specs=[pl.BlockSpec((tm, d), lambda i, j: (i, 0)),
                  pl.BlockSpec((None, 1, d), lambda i, j: (l, 0, 0))] + [w_spec(k) for k in range(len(ws))],
        out_specs=pl.BlockSpec((tm, tn), lambda i, j: (i, j)),
        out_shape=jax.ShapeDtypeStruct((m, starts[-1] * tn), F32),
        scratch_shapes=[pltpu.VMEM((tm, d), BF16)],
        compiler_params=_cparams("parallel", "arbitrary"), name="inproj")(x, nw, *ws)


def _gmlp_kernel(u_ref, v_ref, lnw_ref, lnb_ref, w_ref, bcol_ref, *out_refs, chunk, rows, causal_mask):
    o_ref = out_refs[0]
    nchunk = max(rows // chunk, 1)
    live = min(rows, chunk)
    if causal_mask:
        ii = lax.broadcasted_iota(jnp.int32, (chunk, chunk), 0)
        jj = lax.broadcasted_iota(jnp.int32, (chunk, chunk), 1)
        keep = ii >= jj
    for h in range(A_HEADS):
        sl = slice(h * HEAD_DIM, (h + 1) * HEAD_DIM)
        u = _gelu(u_ref[:, sl])
        v = _gelu(v_ref[:, sl])
        mu = jnp.mean(v, axis=-1, keepdims=True)
        vc = v - mu
        vn = vc * lax.rsqrt(jnp.mean(vc * vc, axis=-1, keepdims=True) + EPS) * lnw_ref[:, sl] + lnb_ref[:, sl]
        if len(out_refs) > 1:
            out_refs[1][:, sl] = vn
        w = w_ref[h]
        if causal_mask:
            w = jnp.where(keep, w, 0.0)
        wb = w.astype(BF16)
        vb = vn.astype(BF16)
        bcol = bcol_ref[0:live, h:h + 1]
        if live < chunk:
            vb = jnp.concatenate([vb, jnp.zeros((chunk - live, HEAD_DIM), BF16)], axis=0)
        for c in range(nchunk):
            rs = slice(c * live, (c + 1) * live)
            mixed = _dot(wb, vb[c * chunk:(c + 1) * chunk])[0:live] + bcol
            o_ref[rs, sl] = (u[rs] * mixed).astype(o_ref.dtype)


def _gmlp(main, lnw, lnb, wmix, bcol, l, *, rows, chunk, causal_mask, want_v):
    m = main.shape[0]
    live = min(rows, chunk)
    cu, cv = COL_AU // A_W, COL_AV // A_W
    out_shape = [jax.ShapeDtypeStruct((m, A_W), BF16)]
    out_specs = [pl.BlockSpec((rows, A_W), lambda i: (i, 0))]
    if want_v:
        out_shape.append(jax.ShapeDtypeStruct((m, A_W), F32))
        out_specs.append(pl.BlockSpec((rows, A_W), lambda i: (i, 0)))
    return pl.pallas_call(
        functools.partial(_gmlp_kernel, chunk=chunk, rows=rows, causal_mask=causal_mask),
        grid=(m // rows,),
        in_specs=[pl.BlockSpec((rows, A_W), lambda i: (i, cu)),
                  pl.BlockSpec((rows, A_W), lambda i: (i, cv)),
                  pl.BlockSpec((None, 1, A_W), lambda i: (l, 0, 0)),
                  pl.BlockSpec((None, 1, A_W), lambda i: (l, 0, 0)),
                  pl.BlockSpec((None, A_HEADS, chunk, chunk), lambda i: (l, 0, 0, 0)),
                  pl.BlockSpec((None, live, A_HEADS), lambda i: (l, 0, 0))],
        out_specs=out_specs, out_shape=out_shape,
        compiler_params=_cparams("parallel"), name="gmlp")(main, main, lnw, lnb, wmix, bcol)


def _delta_kernel(q_ref, k_ref, v_ref, z_ref, ba_ref, cs_ref, cw_ref, alog_ref, dtb_ref, nw_ref, s0_ref,
                  o_ref, s_ref, buf_s, qs_s, ks_s, vs_s, g_s, bt_s, gc_s, u_s, w_s, ai_s, qd_s, kd_s,
                  *, tb, t_valid, t_total):
    t = pl.program_id(1)
    head = SUBLANES

    @pl.when(t == 0)
    def _():
        s_ref[...] = s0_ref[...]
        for j in range(3):
            buf_s[j, 0:head, :] = cs_ref[:, j * B_W:(j + 1) * B_W]

    if t_valid < t_total:
        row = t * tb + lax.broadcasted_iota(jnp.int32, (tb, 1), 0)
        live = row < t_valid
    else:
        live = None

    def keep(x):
        return x if live is None else jnp.where(live, x, 0.0)

    for j, (src, dst) in enumerate(((q_ref, qs_s), (k_ref, ks_s), (v_ref, vs_s))):
        x = src[...]
        buf_s[j, head:head + tb, :] = x
        w = cw_ref[:, j * B_W:(j + 1) * B_W]
        y = w[B_CONV - 1:B_CONV] * x
        for tap in range(B_CONV - 1):
            off = head - (B_CONV - 1) + tap
            y = y + w[tap:tap + 1] * buf_s[j, off:off + tb, :]
        buf_s[j, 0:head, :] = buf_s[j, tb:tb + head, :]
        y = keep(_silu(y))
        for h in range(B_HEADS):
            sl = slice(h * HEAD_DIM, (h + 1) * HEAD_DIM)
            seg = y[:, sl]
            if j == 0:
                seg = seg * lax.rsqrt(jnp.sum(seg * seg, axis=-1, keepdims=True) + EPS) * (HEAD_DIM ** -0.5)
            elif j == 1:
                seg = seg * lax.rsqrt(jnp.sum(seg * seg, axis=-1, keepdims=True) + EPS)
            dst[h] = seg

    ba = ba_ref[...]
    bt_s[...] = keep(_sigmoid(ba))
    g_s[...] = keep(-jnp.exp(alog_ref[...]) * _softplus(ba + dtb_ref[...]))

    n = DN_PAIR
    ii = lax.broadcasted_iota(jnp.int32, (n, n), 0)
    jj = lax.broadcasted_iota(jnp.int32, (n, n), 1)
    same = (ii >= DN_CHUNK) == (jj >= DN_CHUNK)
    incl = same & (ii >= jj)
    strict = same & (ii > jj)
    tri = jnp.where(incl, 1.0, 0.0)
    eye = jnp.where(ii == jj, 1.0, 0.0)
    rowi = lax.broadcasted_iota(jnp.int32, (n, 1), 0)
    lane = lax.broadcasted_iota(jnp.int32, (n, n), 1)
    zeros_half = jnp.zeros((DN_CHUNK, HEAD_DIM), F32)

    tri3 = jnp.concatenate([tri.astype(BF16)] * 3, axis=1)

    heads = range(B_HEADS)
    gcol_of = lambda gc, h: gc[:, B_HEADS + h:B_HEADS + h + 1]

    def intra(p, carry):
        r0 = pl.multiple_of(p * n, n)
        rows = pl.ds(r0, n)
        gc = _dot(tri3, jnp.concatenate(_split3(g_s[rows, :]), axis=0))
        gct = gc.T
        bt = bt_s[rows, :]
        gc_s[rows, :] = gc
        q = [qs_s[h, rows, :] for h in heads]
        k = [ks_s[h, rows, :] for h in heads]
        decay = [jnp.exp(jnp.where(incl, gcol_of(gc, h) - gct[B_HEADS + h:B_HEADS + h + 1, :], NEG_INF))
                 for h in heads]
        kb = [k[h] * bt[:, h:h + 1] for h in heads]
        kq = [_dot_nt(jnp.concatenate([kb[h].astype(BF16), q[h].astype(BF16)], axis=0), k[h].astype(BF16))
              for h in heads]
        a = [-jnp.where(strict, kq[h][0:n] * decay[h], 0.0) for h in heads]
        pm = [eye + a[h] for h in heads]
        a_hl = [_split2(a[h]) for h in heads]
        a = [_dot3(a_hl[h], a_hl[h]) for h in heads]
        for _ in range(int(math.log2(DN_CHUNK)) - 2):
            a_hl = [_split2(a[h]) for h in heads]
            pm_hl = [_split2(pm[h]) for h in heads]
            prod = [_dot3(tuple(jnp.concatenate([x, y], axis=0) for x, y in zip(a_hl[h], pm_hl[h])), a_hl[h])
                    for h in heads]
            a = [prod[h][0:n] for h in heads]
            pm = [pm[h] + prod[h][n:2 * n] for h in heads]
        prod = [_dot3(_split2(pm[h]), _split2(a[h])) for h in heads]
        pm = [pm[h] + prod[h] for h in heads]
        egc = [jnp.exp(gcol_of(gc, h)) for h in heads]
        sol = [_dot3(_split2(pm[h]),
                     _split2(jnp.concatenate([vs_s[h, rows, :] * bt[:, h:h + 1], kb[h] * egc[h]], axis=1)))
               for h in heads]
        for h in heads:
            u_s[h, rows, :] = sol[h][:, 0:HEAD_DIM]
            w_s[h, rows, :] = sol[h][:, HEAD_DIM:2 * HEAD_DIM].astype(BF16)
            ai_s[h, rows, :] = (kq[h][n:2 * n] * decay[h]).astype(BF16)
            qd_s[h, rows, :] = (q[h] * egc[h]).astype(BF16)
            gcol = gcol_of(gc, h)
            glast = jnp.where(rowi < DN_CHUNK, gcol[DN_CHUNK - 1:DN_CHUNK], gcol[n - 1:n])
            kdt = (k[h] * jnp.exp(glast - gcol)).T
            kd_s[h, p, 0] = jnp.where(lane < DN_CHUNK, kdt, 0.0).astype(BF16)
            kd_s[h, p, 1] = jnp.where(lane >= DN_CHUNK, kdt, 0.0).astype(BF16)
        return carry

    def inter(p, carry):
        r0 = pl.multiple_of(p * n, n)
        s = [s_ref[h] for h in heads]
        outs = [[] for _ in heads]
        for c in range(2):
            rs = pl.ds(pl.multiple_of(r0 + c * DN_CHUNK, DN_CHUNK), DN_CHUNK)
            ws = [_dot(jnp.concatenate([w_s[h, rs, :], qd_s[h, rs, :]], axis=0), s[h].astype(BF16)) for h in heads]
            v_new = [u_s[h, rs, :] - ws[h][0:DN_CHUNK] for h in heads]
            v_pad = [jnp.concatenate([v_new[h], zeros_half] if c == 0 else [zeros_half, v_new[h]],
                                     axis=0).astype(BF16) for h in heads]
            av = [_dot(jnp.concatenate([ai_s[h, rs, :], kd_s[h, p, c]], axis=0), v_pad[h]) for h in heads]
            last = gc_s[pl.ds(pl.multiple_of(r0 + (c + 1) * DN_CHUNK - SUBLANES, SUBLANES), SUBLANES), :]
            for h in heads:
                outs[h].append(ws[h][DN_CHUNK:2 * DN_CHUNK] + av[h][0:DN_CHUNK])
                gl = jnp.exp(last[SUBLANES - 1:SUBLANES, B_HEADS + h:B_HEADS + h + 1])
                s[h] = s[h] * gl + av[h][DN_CHUNK:DN_CHUNK + HEAD_DIM]
        for h in heads:
            sl = slice(h * HEAD_DIM, (h + 1) * HEAD_DIM)
            s_ref[h] = s[h]
            o = jnp.concatenate(outs[h], axis=0)
            z = z_ref[pl.ds(r0, n), sl]
            o_ref[pl.ds(r0, n), sl] = (_rms(o, nw_ref[...]) * _silu(z)).astype(o_ref.dtype)
        return carry

    lax.fori_loop(0, tb // n, intra, 0)
    lax.fori_loop(0, tb // n, inter, 0)


def _delta(main, conv_state, conv_w, alog, dtb, nw, s0, l, *, nb, t_total, t_valid, tb):
    m = main.shape[0]
    nt = t_total // tb
    cq, ck, cv, cz, cba = COL_BQ // B_W, COL_BK // B_W, COL_BV // B_W, COL_BZ // B_W, COL_BA // HEAD_DIM
    row = lambda b, t: b * nt + t
    return pl.pallas_call(
        functools.partial(_delta_kernel, tb=tb, t_valid=t_valid, t_total=t_total),
        grid=(nb, nt),
        in_specs=[pl.BlockSpec((tb, B_W), lambda b, t: (row(b, t), cq)),
                  pl.BlockSpec((tb, B_W), lambda b, t: (row(b, t), ck)),
                  pl.BlockSpec((tb, B_W), lambda b, t: (row(b, t), cv)),
                  pl.BlockSpec((tb, B_W), lambda b, t: (row(b, t), cz)),
                  pl.BlockSpec((tb, HEAD_DIM), lambda b, t: (row(b, t), cba)),
                  pl.BlockSpec((None, SUBLANES, 3 * B_W), lambda b, t: (b, 0, 0)),
                  pl.BlockSpec((None, B_CONV, 3 * B_W), lambda b, t: (l, 0, 0)),
                  pl.BlockSpec((None, 1, HEAD_DIM), lambda b, t: (l, 0, 0)),
                  pl.BlockSpec((None, 1, HEAD_DIM), lambda b, t: (l, 0, 0)),
                  pl.BlockSpec((None, 1, HEAD_DIM), lambda b, t: (l, 0, 0)),
                  pl.BlockSpec((None, B_HEADS, HEAD_DIM, HEAD_DIM), lambda b, t: (b, 0, 0, 0))],
        out_specs=[pl.BlockSpec((tb, B_W), lambda b, t: (row(b, t), 0)),
                   pl.BlockSpec((None, B_HEADS, HEAD_DIM, HEAD_DIM), lambda b, t: (b, 0, 0, 0))],
        out_shape=[jax.ShapeDtypeStruct((m, B_W), BF16),
                   jax.ShapeDtypeStruct((nb, B_HEADS, HEAD_DIM, HEAD_DIM), F32)],
        scratch_shapes=[pltpu.VMEM((3, tb + SUBLANES, B_W), F32),
                        pltpu.VMEM((B_HEADS, tb, HEAD_DIM), F32),
                        pltpu.VMEM((B_HEADS, tb, HEAD_DIM), F32),
                        pltpu.VMEM((B_HEADS, tb, HEAD_DIM), F32),
                        pltpu.VMEM((tb, HEAD_DIM), F32),
                        pltpu.VMEM((tb, HEAD_DIM), F32),
                        pltpu.VMEM((tb, HEAD_DIM), F32),
                        pltpu.VMEM((B_HEADS, tb, HEAD_DIM), F32),
                        pltpu.VMEM((B_HEADS, tb, HEAD_DIM), BF16),
                        pltpu.VMEM((B_HEADS, tb, HEAD_DIM), BF16),
                        pltpu.VMEM((B_HEADS, tb, HEAD_DIM), BF16),
                        pltpu.VMEM((B_HEADS, tb // DN_PAIR, 2, HEAD_DIM, DN_PAIR), BF16)],
        compiler_params=_cparams("parallel", "arbitrary"), name="deltanet")(
            main, main, main, main, main, conv_state, conv_w, alog, dtb, nw, s0)


def _bias_from_rel(rel, tbl_ref, h):
    bias = jnp.full(rel.shape, tbl_ref[h], F32)
    for b in range(1, N_BUCKETS):
        bias = jnp.where(rel >= BUCKET_START[b], tbl_ref[b * C_HEADS + h], bias)
    return bias


def _lambda(lamp_ref, lam_init):
    lp = lamp_ref[...]
    s1 = jnp.sum(lp[0:1] * lp[1:2], axis=-1, keepdims=True)
    s2 = jnp.sum(lp[2:3] * lp[3:4], axis=-1, keepdims=True)
    return jnp.exp(s1) - jnp.exp(s2) + lam_init


def _split_halves(qh):
    lane = lax.broadcasted_iota(jnp.int32, qh.shape, 1)
    return jnp.where(lane < C_HALF, qh, 0.0), jnp.where(lane >= C_HALF, qh, 0.0)


def _softmax_update(m_ref, l_ref, acc_ref, idx, s, pv):
    chunks = [s[:, c * HEAD_DIM:(c + 1) * HEAD_DIM] for c in range(s.shape[1] // HEAD_DIM)]
    mx = chunks[0]
    for c in chunks[1:]:
        mx = jnp.maximum(mx, c)
    m_old = m_ref[idx]
    m_new = jnp.maximum(m_old, jnp.max(mx, axis=-1, keepdims=True))
    p = [jnp.exp2(c - m_new) for c in chunks]
    ps = p[0]
    for c in p[1:]:
        ps = ps + c
    a = jnp.exp2(m_old - m_new)
    l_ref[idx] = a * l_ref[idx] + jnp.sum(ps, axis=-1, keepdims=True)
    acc_ref[idx] = a * acc_ref[idx] + pv(jnp.concatenate([c.astype(BF16) for c in p], axis=1))
    m_ref[idx] = m_new


def _prompt_attn_step(h, b, qi, tbl_ref, q_ref, k_ref, v_ref, lamp_ref, nw_ref, o_ref, ko_ref, vo_ref,
                      kb_s, vb_s, bias_s, m_s, l_s, acc_s, *, tq, lam_init):
    far_bias = tbl_ref[(N_BUCKETS - 1) * C_HEADS + h]

    @pl.when((qi == 0) & (b == 0))
    def _():
        r = lax.broadcasted_iota(jnp.int32, (tq, tq), 0) - lax.broadcasted_iota(jnp.int32, (tq, tq), 1)
        bias_s[0] = jnp.where(r >= 0, (_bias_from_rel(r, tbl_ref, h) - far_bias) * LOG2E, NEG_INF)
        bias_s[1] = (_bias_from_rel(r + tq, tbl_ref, h) - far_bias) * LOG2E

    @pl.when(qi == 0)
    def _():
        kf = k_ref[...]
        vf = v_ref[...]
        ko_ref[...] = kf
        vo_ref[...] = vf
        kb_s[...] = kf.astype(BF16)
        vb_s[...] = vf.astype(BF16)

    m_s[...] = jnp.full(m_s.shape, NEG_INF, F32)
    l_s[...] = jnp.zeros(l_s.shape, F32)
    acc_s[...] = jnp.zeros(acc_s.shape, F32)
    q1, q2 = _split_halves(q_ref[...] * (C_HALF ** -0.5 * LOG2E))
    q1 = q1.astype(BF16)
    q2 = q2.astype(BF16)

    def blocks(kblocks, biases):
        starts = [pl.multiple_of(kj * tq, tq) for kj in kblocks]
        scores = [[_dot_nt(qq, kb_s[pl.ds(st, tq), :]) for qq in (q1, q2)] for st in starts]
        for st, sc, bias in zip(starts, scores, biases):
            vb = vb_s[pl.ds(st, tq), :]
            for idx, s in enumerate(sc):
                if bias is not None:
                    s = s + bias
                _softmax_update(m_s, l_s, acc_s, idx, s, lambda p: _dot(p, vb))

    n_far = jnp.maximum(qi - 1, 0)

    def far_pair(j, carry):
        blocks([2 * j, 2 * j + 1], [None, None])
        return carry

    lax.fori_loop(0, n_far // 2, far_pair, 0)

    @pl.when(lax.rem(n_far, 2) == 1)
    def _():
        blocks([n_far - 1], [None])

    @pl.when(qi >= 1)
    def _():
        blocks([qi - 1, qi], [bias_s[1], bias_s[0]])

    @pl.when(qi == 0)
    def _():
        blocks([qi], [bias_s[0]])

    lam = _lambda(lamp_ref, lam_init)
    o = acc_s[0] / l_s[0] - lam * (acc_s[1] / l_s[1])
    o_ref[...] = (_rms(o, nw_ref[...]) * (1.0 - lam_init)).astype(o_ref.dtype)


N_PROMPT_IN, N_SAMPLE_IN, N_PROMPT_SCRATCH = 5, 3, 6


def _attn_kernel(pt_ref, tbl_ref, *refs, n_alias, pages, nb, nq, steps_s, chunks_s, tq, ts, page, lam_init):
    del pt_ref
    q_ref, k_ref, v_ref, lamp_ref, nw_ref = refs[:N_PROMPT_IN]
    refs = refs[N_PROMPT_IN + n_alias:]
    qs_ref, kn_ref, vn_ref = refs[:N_SAMPLE_IN]
    k_refs = refs[N_SAMPLE_IN:N_SAMPLE_IN + pages]
    v_refs = refs[N_SAMPLE_IN + pages:N_SAMPLE_IN + 2 * pages]
    o_ref, ko_ref, vo_ref, os_ref = refs[N_SAMPLE_IN + 2 * pages:N_SAMPLE_IN + 2 * pages + 4]
    scratch = refs[N_SAMPLE_IN + 2 * pages + 4:]
    h, b, qi = pl.program_id(0), pl.program_id(1), pl.program_id(2)
    _prompt_attn_step(h, b, qi, tbl_ref, q_ref, k_ref, v_ref, lamp_ref, nw_ref, o_ref, ko_ref, vo_ref,
                      *scratch[:N_PROMPT_SCRATCH], tq=tq, lam_init=lam_init)
    step = (h * nb + b) * nq + qi

    @pl.when(step < steps_s)
    def _():
        _sample_attn_step(lax.rem(step, chunks_s), chunks_s, tbl_ref, qs_ref, kn_ref, vn_ref, lamp_ref, nw_ref,
                          k_refs, v_refs, os_ref, *scratch[N_PROMPT_SCRATCH:], ts=ts, page=page, lam_init=lam_init)


def _attention(main, q_bm, kn_bm, vn_bm, cache_k, cache_v, page_table, tbl, lamp, nw, l, kv_stacks, *,
               depth, nb, t_total, tq, nb_s, ts, n_pool, lam_init):
    assert tq > FAR_DISTANCE
    m = main.shape[0]
    nq = t_total // tq
    page = cache_k.shape[2]
    n_pages = page_table.shape[0] // nb_s
    pages = math.gcd(n_pages, PAGES_PER_STEP)
    chunks_s = n_pages // pages
    steps_s = nb_s * chunks_s
    assert pages * page > FAR_DISTANCE + ts
    assert steps_s <= C_HEADS * nb * nq
    cq, ck, cv = COL_CQ // HEAD_DIM, COL_CK // HEAD_DIM, COL_CV // HEAD_DIM

    def s_step(h, b, i):
        return jnp.minimum((h * nb + b) * nq + i, steps_s - 1)

    def kv_page(k):
        def index(h, b, i, pt):
            s = s_step(h, b, i)
            return (l * n_pool + pt[(s // chunks_s) * n_pages + (s % chunks_s) * pages + k], 0, 0, 0)
        return pl.BlockSpec((None, C_HEADS, page, HEAD_DIM), index)

    tok = pl.BlockSpec((ts, C_W), lambda h, b, i, pt: (s_step(h, b, i) // chunks_s, 0))
    kv_shape = jax.ShapeDtypeStruct((depth, nb, C_HEADS, t_total, HEAD_DIM), F32)
    kv_spec = pl.BlockSpec((None, None, None, t_total, HEAD_DIM), lambda h, b, i, pt: (l, b, h, 0, 0))
    in_specs = [pl.BlockSpec(memory_space=pltpu.SMEM),
                pl.BlockSpec((tq, HEAD_DIM), lambda h, b, i, pt: (b * nq + i, cq + h)),
                pl.BlockSpec((t_total, HEAD_DIM), lambda h, b, i, pt: (b, ck + h)),
                pl.BlockSpec((t_total, HEAD_DIM), lambda h, b, i, pt: (b, cv + h)),
                pl.BlockSpec((None, 4, C_HALF), lambda h, b, i, pt: (l, 0, 0)),
                pl.BlockSpec((None, 1, HEAD_DIM), lambda h, b, i, pt: (l, 0, 0))]
    args = [tbl, main, main, main, lamp, nw]
    aliases = {}
    if kv_stacks is not None:
        aliases = {1 + len(args): 1, 2 + len(args): 2}
        in_specs += [pl.BlockSpec(memory_space=pl.ANY)] * 2
        args += list(kv_stacks)
    in_specs += [tok, tok, tok] + [kv_page(k) for k in range(pages)] + [kv_page(k) for k in range(pages)]
    args += [q_bm, kn_bm, vn_bm] + [cache_k] * pages + [cache_v] * pages
    rows_s = C_HEADS * 2 * ts
    grid_spec = pltpu.PrefetchScalarGridSpec(
        num_scalar_prefetch=1, grid=(C_HEADS, nb, nq), in_specs=in_specs,
        out_specs=[pl.BlockSpec((tq, HEAD_DIM), lambda h, b, i, pt: (b * nq + i, h)), kv_spec, kv_spec, tok],
        scratch_shapes=[pltpu.VMEM((t_total, HEAD_DIM), BF16),
                        pltpu.VMEM((t_total, HEAD_DIM), BF16),
                        pltpu.VMEM((2, tq, tq), F32),
                        pltpu.VMEM((2, tq, HEAD_DIM), F32),
                        pltpu.VMEM((2, tq, HEAD_DIM), F32),
                        pltpu.VMEM((2, tq, HEAD_DIM), F32),
                        pltpu.VMEM((rows_s, HEAD_DIM), F32),
                        pltpu.VMEM((rows_s, HEAD_DIM), F32),
                        pltpu.VMEM((rows_s, HEAD_DIM), F32),
                        pltpu.VMEM((page, C_W), F32),
                        pltpu.VMEM((page, C_W), F32)])
    return pl.pallas_call(
        functools.partial(_attn_kernel, n_alias=len(aliases), pages=pages, nb=nb, nq=nq, steps_s=steps_s,
                          chunks_s=chunks_s, tq=tq, ts=ts, page=page, lam_init=lam_init),
        grid_spec=grid_spec,
        out_shape=[jax.ShapeDtypeStruct((m, C_W), BF16), kv_shape, kv_shape,
                   jax.ShapeDtypeStruct((nb_s * ts, C_W), F32)],
        input_output_aliases=aliases,
        compiler_params=_cparams("arbitrary", "arbitrary", "arbitrary"), name="attention")(page_table, *args)


def _sample_attn_step(s_id, nsteps, tbl_ref, q_ref, kn_ref, vn_ref, lamp_ref, nw_ref, k_refs, v_refs, o_ref,
                      m_s, l_s, acc_s, kn_s, vn_s, *, ts, page, lam_init):
    width = len(k_refs) * page
    rows = 2 * ts

    @pl.when(s_id == 0)
    def _():
        m_s[...] = jnp.full(m_s.shape, NEG_INF, F32)
        l_s[...] = jnp.zeros(l_s.shape, F32)
        acc_s[...] = jnp.zeros(acc_s.shape, F32)

    q = q_ref[...] * (C_HALF ** -0.5 * LOG2E)
    qq = []
    for h in range(C_HEADS):
        q1, q2 = _split_halves(q[:, h * HEAD_DIM:(h + 1) * HEAD_DIM])
        qq.append(jnp.concatenate([q1, q2], axis=0).astype(BF16))

    def update(keys, vals, bias):
        s = jnp.concatenate([_dot_nt(qq[h], keys(h)) for h in range(C_HEADS)], axis=0)
        if bias is not None:
            s = s + bias

        def pv(p):
            return jnp.concatenate([_dot(p[h * rows:(h + 1) * rows], vals(h)) for h in range(C_HEADS)], axis=0)

        _softmax_update(m_s, l_s, acc_s, slice(None), s, pv)

    def past_keys(h):
        return jnp.concatenate([r[h].astype(BF16) for r in k_refs], axis=0)

    def past_vals(h):
        return jnp.concatenate([r[h].astype(BF16) for r in v_refs], axis=0)

    @pl.when(s_id < nsteps - 1)
    def _():
        update(past_keys, past_vals, None)

    @pl.when(s_id == nsteps - 1)
    def _():
        tq_pos = lax.rem(lax.broadcasted_iota(jnp.int32, (rows, width), 0), ts)
        rel_past = tq_pos + width - lax.broadcasted_iota(jnp.int32, (rows, width), 1)
        rel_new = (lax.rem(lax.broadcasted_iota(jnp.int32, (rows, page), 0), ts)
                   - lax.broadcasted_iota(jnp.int32, (rows, page), 1))
        bias_past, bias_new = [], []
        for h in range(C_HEADS):
            far_bias = tbl_ref[(N_BUCKETS - 1) * C_HEADS + h]
            bias_past.append((_bias_from_rel(rel_past, tbl_ref, h) - far_bias) * LOG2E)
            bias_new.append(jnp.where(rel_new >= 0, (_bias_from_rel(rel_new, tbl_ref, h) - far_bias) * LOG2E, NEG_INF))
        update(past_keys, past_vals, jnp.concatenate(bias_past, axis=0))
        kn_s[...] = jnp.zeros(kn_s.shape, F32)
        vn_s[...] = jnp.zeros(vn_s.shape, F32)
        kn_s[0:ts, :] = kn_ref[...]
        vn_s[0:ts, :] = vn_ref[...]
        update(lambda h: kn_s[:, h * HEAD_DIM:(h + 1) * HEAD_DIM].astype(BF16),
               lambda h: vn_s[:, h * HEAD_DIM:(h + 1) * HEAD_DIM].astype(BF16),
               jnp.concatenate(bias_new, axis=0))
        lam = _lambda(lamp_ref, lam_init)
        on = acc_s[...] / l_s[...]
        for h in range(C_HEADS):
            o = on[h * rows:h * rows + ts] - lam * on[h * rows + ts:(h + 1) * rows]
            o_ref[:, h * HEAD_DIM:(h + 1) * HEAD_DIM] = (_rms(o, nw_ref[...]) * (1.0 - lam_init)).astype(o_ref.dtype)


def _outproj_kernel(ob_ref, oc_ref, oa_ref, wb_ref, wc_ref, wa_ref, x_ref, nw_ref, xo_ref, ho_ref):
    acc = _dot(ob_ref[...], wb_ref[...]) + _dot(oc_ref[...], wc_ref[...]) + _dot(oa_ref[...], wa_ref[...])
    x = x_ref[...] + acc
    xo_ref[...] = x
    ho_ref[...] = _rms(x, nw_ref[...]).astype(ho_ref.dtype)


def _outproj(ob, oc, oa, w, x, nw, l, tm):
    m, d = x.shape
    return pl.pallas_call(
        _outproj_kernel, grid=(m // tm,),
        in_specs=[pl.BlockSpec((tm, B_W), lambda i: (i, 0)),
                  pl.BlockSpec((tm, C_W), lambda i: (i, 0)),
                  pl.BlockSpec((tm, A_W), lambda i: (i, 0)),
                  pl.BlockSpec((None, B_W, d), lambda i: (l, 0, 0)),
                  pl.BlockSpec((None, C_W, d), lambda i: (l, 1, 0)),
                  pl.BlockSpec((None, A_W, d), lambda i: (l, (B_W + C_W) // A_W, 0)),
                  pl.BlockSpec((tm, d), lambda i: (i, 0)),
                  pl.BlockSpec((None, 1, d), lambda i: (l, 0, 0))],
        out_specs=[pl.BlockSpec((tm, d), lambda i: (i, 0)), pl.BlockSpec((tm, d), lambda i: (i, 0))],
        out_shape=[jax.ShapeDtypeStruct((m, d), F32), jax.ShapeDtypeStruct((m, d), BF16)],
        compiler_params=_cparams("parallel"), name="outproj")(ob, oc, oa, w, w, w, x, nw)


def _ffn_up_kernel(h_ref, wg_ref, wu_ref, cw_ref, cb_ref, st_ref, o_ref, so_ref, buf_s, wg_s, wu_s,
                   *, tm, stride, head, blocks):
    i = pl.program_id(1)

    @pl.when(i == 0)
    def _():
        wg_s[...] = wg_ref[...].astype(BF16)
        wu_s[...] = wu_ref[...].astype(BF16)

    @pl.when(i % blocks == 0)
    def _():
        buf_s[0:head, :] = st_ref[...]

    hn = h_ref[...]
    g = _dot(hn, wg_s[...])
    u = _dot(hn, wu_s[...])
    buf_s[head:head + tm, :] = g
    y = cw_ref[FFN_CONV - 1:FFN_CONV, :] * g + cb_ref[...]
    for tap in range(FFN_CONV - 1):
        off = head - (FFN_CONV - 1 - tap) * stride
        y = y + cw_ref[tap:tap + 1, :] * buf_s[off:off + tm, :]
    o_ref[...] = (_gelu(y) * u).astype(o_ref.dtype)
    last = buf_s[tm:tm + head, :]
    so_ref[...] = last
    buf_s[0:head, :] = last


def _ffn_up(hn, wg, wu, cw, cb, state, l, *, tm, tn, stride, blocks):
    m, d = hn.shape
    dff = wg.shape[-1]
    head = state.shape[1]
    nseq = state.shape[0]
    return pl.pallas_call(
        functools.partial(_ffn_up_kernel, tm=tm, stride=stride, head=head, blocks=blocks),
        grid=(dff // tn, m // tm),
        in_specs=[pl.BlockSpec((tm, d), lambda j, i: (i, 0)),
                  pl.BlockSpec((None, d, tn), lambda j, i: (l, 0, j)),
                  pl.BlockSpec((None, d, tn), lambda j, i: (l, 0, j)),
                  pl.BlockSpec((None, FFN_CONV, tn), lambda j, i: (l, 0, j)),
                  pl.BlockSpec((None, 1, tn), lambda j, i: (l, 0, j)),
                  pl.BlockSpec((None, head, tn), lambda j, i: (i // blocks, 0, j))],
        out_specs=[pl.BlockSpec((tm, tn), lambda j, i: (i, j)),
                   pl.BlockSpec((None, head, tn), lambda j, i: (i // blocks, 0, j))],
        out_shape=[jax.ShapeDtypeStruct((m, dff), BF16), jax.ShapeDtypeStruct((nseq, head, dff), F32)],
        scratch_shapes=[pltpu.VMEM((tm + head, tn), F32), pltpu.VMEM((d, tn), BF16), pltpu.VMEM((d, tn), BF16)],
        compiler_params=_cparams("parallel", "arbitrary"), name="ffn_up")(hn, wg, wu, cw, cb, state)


def _ffn_down_kernel(h_ref, w_ref, x_ref, o_ref):
    o_ref[...] = x_ref[...] + _dot(h_ref[...], w_ref[...])


def _ffn_down(hid, w, x, l, tm, tn):
    m, dff = hid.shape
    d = x.shape[1]
    return pl.pallas_call(
        _ffn_down_kernel, grid=(m // tm, d // tn),
        in_specs=[pl.BlockSpec((tm, dff), lambda i, j: (i, 0)),
                  pl.BlockSpec((None, dff, tn), lambda i, j: (l, 0, j)),
                  pl.BlockSpec((tm, tn), lambda i, j: (i, j))],
        out_specs=pl.BlockSpec((tm, tn), lambda i, j: (i, j)),
        out_shape=jax.ShapeDtypeStruct((m, d), F32),
        compiler_params=_cparams("parallel", "parallel"), name="ffn_down")(hid, w, x)


def _pick(total, want):
    t = min(total, want)
    while total % t:
        t //= 2
    return t


def kernel(x_prompt, x_sample, cache_k, cache_v, state_conv_b, state_delta, state_ffn_conv, page_table, attn_norm_w, w_in, a_ln_w, a_ln_b, a_ws, a_bs, b_conv_w, b_a_log, b_dt_bias, b_norm_w, c_lam_q1, c_lam_k1, c_lam_q2, c_lam_k2, c_norm_w, rel_bias, w_out, ffn_norm_w, w_gate, w_up, ffn_conv_w, ffn_conv_b, w_down, final_norm_w):
    bp, tp, d = x_prompt.shape
    bs, ts, _ = x_sample.shape
    depth = w_in.shape[0]
    dff = w_gate.shape[-1]
    n_pool, page = cache_k.shape[1], cache_k.shape[2]
    mp, ms = bp * tp, bs * ts

    o_au, o_av, o_qkv, o_z, o_beta = 0, A_W, 2 * A_W, 2 * A_W + 3 * B_W, 2 * A_W + 4 * B_W
    o_cq = o_beta + 2 * B_HEADS
    o_ck, o_cv = o_cq + C_W, o_cq + 2 * C_W
    w_main = (w_in[:, :, o_qkv:o_beta].astype(BF16),
              jnp.concatenate([w_in[:, :, o_cq:o_ck], w_in[:, :, o_beta:o_cq],
                               jnp.zeros((depth, d, COL_AU - COL_BA - 2 * B_HEADS), w_in.dtype)],
                              axis=-1).astype(BF16),
              w_in[:, :, o_au:o_qkv].astype(BF16),
              w_in[:, :, o_ck:o_cv + C_W].astype(BF16))
    assert sum(w.shape[-1] for w in w_main) == N_MAIN
    w_out_r = jnp.concatenate([w_out[:, A_W:], w_out[:, :A_W]], axis=1).astype(BF16)
    wg, wu = w_gate, w_up
    wd = w_down.astype(BF16)
    attn_nw = attn_norm_w.reshape(depth, 1, d)
    ffn_nw = ffn_norm_w.reshape(depth, 1, d)
    lnw = a_ln_w.reshape(depth, 1, A_W)
    lnb = a_ln_b.reshape(depth, 1, A_W)
    bcol_p = jnp.swapaxes(a_bs, 1, 2)
    tril = jnp.tril(jnp.ones((ts, ts), F32))
    assert ms <= A_CHUNK
    wmix_s = jnp.einsum('lhts,bc->lhtbsc', a_ws[:, :, :ts, :ts] * tril, jnp.eye(bs, dtype=F32)).reshape(
        depth, A_HEADS, ms, ms)
    wmix_s = jnp.pad(wmix_s, ((0, 0), (0, 0), (0, A_CHUNK - ms), (0, A_CHUNK - ms)))
    bcol_s = jnp.repeat(jnp.swapaxes(a_bs[:, :, :ts], 1, 2), bs, axis=1)
    pad_lanes = jnp.zeros((depth, 1, HEAD_DIM - 2 * B_HEADS), F32)
    alog = jnp.concatenate([jnp.zeros((depth, 1, B_HEADS), F32), b_a_log.reshape(depth, 1, B_HEADS), pad_lanes], -1)
    dtb = jnp.concatenate([jnp.zeros((depth, 1, B_HEADS), F32), b_dt_bias.reshape(depth, 1, B_HEADS), pad_lanes], -1)
    b_nw = b_norm_w.reshape(depth, 1, HEAD_DIM)
    c_nw = c_norm_w.reshape(depth, 1, HEAD_DIM)
    lamp = jnp.stack([c_lam_q1, c_lam_k1, c_lam_q2, c_lam_k2], axis=1)
    tbl = rel_bias.reshape(-1)
    ffn_cb = ffn_conv_b.reshape(depth, 1, dff)
    ck_flat = jnp.swapaxes(cache_k, 2, 3).reshape(depth * n_pool, C_HEADS, page, HEAD_DIM)
    cv_flat = jnp.swapaxes(cache_v, 2, 3).reshape(depth * n_pool, C_HEADS, page, HEAD_DIM)
    pt_flat = page_table.reshape(-1)

    conv_head = SUBLANES
    zero_conv_p = jnp.zeros((bp, conv_head, 3 * B_W), F32)
    zero_delta_p = jnp.zeros((bp, B_HEADS, HEAD_DIM, HEAD_DIM), F32)
    zero_ffn_p = jnp.zeros((bp, SUBLANES, dff), F32)
    ffn_head_s = max(SUBLANES, (FFN_CONV - 1) * bs)

    tm_in = _pick(mp, TM_IN)
    tm_out = _pick(mp, TM_OUT)
    tm_up = _pick(tp, TM_UP)
    tm_down = _pick(mp, TM_DOWN)
    tb_delta = _pick(tp, TB_DELTA)
    tq = _pick(tp, TQ)
    ts_pad_delta = DN_PAIR

    xp = x_prompt.reshape(mp, d)
    xs = jnp.swapaxes(x_sample, 0, 1).reshape(ms, d)

    def to_bm(a):
        return jnp.swapaxes(a.reshape(ts, bs, -1), 0, 1)

    def to_tm(a):
        return jnp.swapaxes(a, 0, 1).reshape(ms, -1)

    kv_stacks = None
    p_conv, p_delta, p_ffn = [], [], []
    s_k, s_v, s_conv, s_delta, s_ffn, s_av = [], [], [], [], [], []
    for l in range(depth):
        lam_init = 0.8 - 0.6 * math.exp(-0.3 * l)

        main = _inproj(xp, attn_nw, w_main, l, tm_in, 512)
        main_s = _inproj(xs, attn_nw, w_main, l, ms, 512)
        main_bm = to_bm(main_s)
        q_bm = main_bm[:, :, COL_CQ:COL_CQ + C_W].reshape(ms, C_W)
        kn_bm = main_bm[:, :, COL_CK:COL_CK + C_W]
        vn_bm = main_bm[:, :, COL_CV:COL_CV + C_W]
        out_c, k_stack, v_stack, out_c_s = _attention(
            main, q_bm, kn_bm.reshape(ms, C_W), vn_bm.reshape(ms, C_W), ck_flat, cv_flat, pt_flat, tbl, lamp, c_nw,
            l, kv_stacks, depth=depth, nb=bp, t_total=tp, tq=tq, nb_s=bs, ts=ts, n_pool=n_pool, lam_init=lam_init)
        kv_stacks = (k_stack, v_stack)

        (out_a,) = _gmlp(main, lnw, lnb, a_ws, bcol_p, l, rows=4 * A_CHUNK, chunk=A_CHUNK, causal_mask=True,
                         want_v=False)
        out_b, delta_new = _delta(main, zero_conv_p, b_conv_w, alog, dtb, b_nw, zero_delta_p, l,
                                  nb=bp, t_total=tp, t_valid=tp, tb=tb_delta)
        xp, hn2 = _outproj(out_b, out_c, out_a, w_out_r, xp, ffn_nw, l, tm_out)
        hid, ffn_last = _ffn_up(hn2, wg, wu, ffn_conv_w, ffn_cb, zero_ffn_p, l, tm=tm_up, tn=512, stride=1,
                                blocks=tp // tm_up)
        xp = _ffn_down(hid, wd, xp, l, tm_down, 512)
        main3 = main.reshape(bp, tp, N_MAIN)
        p_conv.append(main3[:, tp - (B_CONV - 1):, COL_BQ:COL_BQ + 3 * B_W])
        p_delta.append(delta_new)
        p_ffn.append(ffn_last[:, SUBLANES - (FFN_CONV - 1):])

        out_a_s, va_s = _gmlp(main_s, lnw, lnb, wmix_s, bcol_s, l, rows=ms, chunk=A_CHUNK, causal_mask=False,
                              want_v=True)
        main_pad = jnp.pad(main_bm, ((0, 0), (0, ts_pad_delta - ts), (0, 0))).reshape(bs * ts_pad_delta, N_MAIN)
        conv_s = jnp.pad(state_conv_b[l], ((0, 0), (conv_head - (B_CONV - 1), 0), (0, 0)))
        out_b_s, delta_new_s = _delta(main_pad, conv_s, b_conv_w, alog, dtb, b_nw, state_delta[l], l,
                                      nb=bs, t_total=ts_pad_delta, t_valid=ts, tb=ts_pad_delta)
        out_b_s = to_tm(out_b_s.reshape(bs, ts_pad_delta, B_W)[:, :ts])
        out_c_s = to_tm(out_c_s.reshape(bs, ts, C_W)).astype(BF16)
        xs, hn2_s = _outproj(out_b_s, out_c_s, out_a_s, w_out_r, xs, ffn_nw, l, ms)
        ffn_state_s = jnp.swapaxes(state_ffn_conv[l], 0, 1).reshape(1, (FFN_CONV - 1) * bs, dff)
        ffn_state_s = jnp.pad(ffn_state_s, ((0, 0), (ffn_head_s - (FFN_CONV - 1) * bs, 0), (0, 0)))
        hid_s, ffn_last_s = _ffn_up(hn2_s, wg, wu, ffn_conv_w, ffn_cb, ffn_state_s, l, tm=ms, tn=512, stride=bs,
                                    blocks=1)
        xs = _ffn_down(hid_s, wd, xs, l, ms, 512)
        s_k.append(kn_bm.reshape(bs, ts, C_HEADS, HEAD_DIM))
        s_v.append(vn_bm.reshape(bs, ts, C_HEADS, HEAD_DIM))
        s_conv.append(main_bm[:, ts - (B_CONV - 1):, COL_BQ:COL_BQ + 3 * B_W])
        s_delta.append(delta_new_s)
        s_ffn.append(jnp.swapaxes(ffn_last_s[0, ffn_head_s - (FFN_CONV - 1) * bs:].reshape(FFN_CONV - 1, bs, dff),
                                  0, 1))
        s_av.append(to_bm(va_s).reshape(bs, ts, A_HEADS, HEAD_DIM))

    y_prompt = _rmsnorm(xp, final_norm_w, _pick(mp, 512)).reshape(bp, tp, d)
    y_sample = jnp.swapaxes(_rmsnorm(xs, final_norm_w, ms).reshape(ts, bs, d), 0, 1)
    k_prompt, v_prompt = (jnp.swapaxes(a, 2, 3) for a in kv_stacks)
    return (y_prompt, y_sample,
            k_prompt, v_prompt, jnp.stack(p_conv), jnp.stack(p_delta), jnp.stack(p_ffn),
            jnp.stack(s_k), jnp.stack(s_v), jnp.stack(s_conv), jnp.stack(s_delta), jnp.stack(s_ffn),
            jnp.stack(s_av))
```

```python
import functools
import math

import numpy as np
import jax
import jax.numpy as jnp
from jax import lax
from jax.experimental import pallas as pl
from jax.experimental.pallas import tpu as pltpu

F32 = jnp.float32
BF16 = jnp.bfloat16
HI = lax.Precision.HIGHEST

HEAD_DIM = 128
SUBLANES = 8
A_HEADS, B_HEADS, C_HEADS = 4, 6, 6
A_W, B_W, C_W = A_HEADS * HEAD_DIM, B_HEADS * HEAD_DIM, C_HEADS * HEAD_DIM
A_CHUNK = 128
B_CONV = 4
DN_CHUNK = 64
DN_PAIR = 2 * DN_CHUNK
C_HALF = HEAD_DIM // 2
N_BUCKETS = 32
MAX_DISTANCE = 128
FFN_CONV = 3
EPS = 1e-6
NEG_INF = -1e30
VMEM_LIMIT = 56 * 1024 * 1024
TM_IN, TM_OUT, TM_UP, TM_DOWN, TB_DELTA, TQ = 1024, 512, 1024, 1024, 256, 512
TN = 512
PAGES_PER_STEP = 16
LOG2E = math.log2(math.e)

COL_BQ, COL_BK, COL_BV, COL_BZ, COL_CQ = 0, B_W, 2 * B_W, 3 * B_W, 4 * B_W
COL_BA = COL_CQ + C_W
COL_AU = COL_BA + 2 * HEAD_DIM
COL_AV = COL_AU + A_W
COL_CK = COL_AV + A_W
COL_CV = COL_CK + C_W
N_MAIN = COL_CV + C_W


def _t5_bucket_starts():
    n = np.arange(4 * MAX_DISTANCE)
    max_exact = N_BUCKETS // 2
    nf = np.maximum(n, max_exact).astype(np.float64)
    large = max_exact + (np.log(nf / max_exact) / math.log(MAX_DISTANCE / max_exact)
                         * (N_BUCKETS - max_exact)).astype(np.int64)
    bucket = np.where(n < max_exact, n, np.minimum(large, N_BUCKETS - 1))
    return [int(n[bucket == b].min()) for b in range(N_BUCKETS)]


BUCKET_START = _t5_bucket_starts()
FAR_DISTANCE = BUCKET_START[-1]


def _cparams(*sem):
    return pltpu.CompilerParams(dimension_semantics=sem, vmem_limit_bytes=VMEM_LIMIT)


def _gelu(x):
    return 0.5 * x * (1.0 + jnp.tanh(math.sqrt(2.0 / math.pi) * (x + 0.044715 * (x * x * x))))


def _sigmoid(x):
    return 1.0 / (1.0 + jnp.exp(-x))


def _silu(x):
    return x * _sigmoid(x)


def _softplus(x):
    return jnp.maximum(x, 0.0) + jnp.log(1.0 + jnp.exp(-jnp.abs(x)))


def _rms(x, w):
    return x * lax.rsqrt(jnp.mean(x * x, axis=-1, keepdims=True) + EPS) * w


def _dot(a, b, **kw):
    return jnp.dot(a, b, preferred_element_type=F32, **kw)


def _dot_nt(a, b, **kw):
    return lax.dot_general(a, b, (((1,), (1,)), ((), ())), preferred_element_type=F32, **kw)


def _split2(x):
    hi = x.astype(BF16)
    return hi, (x - hi.astype(F32)).astype(BF16)


def _split3(x):
    hi = x.astype(BF16)
    r = x - hi.astype(F32)
    mid = r.astype(BF16)
    return hi, mid, (r - mid.astype(F32)).astype(BF16)


def _dot3(a_hl, b_hl):
    (ah, al), (bh, bl) = a_hl, b_hl
    return _dot(jnp.concatenate([ah, ah, al], axis=1), jnp.concatenate([bh, bl, bh], axis=0))


def _rmsnorm_kernel(x_ref, w_ref, o_ref):
    o_ref[...] = _rms(x_ref[...], w_ref[...]).astype(o_ref.dtype)


def _rmsnorm(x, w, tm):
    m, d = x.shape
    return pl.pallas_call(
        _rmsnorm_kernel, grid=(m // tm,),
        in_specs=[pl.BlockSpec((tm, d), lambda i: (i, 0)), pl.BlockSpec((1, d), lambda i: (0, 0))],
        out_specs=pl.BlockSpec((tm, d), lambda i: (i, 0)),
        out_shape=jax.ShapeDtypeStruct((m, d), F32),
        compiler_params=_cparams("parallel"), name="final_rmsnorm")(x, w.reshape(1, d))


def _inproj_kernel(x_ref, nw_ref, w_ref, o_ref, hn_s):
    @pl.when(pl.program_id(1) == 0)
    def _():
        hn_s[...] = _rms(x_ref[...], nw_ref[...]).astype(BF16)

    o_ref[...] = _dot(hn_s[...], w_ref[...])


def _col_tiles(w, tn):
    depth, k, n = w.shape
    return jnp.swapaxes(w.reshape(depth, k, n // tn, tn), 1, 2)


def _inproj(x, nw, w, l, tm):
    m, d = x.shape
    nt, tn = w.shape[1], w.shape[3]
    return pl.pallas_call(
        _inproj_kernel, grid=(m // tm, nt),
        in_specs=[pl.BlockSpec((tm, d), lambda i, j: (i, 0)),
                  pl.BlockSpec((None, 1, d), lambda i, j: (l, 0, 0)),
                  pl.BlockSpec((None, None, d, tn), lambda i, j: (l, j, 0, 0))],
        out_specs=pl.BlockSpec((tm, tn), lambda i, j: (i, j)),
        out_shape=jax.ShapeDtypeStruct((m, nt * tn), F32),
        scratch_shapes=[pltpu.VMEM((tm, d), BF16)],
        compiler_params=_cparams("parallel", "arbitrary"), name="inproj")(x, nw, w)


def _gmlp_kernel(u_ref, v_ref, lnw_ref, lnb_ref, w_ref, bcol_ref, *out_refs, chunk, rows, causal_mask):
    o_ref = out_refs[0]
    nchunk = max(rows // chunk, 1)
    live = min(rows, chunk)
    if causal_mask:
        ii = lax.broadcasted_iota(jnp.int32, (chunk, chunk), 0)
        jj = lax.broadcasted_iota(jnp.int32, (chunk, chunk), 1)
        keep = ii >= jj
    for h in range(A_HEADS):
        sl = slice(h * HEAD_DIM, (h + 1) * HEAD_DIM)
        u = _gelu(u_ref[:, sl])
        v = _gelu(v_ref[:, sl])
        mu = jnp.mean(v, axis=-1, keepdims=True)
        vc = v - mu
        vn = vc * lax.rsqrt(jnp.mean(vc * vc, axis=-1, keepdims=True) + EPS) * lnw_ref[:, sl] + lnb_ref[:, sl]
        if len(out_refs) > 1:
            out_refs[1][:, sl] = vn
        w = w_ref[h]
        if causal_mask:
            w = jnp.where(keep, w, 0.0)
        wb = w.astype(BF16)
        vb = vn.astype(BF16)
        bcol = bcol_ref[0:live, h:h + 1]
        if live < chunk:
            vb = jnp.concatenate([vb, jnp.zeros((chunk - live, HEAD_DIM), BF16)], axis=0)
        for c in range(nchunk):
            rs = slice(c * live, (c + 1) * live)
            mixed = _dot(wb, vb[c * chunk:(c + 1) * chunk])[0:live] + bcol
            o_ref[rs, sl] = (u[rs] * mixed).astype(o_ref.dtype)


def _gmlp(main, lnw, lnb, wmix, bcol, l, *, rows, chunk, causal_mask, want_v):
    m = main.shape[0]
    live = min(rows, chunk)
    cu, cv = COL_AU // A_W, COL_AV // A_W
    out_shape = [jax.ShapeDtypeStruct((m, A_W), BF16)]
    out_specs = [pl.BlockSpec((rows, A_W), lambda i: (i, 0))]
    if want_v:
        out_shape.append(jax.ShapeDtypeStruct((m, A_W), F32))
        out_specs.append(pl.BlockSpec((rows, A_W), lambda i: (i, 0)))
    return pl.pallas_call(
        functools.partial(_gmlp_kernel, chunk=chunk, rows=rows, causal_mask=causal_mask),
        grid=(m // rows,),
        in_specs=[pl.BlockSpec((rows, A_W), lambda i: (i, cu)),
                  pl.BlockSpec((rows, A_W), lambda i: (i, cv)),
                  pl.BlockSpec((None, 1, A_W), lambda i: (l, 0, 0)),
                  pl.BlockSpec((None, 1, A_W), lambda i: (l, 0, 0)),
                  pl.BlockSpec((None, A_HEADS, chunk, chunk), lambda i: (l, 0, 0, 0)),
                  pl.BlockSpec((None, live, A_HEADS), lambda i: (l, 0, 0))],
        out_specs=out_specs, out_shape=out_shape,
        compiler_params=_cparams("parallel"), name="gmlp")(main, main, lnw, lnb, wmix, bcol)


def _delta_kernel(q_ref, k_ref, v_ref, z_ref, ba_ref, cs_ref, cw_ref, alog_ref, dtb_ref, nw_ref, s0_ref,
                  o_ref, s_ref, buf_s, qs_s, ks_s, vs_s, g_s, bt_s, gc_s, u_s, w_s, ai_s, qd_s, kd_s,
                  *, tb, t_valid, t_total):
    t = pl.program_id(1)
    head = SUBLANES

    @pl.when(t == 0)
    def _():
        s_ref[...] = s0_ref[...]
        for j in range(3):
            buf_s[j, 0:head, :] = cs_ref[:, j * B_W:(j + 1) * B_W]

    if t_valid < t_total:
        row = t * tb + lax.broadcasted_iota(jnp.int32, (tb, 1), 0)
        live = row < t_valid
    else:
        live = None

    def keep(x):
        return x if live is None else jnp.where(live, x, 0.0)

    for j, (src, dst) in enumerate(((q_ref, qs_s), (k_ref, ks_s), (v_ref, vs_s))):
        x = src[...]
        buf_s[j, head:head + tb, :] = x
        w = cw_ref[:, j * B_W:(j + 1) * B_W]
        y = w[B_CONV - 1:B_CONV] * x
        for tap in range(B_CONV - 1):
            off = head - (B_CONV - 1) + tap
            y = y + w[tap:tap + 1] * buf_s[j, off:off + tb, :]
        buf_s[j, 0:head, :] = buf_s[j, tb:tb + head, :]
        y = keep(_silu(y))
        for h in range(B_HEADS):
            sl = slice(h * HEAD_DIM, (h + 1) * HEAD_DIM)
            seg = y[:, sl]
            if j == 0:
                seg = seg * lax.rsqrt(jnp.sum(seg * seg, axis=-1, keepdims=True) + EPS) * (HEAD_DIM ** -0.5)
            elif j == 1:
                seg = seg * lax.rsqrt(jnp.sum(seg * seg, axis=-1, keepdims=True) + EPS)
            dst[h] = seg

    ba = ba_ref[...]
    bt_s[...] = keep(_sigmoid(ba))
    g_s[...] = keep(-jnp.exp(alog_ref[...]) * _softplus(ba + dtb_ref[...]))

    n = DN_PAIR
    ii = lax.broadcasted_iota(jnp.int32, (n, n), 0)
    jj = lax.broadcasted_iota(jnp.int32, (n, n), 1)
    same = (ii >= DN_CHUNK) == (jj >= DN_CHUNK)
    incl = same & (ii >= jj)
    strict = same & (ii > jj)
    tri = jnp.where(incl, 1.0, 0.0)
    eye = jnp.where(ii == jj, 1.0, 0.0)
    rowi = lax.broadcasted_iota(jnp.int32, (n, 1), 0)
    lane = lax.broadcasted_iota(jnp.int32, (n, n), 1)
    zeros_half = jnp.zeros((DN_CHUNK, HEAD_DIM), F32)

    tri3 = jnp.concatenate([tri.astype(BF16)] * 3, axis=1)

    heads = range(B_HEADS)
    gcol_of = lambda gc, h: gc[:, B_HEADS + h:B_HEADS + h + 1]

    def intra(p, carry):
        r0 = pl.multiple_of(p * n, n)
        rows = pl.ds(r0, n)
        gc = _dot(tri3, jnp.concatenate(_split3(g_s[rows, :]), axis=0))
        gct = gc.T
        bt = bt_s[rows, :]
        gc_s[rows, :] = gc
        q = [qs_s[h, rows, :] for h in heads]
        k = [ks_s[h, rows, :] for h in heads]
        decay = [jnp.exp(jnp.where(incl, gcol_of(gc, h) - gct[B_HEADS + h:B_HEADS + h + 1, :], NEG_INF))
                 for h in heads]
        kb = [k[h] * bt[:, h:h + 1] for h in heads]
        kq = [_dot_nt(jnp.concatenate([kb[h].astype(BF16), q[h].astype(BF16)], axis=0), k[h].astype(BF16))
              for h in heads]
        a = [-jnp.where(strict, kq[h][0:n] * decay[h], 0.0) for h in heads]
        pm = [eye + a[h] for h in heads]
        a_hl = [_split2(a[h]) for h in heads]
        a = [_dot3(a_hl[h], a_hl[h]) for h in heads]
        for _ in range(int(math.log2(DN_CHUNK)) - 2):
            a_hl = [_split2(a[h]) for h in heads]
            pm_hl = [_split2(pm[h]) for h in heads]
            prod = [_dot3(tuple(jnp.concatenate([x, y], axis=0) for x, y in zip(a_hl[h], pm_hl[h])), a_hl[h])
                    for h in heads]
            a = [prod[h][0:n] for h in heads]
            pm = [pm[h] + prod[h][n:2 * n] for h in heads]
        prod = [_dot3(_split2(pm[h]), _split2(a[h])) for h in heads]
        pm = [pm[h] + prod[h] for h in heads]
        egc = [jnp.exp(gcol_of(gc, h)) for h in heads]
        sol = [_dot3(_split2(pm[h]),
                     _split2(jnp.concatenate([vs_s[h, rows, :] * bt[:, h:h + 1], kb[h] * egc[h]], axis=1)))
               for h in heads]
        for h in heads:
            u_s[h, rows, :] = sol[h][:, 0:HEAD_DIM]
            w_s[h, rows, :] = sol[h][:, HEAD_DIM:2 * HEAD_DIM].astype(BF16)
            ai_s[h, rows, :] = (kq[h][n:2 * n] * decay[h]).astype(BF16)
            qd_s[h, rows, :] = (q[h] * egc[h]).astype(BF16)
            gcol = gcol_of(gc, h)
            glast = jnp.where(rowi < DN_CHUNK, gcol[DN_CHUNK - 1:DN_CHUNK], gcol[n - 1:n])
            kdt = (k[h] * jnp.exp(glast - gcol)).T
            kd_s[h, p, 0] = jnp.where(lane < DN_CHUNK, kdt, 0.0).astype(BF16)
            kd_s[h, p, 1] = jnp.where(lane >= DN_CHUNK, kdt, 0.0).astype(BF16)
        return carry

    def inter(p, carry):
        r0 = pl.multiple_of(p * n, n)
        s = [s_ref[h] for h in heads]
        outs = [[] for _ in heads]
        for c in range(2):
            rs = pl.ds(pl.multiple_of(r0 + c * DN_CHUNK, DN_CHUNK), DN_CHUNK)
            ws = [_dot(jnp.concatenate([w_s[h, rs, :], qd_s[h, rs, :]], axis=0), s[h].astype(BF16)) for h in heads]
            v_new = [u_s[h, rs, :] - ws[h][0:DN_CHUNK] for h in heads]
            v_pad = [jnp.concatenate([v_new[h], zeros_half] if c == 0 else [zeros_half, v_new[h]],
                                     axis=0).astype(BF16) for h in heads]
            av = [_dot(jnp.concatenate([ai_s[h, rs, :], kd_s[h, p, c]], axis=0), v_pad[h]) for h in heads]
            last = gc_s[pl.ds(pl.multiple_of(r0 + (c + 1) * DN_CHUNK - SUBLANES, SUBLANES), SUBLANES), :]
            for h in heads:
                outs[h].append(ws[h][DN_CHUNK:2 * DN_CHUNK] + av[h][0:DN_CHUNK])
                gl = jnp.exp(last[SUBLANES - 1:SUBLANES, B_HEADS + h:B_HEADS + h + 1])
                s[h] = s[h] * gl + av[h][DN_CHUNK:DN_CHUNK + HEAD_DIM]
        for h in heads:
            sl = slice(h * HEAD_DIM, (h + 1) * HEAD_DIM)
            s_ref[h] = s[h]
            o = jnp.concatenate(outs[h], axis=0)
            z = z_ref[pl.ds(r0, n), sl]
            o_ref[pl.ds(r0, n), sl] = (_rms(o, nw_ref[...]) * _silu(z)).astype(o_ref.dtype)
        return carry

    lax.fori_loop(0, tb // n, intra, 0)
    lax.fori_loop(0, tb // n, inter, 0)


def _delta(main, conv_state, conv_w, alog, dtb, nw, s0, l, *, nb, t_total, t_valid, tb):
    m = main.shape[0]
    nt = t_total // tb
    cq, ck, cv, cz, cba = COL_BQ // B_W, COL_BK // B_W, COL_BV // B_W, COL_BZ // B_W, COL_BA // HEAD_DIM
    row = lambda b, t: b * nt + t
    return pl.pallas_call(
        functools.partial(_delta_kernel, tb=tb, t_valid=t_valid, t_total=t_total),
        grid=(nb, nt),
        in_specs=[pl.BlockSpec((tb, B_W), lambda b, t: (row(b, t), cq)),
                  pl.BlockSpec((tb, B_W), lambda b, t: (row(b, t), ck)),
                  pl.BlockSpec((tb, B_W), lambda b, t: (row(b, t), cv)),
                  pl.BlockSpec((tb, B_W), lambda b, t: (row(b, t), cz)),
                  pl.BlockSpec((tb, HEAD_DIM), lambda b, t: (row(b, t), cba)),
                  pl.BlockSpec((None, SUBLANES, 3 * B_W), lambda b, t: (b, 0, 0)),
                  pl.BlockSpec((None, B_CONV, 3 * B_W), lambda b, t: (l, 0, 0)),
                  pl.BlockSpec((None, 1, HEAD_DIM), lambda b, t: (l, 0, 0)),
                  pl.BlockSpec((None, 1, HEAD_DIM), lambda b, t: (l, 0, 0)),
                  pl.BlockSpec((None, 1, HEAD_DIM), lambda b, t: (l, 0, 0)),
                  pl.BlockSpec((None, B_HEADS, HEAD_DIM, HEAD_DIM), lambda b, t: (b, 0, 0, 0))],
        out_specs=[pl.BlockSpec((tb, B_W), lambda b, t: (row(b, t), 0)),
                   pl.BlockSpec((None, B_HEADS, HEAD_DIM, HEAD_DIM), lambda b, t: (b, 0, 0, 0))],
        out_shape=[jax.ShapeDtypeStruct((m, B_W), BF16),
                   jax.ShapeDtypeStruct((nb, B_HEADS, HEAD_DIM, HEAD_DIM), F32)],
        scratch_shapes=[pltpu.VMEM((3, tb + SUBLANES, B_W), F32),
                        pltpu.VMEM((B_HEADS, tb, HEAD_DIM), F32),
                        pltpu.VMEM((B_HEADS, tb, HEAD_DIM), F32),
                        pltpu.VMEM((B_HEADS, tb, HEAD_DIM), F32),
                        pltpu.VMEM((tb, HEAD_DIM), F32),
                        pltpu.VMEM((tb, HEAD_DIM), F32),
                        pltpu.VMEM((tb, HEAD_DIM), F32),
                        pltpu.VMEM((B_HEADS, tb, HEAD_DIM), F32),
                        pltpu.VMEM((B_HEADS, tb, HEAD_DIM), BF16),
                        pltpu.VMEM((B_HEADS, tb, HEAD_DIM), BF16),
                        pltpu.VMEM((B_HEADS, tb, HEAD_DIM), BF16),
                        pltpu.VMEM((B_HEADS, tb // DN_PAIR, 2, HEAD_DIM, DN_PAIR), BF16)],
        compiler_params=_cparams("parallel", "arbitrary"), name="deltanet")(
            main, main, main, main, main, conv_state, conv_w, alog, dtb, nw, s0)


def _bias_from_rel(rel, tbl_ref, h):
    bias = jnp.full(rel.shape, tbl_ref[h], F32)
    for b in range(1, N_BUCKETS):
        bias = jnp.where(rel >= BUCKET_START[b], tbl_ref[b * C_HEADS + h], bias)
    return bias


def _lambda(lamp_ref, lam_init):
    lp = lamp_ref[...]
    s1 = jnp.sum(lp[0:1] * lp[1:2], axis=-1, keepdims=True)
    s2 = jnp.sum(lp[2:3] * lp[3:4], axis=-1, keepdims=True)
    return jnp.exp(s1) - jnp.exp(s2) + lam_init


def _split_halves(qh):
    lane = lax.broadcasted_iota(jnp.int32, qh.shape, 1)
    return jnp.where(lane < C_HALF, qh, 0.0), jnp.where(lane >= C_HALF, qh, 0.0)


def _softmax_update(m_ref, l_ref, acc_ref, idx, s, pv):
    chunks = [s[:, c * HEAD_DIM:(c + 1) * HEAD_DIM] for c in range(s.shape[1] // HEAD_DIM)]
    mx = chunks[0]
    for c in chunks[1:]:
        mx = jnp.maximum(mx, c)
    m_old = m_ref[idx]
    m_new = jnp.maximum(m_old, jnp.max(mx, axis=-1, keepdims=True))
    p = [jnp.exp2(c - m_new) for c in chunks]
    ps = p[0]
    for c in p[1:]:
        ps = ps + c
    a = jnp.exp2(m_old - m_new)
    l_ref[idx] = a * l_ref[idx] + jnp.sum(ps, axis=-1, keepdims=True)
    acc_ref[idx] = a * acc_ref[idx] + pv(jnp.concatenate([c.astype(BF16) for c in p], axis=1))
    m_ref[idx] = m_new


def _prompt_attn_step(h, b, qi, tbl_ref, q_ref, k_ref, v_ref, lamp_ref, nw_ref, o_ref, ko_ref, vo_ref,
                      kb_s, vb_s, bias_s, m_s, l_s, acc_s, *, tq, lam_init):
    far_bias = tbl_ref[(N_BUCKETS - 1) * C_HEADS + h]

    @pl.when((qi == 0) & (b == 0))
    def _():
        r = lax.broadcasted_iota(jnp.int32, (tq, tq), 0) - lax.broadcasted_iota(jnp.int32, (tq, tq), 1)
        bias_s[0] = jnp.where(r >= 0, (_bias_from_rel(r, tbl_ref, h) - far_bias) * LOG2E, NEG_INF)
        bias_s[1] = (_bias_from_rel(r + tq, tbl_ref, h) - far_bias) * LOG2E

    @pl.when(qi == 0)
    def _():
        kf = k_ref[...]
        vf = v_ref[...]
        ko_ref[...] = kf
        vo_ref[...] = vf
        kb_s[...] = kf.astype(BF16)
        vb_s[...] = vf.astype(BF16)

    m_s[...] = jnp.full(m_s.shape, NEG_INF, F32)
    l_s[...] = jnp.zeros(l_s.shape, F32)
    acc_s[...] = jnp.zeros(acc_s.shape, F32)
    q1, q2 = _split_halves(q_ref[...] * (C_HALF ** -0.5 * LOG2E))
    q1 = q1.astype(BF16)
    q2 = q2.astype(BF16)

    def blocks(kblocks, biases):
        starts = [pl.multiple_of(kj * tq, tq) for kj in kblocks]
        scores = [[_dot_nt(qq, kb_s[pl.ds(st, tq), :]) for qq in (q1, q2)] for st in starts]
        for st, sc, bias in zip(starts, scores, biases):
            vb = vb_s[pl.ds(st, tq), :]
            for idx, s in enumerate(sc):
                if bias is not None:
                    s = s + bias
                _softmax_update(m_s, l_s, acc_s, idx, s, lambda p: _dot(p, vb))

    n_far = jnp.maximum(qi - 1, 0)

    def far_pair(j, carry):
        blocks([2 * j, 2 * j + 1], [None, None])
        return carry

    lax.fori_loop(0, n_far // 2, far_pair, 0)

    @pl.when(lax.rem(n_far, 2) == 1)
    def _():
        blocks([n_far - 1], [None])

    @pl.when(qi >= 1)
    def _():
        blocks([qi - 1, qi], [bias_s[1], bias_s[0]])

    @pl.when(qi == 0)
    def _():
        blocks([qi], [bias_s[0]])

    lam = _lambda(lamp_ref, lam_init)
    o = acc_s[0] / l_s[0] - lam * (acc_s[1] / l_s[1])
    o_ref[...] = (_rms(o, nw_ref[...]) * (1.0 - lam_init)).astype(o_ref.dtype)


N_PROMPT_IN, N_SAMPLE_IN, N_PROMPT_SCRATCH = 5, 3, 6


def _attn_kernel(pt_ref, tbl_ref, *refs, n_alias, pages, nb, nq, steps_s, chunks_s, tq, ts, page, lam_init):
    del pt_ref
    q_ref, k_ref, v_ref, lamp_ref, nw_ref = refs[:N_PROMPT_IN]
    refs = refs[N_PROMPT_IN + n_alias:]
    qs_ref, kn_ref, vn_ref = refs[:N_SAMPLE_IN]
    k_refs = refs[N_SAMPLE_IN:N_SAMPLE_IN + pages]
    v_refs = refs[N_SAMPLE_IN + pages:N_SAMPLE_IN + 2 * pages]
    o_ref, ko_ref, vo_ref, os_ref = refs[N_SAMPLE_IN + 2 * pages:N_SAMPLE_IN + 2 * pages + 4]
    scratch = refs[N_SAMPLE_IN + 2 * pages + 4:]
    h, b, qi = pl.program_id(0), pl.program_id(1), pl.program_id(2)
    _prompt_attn_step(h, b, qi, tbl_ref, q_ref, k_ref, v_ref, lamp_ref, nw_ref, o_ref, ko_ref, vo_ref,
                      *scratch[:N_PROMPT_SCRATCH], tq=tq, lam_init=lam_init)
    step = (h * nb + b) * nq + qi

    @pl.when(step < steps_s)
    def _():
        _sample_attn_step(lax.rem(step, chunks_s), chunks_s, tbl_ref, qs_ref, kn_ref, vn_ref, lamp_ref, nw_ref,
                          k_refs, v_refs, os_ref, *scratch[N_PROMPT_SCRATCH:], ts=ts, page=page, lam_init=lam_init)


def _attention(main, q_bm, kn_bm, vn_bm, cache_k, cache_v, page_table, tbl, lamp, nw, l, kv_stacks, *,
               depth, nb, t_total, tq, nb_s, ts, n_pool, lam_init):
    assert tq > FAR_DISTANCE
    m = main.shape[0]
    nq = t_total // tq
    page = cache_k.shape[2]
    n_pages = page_table.shape[0] // nb_s
    pages = math.gcd(n_pages, PAGES_PER_STEP)
    chunks_s = n_pages // pages
    steps_s = nb_s * chunks_s
    assert pages * page > FAR_DISTANCE + ts
    assert steps_s <= C_HEADS * nb * nq
    cq, ck, cv = COL_CQ // HEAD_DIM, COL_CK // HEAD_DIM, COL_CV // HEAD_DIM

    def s_step(h, b, i):
        return jnp.minimum((h * nb + b) * nq + i, steps_s - 1)

    def kv_page(k):
        def index(h, b, i, pt):
            s = s_step(h, b, i)
            return (l * n_pool + pt[(s // chunks_s) * n_pages + (s % chunks_s) * pages + k], 0, 0, 0)
        return pl.BlockSpec((None, C_HEADS, page, HEAD_DIM), index)

    tok = pl.BlockSpec((ts, C_W), lambda h, b, i, pt: (s_step(h, b, i) // chunks_s, 0))
    kv_shape = jax.ShapeDtypeStruct((depth, nb, C_HEADS, t_total, HEAD_DIM), F32)
    kv_spec = pl.BlockSpec((None, None, None, t_total, HEAD_DIM), lambda h, b, i, pt: (l, b, h, 0, 0))
    in_specs = [pl.BlockSpec(memory_space=pltpu.SMEM),
                pl.BlockSpec((tq, HEAD_DIM), lambda h, b, i, pt: (b * nq + i, cq + h)),
                pl.BlockSpec((t_total, HEAD_DIM), lambda h, b, i, pt: (b, ck + h)),
                pl.BlockSpec((t_total, HEAD_DIM), lambda h, b, i, pt: (b, cv + h)),
                pl.BlockSpec((None, 4, C_HALF), lambda h, b, i, pt: (l, 0, 0)),
                pl.BlockSpec((None, 1, HEAD_DIM), lambda h, b, i, pt: (l, 0, 0))]
    args = [tbl, main, main, main, lamp, nw]
    aliases = {}
    if kv_stacks is not None:
        aliases = {1 + len(args): 1, 2 + len(args): 2}
        in_specs += [pl.BlockSpec(memory_space=pl.ANY)] * 2
        args += list(kv_stacks)
    in_specs += [tok, tok, tok] + [kv_page(k) for k in range(pages)] + [kv_page(k) for k in range(pages)]
    args += [q_bm, kn_bm, vn_bm] + [cache_k] * pages + [cache_v] * pages
    rows_s = C_HEADS * 2 * ts
    grid_spec = pltpu.PrefetchScalarGridSpec(
        num_scalar_prefetch=1, grid=(C_HEADS, nb, nq), in_specs=in_specs,
        out_specs=[pl.BlockSpec((tq, HEAD_DIM), lambda h, b, i, pt: (b * nq + i, h)), kv_spec, kv_spec, tok],
        scratch_shapes=[pltpu.VMEM((t_total, HEAD_DIM), BF16),
                        pltpu.VMEM((t_total, HEAD_DIM), BF16),
                        pltpu.VMEM((2, tq, tq), F32),
                        pltpu.VMEM((2, tq, HEAD_DIM), F32),
                        pltpu.VMEM((2, tq, HEAD_DIM), F32),
                        pltpu.VMEM((2, tq, HEAD_DIM), F32),
                        pltpu.VMEM((rows_s, HEAD_DIM), F32),
                        pltpu.VMEM((rows_s, HEAD_DIM), F32),
                        pltpu.VMEM((rows_s, HEAD_DIM), F32),
                        pltpu.VMEM((page, C_W), F32),
                        pltpu.VMEM((page, C_W), F32)])
    return pl.pallas_call(
        functools.partial(_attn_kernel, n_alias=len(aliases), pages=pages, nb=nb, nq=nq, steps_s=steps_s,
                          chunks_s=chunks_s, tq=tq, ts=ts, page=page, lam_init=lam_init),
        grid_spec=grid_spec,
        out_shape=[jax.ShapeDtypeStruct((m, C_W), BF16), kv_shape, kv_shape,
                   jax.ShapeDtypeStruct((nb_s * ts, C_W), F32)],
        input_output_aliases=aliases,
        compiler_params=_cparams("arbitrary", "arbitrary", "arbitrary"), name="attention")(page_table, *args)


def _sample_attn_step(s_id, nsteps, tbl_ref, q_ref, kn_ref, vn_ref, lamp_ref, nw_ref, k_refs, v_refs, o_ref,
                      m_s, l_s, acc_s, kn_s, vn_s, *, ts, page, lam_init):
    width = len(k_refs) * page
    rows = 2 * ts

    @pl.when(s_id == 0)
    def _():
        m_s[...] = jnp.full(m_s.shape, NEG_INF, F32)
        l_s[...] = jnp.zeros(l_s.shape, F32)
        acc_s[...] = jnp.zeros(acc_s.shape, F32)

    q = q_ref[...] * (C_HALF ** -0.5 * LOG2E)
    qq = []
    for h in range(C_HEADS):
        q1, q2 = _split_halves(q[:, h * HEAD_DIM:(h + 1) * HEAD_DIM])
        qq.append(jnp.concatenate([q1, q2], axis=0).astype(BF16))

    def update(keys, vals, bias):
        s = jnp.concatenate([_dot_nt(qq[h], keys(h)) for h in range(C_HEADS)], axis=0)
        if bias is not None:
            s = s + bias

        def pv(p):
            return jnp.concatenate([_dot(p[h * rows:(h + 1) * rows], vals(h)) for h in range(C_HEADS)], axis=0)

        _softmax_update(m_s, l_s, acc_s, slice(None), s, pv)

    def past_keys(h):
        return jnp.concatenate([r[h].astype(BF16) for r in k_refs], axis=0)

    def past_vals(h):
        return jnp.concatenate([r[h].astype(BF16) for r in v_refs], axis=0)

    @pl.when(s_id < nsteps - 1)
    def _():
        update(past_keys, past_vals, None)

    @pl.when(s_id == nsteps - 1)
    def _():
        tq_pos = lax.rem(lax.broadcasted_iota(jnp.int32, (rows, width), 0), ts)
        rel_past = tq_pos + width - lax.broadcasted_iota(jnp.int32, (rows, width), 1)
        rel_new = (lax.rem(lax.broadcasted_iota(jnp.int32, (rows, page), 0), ts)
                   - lax.broadcasted_iota(jnp.int32, (rows, page), 1))
        bias_past, bias_new = [], []
        for h in range(C_HEADS):
            far_bias = tbl_ref[(N_BUCKETS - 1) * C_HEADS + h]
            bias_past.append((_bias_from_rel(rel_past, tbl_ref, h) - far_bias) * LOG2E)
            bias_new.append(jnp.where(rel_new >= 0, (_bias_from_rel(rel_new, tbl_ref, h) - far_bias) * LOG2E, NEG_INF))
        update(past_keys, past_vals, jnp.concatenate(bias_past, axis=0))
        kn_s[...] = jnp.zeros(kn_s.shape, F32)
        vn_s[...] = jnp.zeros(vn_s.shape, F32)
        kn_s[0:ts, :] = kn_ref[...]
        vn_s[0:ts, :] = vn_ref[...]
        update(lambda h: kn_s[:, h * HEAD_DIM:(h + 1) * HEAD_DIM].astype(BF16),
               lambda h: vn_s[:, h * HEAD_DIM:(h + 1) * HEAD_DIM].astype(BF16),
               jnp.concatenate(bias_new, axis=0))
        lam = _lambda(lamp_ref, lam_init)
        on = acc_s[...] / l_s[...]
        for h in range(C_HEADS):
            o = on[h * rows:h * rows + ts] - lam * on[h * rows + ts:(h + 1) * rows]
            o_ref[:, h * HEAD_DIM:(h + 1) * HEAD_DIM] = (_rms(o, nw_ref[...]) * (1.0 - lam_init)).astype(o_ref.dtype)


def _outproj_kernel(ob_ref, oc_ref, oa_ref, wb_ref, wc_ref, wa_ref, x_ref, nw_ref, xo_ref, ho_ref):
    acc = _dot(ob_ref[...], wb_ref[...]) + _dot(oc_ref[...], wc_ref[...]) + _dot(oa_ref[...], wa_ref[...])
    x = x_ref[...] + acc
    xo_ref[...] = x
    ho_ref[...] = _rms(x, nw_ref[...]).astype(ho_ref.dtype)


def _outproj(ob, oc, oa, w, x, nw, l, tm):
    m, d = x.shape
    return pl.pallas_call(
        _outproj_kernel, grid=(m // tm,),
        in_specs=[pl.BlockSpec((tm, B_W), lambda i: (i, 0)),
                  pl.BlockSpec((tm, C_W), lambda i: (i, 0)),
                  pl.BlockSpec((tm, A_W), lambda i: (i, 0)),
                  pl.BlockSpec((None, B_W, d), lambda i: (l, 0, 0)),
                  pl.BlockSpec((None, C_W, d), lambda i: (l, 1, 0)),
                  pl.BlockSpec((None, A_W, d), lambda i: (l, (B_W + C_W) // A_W, 0)),
                  pl.BlockSpec((tm, d), lambda i: (i, 0)),
                  pl.BlockSpec((None, 1, d), lambda i: (l, 0, 0))],
        out_specs=[pl.BlockSpec((tm, d), lambda i: (i, 0)), pl.BlockSpec((tm, d), lambda i: (i, 0))],
        out_shape=[jax.ShapeDtypeStruct((m, d), F32), jax.ShapeDtypeStruct((m, d), BF16)],
        compiler_params=_cparams("parallel"), name="outproj")(ob, oc, oa, w, w, w, x, nw)


def _ffn_up_kernel(h_ref, wg_ref, wu_ref, cw_ref, cb_ref, st_ref, o_ref, so_ref, buf_s, wg_s, wu_s,
                   *, tm, stride, head, blocks):
    i = pl.program_id(1)

    @pl.when(i == 0)
    def _():
        wg_s[...] = wg_ref[...].astype(BF16)
        wu_s[...] = wu_ref[...].astype(BF16)

    @pl.when(i % blocks == 0)
    def _():
        buf_s[0:head, :] = st_ref[...]

    hn = h_ref[...]
    g = _dot(hn, wg_s[...])
    u = _dot(hn, wu_s[...])
    buf_s[head:head + tm, :] = g
    y = cw_ref[FFN_CONV - 1:FFN_CONV, :] * g + cb_ref[...]
    for tap in range(FFN_CONV - 1):
        off = head - (FFN_CONV - 1 - tap) * stride
        y = y + cw_ref[tap:tap + 1, :] * buf_s[off:off + tm, :]
    o_ref[...] = (_gelu(y) * u).astype(o_ref.dtype)
    last = buf_s[tm:tm + head, :]
    so_ref[...] = last
    buf_s[0:head, :] = last


def _ffn_up(hn, wg, wu, cw, cb, state, l, *, tm, tn, stride, blocks):
    m, d = hn.shape
    dff = wg.shape[-1]
    head = state.shape[1]
    nseq = state.shape[0]
    return pl.pallas_call(
        functools.partial(_ffn_up_kernel, tm=tm, stride=stride, head=head, blocks=blocks),
        grid=(dff // tn, m // tm),
        in_specs=[pl.BlockSpec((tm, d), lambda j, i: (i, 0)),
                  pl.BlockSpec((None, d, tn), lambda j, i: (l, 0, j)),
                  pl.BlockSpec((None, d, tn), lambda j, i: (l, 0, j)),
                  pl.BlockSpec((None, FFN_CONV, tn), lambda j, i: (l, 0, j)),
                  pl.BlockSpec((None, 1, tn), lambda j, i: (l, 0, j)),
                  pl.BlockSpec((None, head, tn), lambda j, i: (i // blocks, 0, j))],
        out_specs=[pl.BlockSpec((tm, tn), lambda j, i: (i, j)),
                   pl.BlockSpec((None, head, tn), lambda j, i: (i // blocks, 0, j))],
        out_shape=[jax.ShapeDtypeStruct((m, dff), BF16), jax.ShapeDtypeStruct((nseq, head, dff), F32)],
        scratch_shapes=[pltpu.VMEM((tm + head, tn), F32), pltpu.VMEM((d, tn), BF16), pltpu.VMEM((d, tn), BF16)],
        compiler_params=_cparams("parallel", "arbitrary"), name="ffn_up")(hn, wg, wu, cw, cb, state)


def _ffn_down_kernel(h_ref, w_ref, x_ref, o_ref):
    o_ref[...] = x_ref[...] + _dot(h_ref[...], w_ref[...])


def _ffn_down(hid, w, x, l, tm):
    m, dff = hid.shape
    d = x.shape[1]
    nt, tn = w.shape[1], w.shape[3]
    return pl.pallas_call(
        _ffn_down_kernel, grid=(m // tm, nt),
        in_specs=[pl.BlockSpec((tm, dff), lambda i, j: (i, 0)),
                  pl.BlockSpec((None, None, dff, tn), lambda i, j: (l, j, 0, 0)),
                  pl.BlockSpec((tm, tn), lambda i, j: (i, j))],
        out_specs=pl.BlockSpec((tm, tn), lambda i, j: (i, j)),
        out_shape=jax.ShapeDtypeStruct((m, d), F32),
        compiler_params=_cparams("parallel", "parallel"), name="ffn_down")(hid, w, x)


def _pick(total, want):
    t = min(total, want)
    while total % t:
        t //= 2
    return t


def kernel(x_prompt, x_sample, cache_k, cache_v, state_conv_b, state_delta, state_ffn_conv, page_table, attn_norm_w, w_in, a_ln_w, a_ln_b, a_ws, a_bs, b_conv_w, b_a_log, b_dt_bias, b_norm_w, c_lam_q1, c_lam_k1, c_lam_q2, c_lam_k2, c_norm_w, rel_bias, w_out, ffn_norm_w, w_gate, w_up, ffn_conv_w, ffn_conv_b, w_down, final_norm_w):
    bp, tp, d = x_prompt.shape
    bs, ts, _ = x_sample.shape
    depth = w_in.shape[0]
    dff = w_gate.shape[-1]
    n_pool, page = cache_k.shape[1], cache_k.shape[2]
    mp, ms = bp * tp, bs * ts

    o_au, o_av, o_qkv, o_z, o_beta = 0, A_W, 2 * A_W, 2 * A_W + 3 * B_W, 2 * A_W + 4 * B_W
    o_cq = o_beta + 2 * B_HEADS
    o_ck, o_cv = o_cq + C_W, o_cq + 2 * C_W
    w_main = _col_tiles(jnp.concatenate([
        w_in[:, :, o_qkv:o_z], w_in[:, :, o_z:o_beta], w_in[:, :, o_cq:o_ck], w_in[:, :, o_beta:o_cq],
        jnp.zeros((depth, d, COL_AU - COL_BA - 2 * B_HEADS), w_in.dtype),
        w_in[:, :, o_au:o_qkv], w_in[:, :, o_ck:o_cv + C_W]], axis=-1).astype(BF16), TN)
    w_out_r = jnp.concatenate([w_out[:, A_W:], w_out[:, :A_W]], axis=1).astype(BF16)
    wg, wu = w_gate, w_up
    wd = _col_tiles(w_down.astype(BF16), TN)
    attn_nw = attn_norm_w.reshape(depth, 1, d)
    ffn_nw = ffn_norm_w.reshape(depth, 1, d)
    lnw = a_ln_w.reshape(depth, 1, A_W)
    lnb = a_ln_b.reshape(depth, 1, A_W)
    bcol_p = jnp.swapaxes(a_bs, 1, 2)
    tril = jnp.tril(jnp.ones((ts, ts), F32))
    assert ms <= A_CHUNK
    wmix_s = jnp.einsum('lhts,bc->lhtbsc', a_ws[:, :, :ts, :ts] * tril, jnp.eye(bs, dtype=F32)).reshape(
        depth, A_HEADS, ms, ms)
    wmix_s = jnp.pad(wmix_s, ((0, 0), (0, 0), (0, A_CHUNK - ms), (0, A_CHUNK - ms)))
    bcol_s = jnp.repeat(jnp.swapaxes(a_bs[:, :, :ts], 1, 2), bs, axis=1)
    pad_lanes = jnp.zeros((depth, 1, HEAD_DIM - 2 * B_HEADS), F32)
    alog = jnp.concatenate([jnp.zeros((depth, 1, B_HEADS), F32), b_a_log.reshape(depth, 1, B_HEADS), pad_lanes], -1)
    dtb = jnp.concatenate([jnp.zeros((depth, 1, B_HEADS), F32), b_dt_bias.reshape(depth, 1, B_HEADS), pad_lanes], -1)
    b_nw = b_norm_w.reshape(depth, 1, HEAD_DIM)
    c_nw = c_norm_w.reshape(depth, 1, HEAD_DIM)
    lamp = jnp.stack([c_lam_q1, c_lam_k1, c_lam_q2, c_lam_k2], axis=1)
    tbl = rel_bias.reshape(-1)
    ffn_cb = ffn_conv_b.reshape(depth, 1, dff)
    ck_flat = jnp.swapaxes(cache_k, 2, 3).reshape(depth * n_pool, C_HEADS, page, HEAD_DIM)
    cv_flat = jnp.swapaxes(cache_v, 2, 3).reshape(depth * n_pool, C_HEADS, page, HEAD_DIM)
    pt_flat = page_table.reshape(-1)

    conv_head = SUBLANES
    zero_conv_p = jnp.zeros((bp, conv_head, 3 * B_W), F32)
    zero_delta_p = jnp.zeros((bp, B_HEADS, HEAD_DIM, HEAD_DIM), F32)
    zero_ffn_p = jnp.zeros((bp, SUBLANES, dff), F32)
    ffn_head_s = max(SUBLANES, (FFN_CONV - 1) * bs)

    tm_in = _pick(mp, TM_IN)
    tm_out = _pick(mp, TM_OUT)
    tm_up = _pick(tp, TM_UP)
    tm_down = _pick(mp, TM_DOWN)
    tb_delta = _pick(tp, TB_DELTA)
    tq = _pick(tp, TQ)
    ts_pad_delta = DN_PAIR

    xp = x_prompt.reshape(mp, d)
    xs = jnp.swapaxes(x_sample, 0, 1).reshape(ms, d)

    def to_bm(a):
        return jnp.swapaxes(a.reshape(ts, bs, -1), 0, 1)

    def to_tm(a):
        return jnp.swapaxes(a, 0, 1).reshape(ms, -1)

    kv_stacks = None
    p_conv, p_delta, p_ffn = [], [], []
    s_k, s_v, s_conv, s_delta, s_ffn, s_av = [], [], [], [], [], []
    for l in range(depth):
        lam_init = 0.8 - 0.6 * math.exp(-0.3 * l)

        main = _inproj(xp, attn_nw, w_main, l, tm_in)
        main_s = _inproj(xs, attn_nw, w_main, l, ms)
        main_bm = to_bm(main_s)
        q_bm = main_bm[:, :, COL_CQ:COL_CQ + C_W].reshape(ms, C_W)
        kn_bm = main_bm[:, :, COL_CK:COL_CK + C_W]
        vn_bm = main_bm[:, :, COL_CV:COL_CV + C_W]
        out_c, k_stack, v_stack, out_c_s = _attention(
            main, q_bm, kn_bm.reshape(ms, C_W), vn_bm.reshape(ms, C_W), ck_flat, cv_flat, pt_flat, tbl, lamp, c_nw,
            l, kv_stacks, depth=depth, nb=bp, t_total=tp, tq=tq, nb_s=bs, ts=ts, n_pool=n_pool, lam_init=lam_init)
        kv_stacks = (k_stack, v_stack)

        (out_a,) = _gmlp(main, lnw, lnb, a_ws, bcol_p, l, rows=4 * A_CHUNK, chunk=A_CHUNK, causal_mask=True,
                         want_v=False)
        out_b, delta_new = _delta(main, zero_conv_p, b_conv_w, alog, dtb, b_nw, zero_delta_p, l,
                                  nb=bp, t_total=tp, t_valid=tp, tb=tb_delta)
        xp, hn2 = _outproj(out_b, out_c, out_a, w_out_r, xp, ffn_nw, l, tm_out)
        hid, ffn_last = _ffn_up(hn2, wg, wu, ffn_conv_w, ffn_cb, zero_ffn_p, l, tm=tm_up, tn=512, stride=1,
                                blocks=tp // tm_up)
        xp = _ffn_down(hid, wd, xp, l, tm_down)
        main3 = main.reshape(bp, tp, N_MAIN)
        p_conv.append(main3[:, tp - (B_CONV - 1):, COL_BQ:COL_BQ + 3 * B_W])
        p_delta.append(delta_new)
        p_ffn.append(ffn_last[:, SUBLANES - (FFN_CONV - 1):])

        out_a_s, va_s = _gmlp(main_s, lnw, lnb, wmix_s, bcol_s, l, rows=ms, chunk=A_CHUNK, causal_mask=False,
                              want_v=True)
        main_pad = jnp.pad(main_bm, ((0, 0), (0, ts_pad_delta - ts), (0, 0))).reshape(bs * ts_pad_delta, N_MAIN)
        conv_s = jnp.pad(state_conv_b[l], ((0, 0), (conv_head - (B_CONV - 1), 0), (0, 0)))
        out_b_s, delta_new_s = _delta(main_pad, conv_s, b_conv_w, alog, dtb, b_nw, state_delta[l], l,
                                      nb=bs, t_total=ts_pad_delta, t_valid=ts, tb=ts_pad_delta)
        out_b_s = to_tm(out_b_s.reshape(bs, ts_pad_delta, B_W)[:, :ts])
        out_c_s = to_tm(out_c_s.reshape(bs, ts, C_W)).astype(BF16)
        xs, hn2_s = _outproj(out_b_s, out_c_s, out_a_s, w_out_r, xs, ffn_nw, l, ms)
        ffn_state_s = jnp.swapaxes(state_ffn_conv[l], 0, 1).reshape(1, (FFN_CONV - 1) * bs, dff)
        ffn_state_s = jnp.pad(ffn_state_s, ((0, 0), (ffn_head_s - (FFN_CONV - 1) * bs, 0), (0, 0)))
        hid_s, ffn_last_s = _ffn_up(hn2_s, wg, wu, ffn_conv_w, ffn_cb, ffn_state_s, l, tm=ms, tn=512, stride=bs,
                                    blocks=1)
        xs = _ffn_down(hid_s, wd, xs, l, ms)
        s_k.append(kn_bm.reshape(bs, ts, C_HEADS, HEAD_DIM))
        s_v.append(vn_bm.reshape(bs, ts, C_HEADS, HEAD_DIM))
        s_conv.append(main_bm[:, ts - (B_CONV - 1):, COL_BQ:COL_BQ + 3 * B_W])
        s_delta.append(delta_new_s)
        s_ffn.append(jnp.swapaxes(ffn_last_s[0, ffn_head_s - (FFN_CONV - 1) * bs:].reshape(FFN_CONV - 1, bs, dff),
                                  0, 1))
        s_av.append(to_bm(va_s).reshape(bs, ts, A_HEADS, HEAD_DIM))

    y_prompt = _rmsnorm(xp, final_norm_w, _pick(mp, 512)).reshape(bp, tp, d)
    y_sample = jnp.swapaxes(_rmsnorm(xs, final_norm_w, ms).reshape(ts, bs, d), 0, 1)
    k_prompt, v_prompt = (jnp.swapaxes(a, 2, 3) for a in kv_stacks)
    return (y_prompt, y_sample,
            k_prompt, v_prompt, jnp.stack(p_conv), jnp.stack(p_delta), jnp.stack(p_ffn),
            jnp.stack(s_k), jnp.stack(s_v), jnp.stack(s_conv), jnp.stack(s_delta), jnp.stack(s_ffn),
            jnp.stack(s_av))
```

```python
import functools
import math

import numpy as np
import jax
import jax.numpy as jnp
from jax import lax
from jax.experimental import pallas as pl
from jax.experimental.pallas import tpu as pltpu

F32 = jnp.float32
BF16 = jnp.bfloat16
HI = lax.Precision.HIGHEST

HEAD_DIM = 128
SUBLANES = 8
A_HEADS, B_HEADS, C_HEADS = 4, 6, 6
A_W, B_W, C_W = A_HEADS * HEAD_DIM, B_HEADS * HEAD_DIM, C_HEADS * HEAD_DIM
A_CHUNK = 128
B_CONV = 4
DN_CHUNK = 64
DN_PAIR = 2 * DN_CHUNK
C_HALF = HEAD_DIM // 2
N_BUCKETS = 32
MAX_DISTANCE = 128
FFN_CONV = 3
EPS = 1e-6
NEG_INF = -1e30
VMEM_LIMIT = 56 * 1024 * 1024
TM_IN, TM_OUT, TM_UP, TM_DOWN, TB_DELTA, TQ = 1024, 512, 1024, 1024, 256, 512
TN = 512
PAGES_PER_STEP = 16
LOG2E = math.log2(math.e)

COL_BQ, COL_BK, COL_BV, COL_BZ, COL_CQ = 0, B_W, 2 * B_W, 3 * B_W, 4 * B_W
COL_BA = COL_CQ + C_W
COL_AU = COL_BA + 2 * HEAD_DIM
COL_AV = COL_AU + A_W
COL_CK = COL_AV + A_W
COL_CV = COL_CK + C_W
N_MAIN = COL_CV + C_W


def _t5_bucket_starts():
    n = np.arange(4 * MAX_DISTANCE)
    max_exact = N_BUCKETS // 2
    nf = np.maximum(n, max_exact).astype(np.float64)
    large = max_exact + (np.log(nf / max_exact) / math.log(MAX_DISTANCE / max_exact)
                         * (N_BUCKETS - max_exact)).astype(np.int64)
    bucket = np.where(n < max_exact, n, np.minimum(large, N_BUCKETS - 1))
    return [int(n[bucket == b].min()) for b in range(N_BUCKETS)]


BUCKET_START = _t5_bucket_starts()
FAR_DISTANCE = BUCKET_START[-1]


def _cparams(*sem):
    return pltpu.CompilerParams(dimension_semantics=sem, vmem_limit_bytes=VMEM_LIMIT)


def _gelu(x):
    return 0.5 * x * (1.0 + jnp.tanh(math.sqrt(2.0 / math.pi) * (x + 0.044715 * (x * x * x))))


def _sigmoid(x):
    return 1.0 / (1.0 + jnp.exp(-x))


def _silu(x):
    return x * _sigmoid(x)


def _softplus(x):
    return jnp.maximum(x, 0.0) + jnp.log(1.0 + jnp.exp(-jnp.abs(x)))


def _rms(x, w):
    return x * lax.rsqrt(jnp.mean(x * x, axis=-1, keepdims=True) + EPS) * w


def _dot(a, b, **kw):
    return jnp.dot(a, b, preferred_element_type=F32, **kw)


def _dot_nt(a, b, **kw):
    return lax.dot_general(a, b, (((1,), (1,)), ((), ())), preferred_element_type=F32, **kw)


def _split2(x):
    hi = x.astype(BF16)
    return hi, (x - hi.astype(F32)).astype(BF16)


def _split3(x):
    hi = x.astype(BF16)
    r = x - hi.astype(F32)
    mid = r.astype(BF16)
    return hi, mid, (r - mid.astype(F32)).astype(BF16)


def _dot3(a_hl, b_hl):
    (ah, al), (bh, bl) = a_hl, b_hl
    return _dot(jnp.concatenate([ah, ah, al], axis=1), jnp.concatenate([bh, bl, bh], axis=0))


def _rmsnorm_kernel(x_ref, w_ref, o_ref):
    o_ref[...] = _rms(x_ref[...], w_ref[...]).astype(o_ref.dtype)


def _rmsnorm(x, w, tm):
    m, d = x.shape
    return pl.pallas_call(
        _rmsnorm_kernel, grid=(m // tm,),
        in_specs=[pl.BlockSpec((tm, d), lambda i: (i, 0)), pl.BlockSpec((1, d), lambda i: (0, 0))],
        out_specs=pl.BlockSpec((tm, d), lambda i: (i, 0)),
        out_shape=jax.ShapeDtypeStruct((m, d), F32),
        compiler_params=_cparams("parallel"), name="final_rmsnorm")(x, w.reshape(1, d))


def _inproj_kernel(x_ref, nw_ref, w_ref, o_ref, hn_s):
    @pl.when(pl.program_id(1) == 0)
    def _():
        hn_s[...] = _rms(x_ref[...], nw_ref[...]).astype(BF16)

    o_ref[...] = _dot(hn_s[...], w_ref[...])


def _inproj(x, nw, w, l, tm, tn):
    m, d = x.shape
    n = w.shape[-1]
    return pl.pallas_call(
        _inproj_kernel, grid=(m // tm, n // tn),
        in_specs=[pl.BlockSpec((tm, d), lambda i, j: (i, 0)),
                  pl.BlockSpec((None, 1, d), lambda i, j: (l, 0, 0)),
                  pl.BlockSpec((None, d, tn), lambda i, j: (l, 0, j))],
        out_specs=pl.BlockSpec((tm, tn), lambda i, j: (i, j)),
        out_shape=jax.ShapeDtypeStruct((m, n), F32),
        scratch_shapes=[pltpu.VMEM((tm, d), BF16)],
        compiler_params=_cparams("parallel", "arbitrary"), name="inproj")(x, nw, w)


def _gmlp_kernel(u_ref, v_ref, lnw_ref, lnb_ref, w_ref, bcol_ref, *out_refs, chunk, rows, causal_mask):
    o_ref = out_refs[0]
    nchunk = max(rows // chunk, 1)
    live = min(rows, chunk)
    if causal_mask:
        ii = lax.broadcasted_iota(jnp.int32, (chunk, chunk), 0)
        jj = lax.broadcasted_iota(jnp.int32, (chunk, chunk), 1)
        keep = ii >= jj
    for h in range(A_HEADS):
        sl = slice(h * HEAD_DIM, (h + 1) * HEAD_DIM)
        u = _gelu(u_ref[:, sl])
        v = _gelu(v_ref[:, sl])
        mu = jnp.mean(v, axis=-1, keepdims=True)
        vc = v - mu
        vn = vc * lax.rsqrt(jnp.mean(vc * vc, axis=-1, keepdims=True) + EPS) * lnw_ref[:, sl] + lnb_ref[:, sl]
        if len(out_refs) > 1:
            out_refs[1][:, sl] = vn
        w = w_ref[h]
        if causal_mask:
            w = jnp.where(keep, w, 0.0)
        wb = w.astype(BF16)
        vb = vn.astype(BF16)
        bcol = bcol_ref[0:live, h:h + 1]
        if live < chunk:
            vb = jnp.concatenate([vb, jnp.zeros((chunk - live, HEAD_DIM), BF16)], axis=0)
        for c in range(nchunk):
            rs = slice(c * live, (c + 1) * live)
            mixed = _dot(wb, vb[c * chunk:(c + 1) * chunk])[0:live] + bcol
            o_ref[rs, sl] = (u[rs] * mixed).astype(o_ref.dtype)


def _gmlp(main, lnw, lnb, wmix, bcol, l, *, rows, chunk, causal_mask, want_v):
    m = main.shape[0]
    live = min(rows, chunk)
    cu, cv = COL_AU // A_W, COL_AV // A_W
    out_shape = [jax.ShapeDtypeStruct((m, A_W), BF16)]
    out_specs = [pl.BlockSpec((rows, A_W), lambda i: (i, 0))]
    if want_v:
        out_shape.append(jax.ShapeDtypeStruct((m, A_W), F32))
        out_specs.append(pl.BlockSpec((rows, A_W), lambda i: (i, 0)))
    return pl.pallas_call(
        functools.partial(_gmlp_kernel, chunk=chunk, rows=rows, causal_mask=causal_mask),
        grid=(m // rows,),
        in_specs=[pl.BlockSpec((rows, A_W), lambda i: (i, cu)),
                  pl.BlockSpec((rows, A_W), lambda i: (i, cv)),
                  pl.BlockSpec((None, 1, A_W), lambda i: (l, 0, 0)),
                  pl.BlockSpec((None, 1, A_W), lambda i: (l, 0, 0)),
                  pl.BlockSpec((None, A_HEADS, chunk, chunk), lambda i: (l, 0, 0, 0)),
                  pl.BlockSpec((None, live, A_HEADS), lambda i: (l, 0, 0))],
        out_specs=out_specs, out_shape=out_shape,
        compiler_params=_cparams("parallel"), name="gmlp")(main, main, lnw, lnb, wmix, bcol)


def _delta_kernel(q_ref, k_ref, v_ref, z_ref, ba_ref, cs_ref, cw_ref, alog_ref, dtb_ref, nw_ref, s0_ref,
                  o_ref, s_ref, buf_s, qs_s, ks_s, vs_s, g_s, bt_s, gc_s, u_s, w_s, ai_s, qd_s, kd_s,
                  *, tb, t_valid, t_total):
    t = pl.program_id(1)
    head = SUBLANES

    @pl.when(t == 0)
    def _():
        s_ref[...] = s0_ref[...]
        for j in range(3):
            buf_s[j, 0:head, :] = cs_ref[:, j * B_W:(j + 1) * B_W]

    if t_valid < t_total:
        row = t * tb + lax.broadcasted_iota(jnp.int32, (tb, 1), 0)
        live = row < t_valid
    else:
        live = None

    def keep(x):
        return x if live is None else jnp.where(live, x, 0.0)

    for j, (src, dst) in enumerate(((q_ref, qs_s), (k_ref, ks_s), (v_ref, vs_s))):
        x = src[...]
        buf_s[j, head:head + tb, :] = x
        w = cw_ref[:, j * B_W:(j + 1) * B_W]
        y = w[B_CONV - 1:B_CONV] * x
        for tap in range(B_CONV - 1):
            off = head - (B_CONV - 1) + tap
            y = y + w[tap:tap + 1] * buf_s[j, off:off + tb, :]
        buf_s[j, 0:head, :] = buf_s[j, tb:tb + head, :]
        y = keep(_silu(y))
        for h in range(B_HEADS):
            sl = slice(h * HEAD_DIM, (h + 1) * HEAD_DIM)
            seg = y[:, sl]
            if j == 0:
                seg = seg * lax.rsqrt(jnp.sum(seg * seg, axis=-1, keepdims=True) + EPS) * (HEAD_DIM ** -0.5)
            elif j == 1:
                seg = seg * lax.rsqrt(jnp.sum(seg * seg, axis=-1, keepdims=True) + EPS)
            dst[h] = seg

    ba = ba_ref[...]
    bt_s[...] = keep(_sigmoid(ba))
    g_s[...] = keep(-jnp.exp(alog_ref[...]) * _softplus(ba + dtb_ref[...]))

    n = DN_PAIR
    ii = lax.broadcasted_iota(jnp.int32, (n, n), 0)
    jj = lax.broadcasted_iota(jnp.int32, (n, n), 1)
    same = (ii >= DN_CHUNK) == (jj >= DN_CHUNK)
    incl = same & (ii >= jj)
    strict = same & (ii > jj)
    tri = jnp.where(incl, 1.0, 0.0)
    eye = jnp.where(ii == jj, 1.0, 0.0)
    rowi = lax.broadcasted_iota(jnp.int32, (n, 1), 0)
    lane = lax.broadcasted_iota(jnp.int32, (n, n), 1)
    zeros_half = jnp.zeros((DN_CHUNK, HEAD_DIM), F32)

    tri3 = jnp.concatenate([tri.astype(BF16)] * 3, axis=1)

    heads = range(B_HEADS)
    gcol_of = lambda gc, h: gc[:, B_HEADS + h:B_HEADS + h + 1]

    def intra(p, carry):
        r0 = pl.multiple_of(p * n, n)
        rows = pl.ds(r0, n)
        gc = _dot(tri3, jnp.concatenate(_split3(g_s[rows, :]), axis=0))
        gct = gc.T
        bt = bt_s[rows, :]
        gc_s[rows, :] = gc
        q = [qs_s[h, rows, :] for h in heads]
        k = [ks_s[h, rows, :] for h in heads]
        decay = [jnp.exp(jnp.where(incl, gcol_of(gc, h) - gct[B_HEADS + h:B_HEADS + h + 1, :], NEG_INF))
                 for h in heads]
        kb = [k[h] * bt[:, h:h + 1] for h in heads]
        kq = [_dot_nt(jnp.concatenate([kb[h].astype(BF16), q[h].astype(BF16)], axis=0), k[h].astype(BF16))
              for h in heads]
        a = [-jnp.where(strict, kq[h][0:n] * decay[h], 0.0) for h in heads]
        pm = [eye + a[h] for h in heads]
        a_hl = [_split2(a[h]) for h in heads]
        a = [_dot3(a_hl[h], a_hl[h]) for h in heads]
        for _ in range(int(math.log2(DN_CHUNK)) - 2):
            a_hl = [_split2(a[h]) for h in heads]
            pm_hl = [_split2(pm[h]) for h in heads]
            prod = [_dot3(tuple(jnp.concatenate([x, y], axis=0) for x, y in zip(a_hl[h], pm_hl[h])), a_hl[h])
                    for h in heads]
            a = [prod[h][0:n] for h in heads]
            pm = [pm[h] + prod[h][n:2 * n] for h in heads]
        prod = [_dot3(_split2(pm[h]), _split2(a[h])) for h in heads]
        pm = [pm[h] + prod[h] for h in heads]
        egc = [jnp.exp(gcol_of(gc, h)) for h in heads]
        sol = [_dot3(_split2(pm[h]),
                     _split2(jnp.concatenate([vs_s[h, rows, :] * bt[:, h:h + 1], kb[h] * egc[h]], axis=1)))
               for h in heads]
        for h in heads:
            u_s[h, rows, :] = sol[h][:, 0:HEAD_DIM]
            w_s[h, rows, :] = sol[h][:, HEAD_DIM:2 * HEAD_DIM].astype(BF16)
            ai_s[h, rows, :] = (kq[h][n:2 * n] * decay[h]).astype(BF16)
            qd_s[h, rows, :] = (q[h] * egc[h]).astype(BF16)
            gcol = gcol_of(gc, h)
            glast = jnp.where(rowi < DN_CHUNK, gcol[DN_CHUNK - 1:DN_CHUNK], gcol[n - 1:n])
            kdt = (k[h] * jnp.exp(glast - gcol)).T
            kd_s[h, p, 0] = jnp.where(lane < DN_CHUNK, kdt, 0.0).astype(BF16)
            kd_s[h, p, 1] = jnp.where(lane >= DN_CHUNK, kdt, 0.0).astype(BF16)
        return carry

    def inter(p, carry):
        r0 = pl.multiple_of(p * n, n)
        s = [s_ref[h] for h in heads]
        outs = [[] for _ in heads]
        for c in range(2):
            rs = pl.ds(pl.multiple_of(r0 + c * DN_CHUNK, DN_CHUNK), DN_CHUNK)
            ws = [_dot(jnp.concatenate([w_s[h, rs, :], qd_s[h, rs, :]], axis=0), s[h].astype(BF16)) for h in heads]
            v_new = [u_s[h, rs, :] - ws[h][0:DN_CHUNK] for h in heads]
            v_pad = [jnp.concatenate([v_new[h], zeros_half] if c == 0 else [zeros_half, v_new[h]],
                                     axis=0).astype(BF16) for h in heads]
            av = [_dot(jnp.concatenate([ai_s[h, rs, :], kd_s[h, p, c]], axis=0), v_pad[h]) for h in heads]
            last = gc_s[pl.ds(pl.multiple_of(r0 + (c + 1) * DN_CHUNK - SUBLANES, SUBLANES), SUBLANES), :]
            for h in heads:
                outs[h].append(ws[h][DN_CHUNK:2 * DN_CHUNK] + av[h][0:DN_CHUNK])
                gl = jnp.exp(last[SUBLANES - 1:SUBLANES, B_HEADS + h:B_HEADS + h + 1])
                s[h] = s[h] * gl + av[h][DN_CHUNK:DN_CHUNK + HEAD_DIM]
        for h in heads:
            sl = slice(h * HEAD_DIM, (h + 1) * HEAD_DIM)
            s_ref[h] = s[h]
            o = jnp.concatenate(outs[h], axis=0)
            z = z_ref[pl.ds(r0, n), sl]
            o_ref[pl.ds(r0, n), sl] = (_rms(o, nw_ref[...]) * _silu(z)).astype(o_ref.dtype)
        return carry

    lax.fori_loop(0, tb // n, intra, 0)
    lax.fori_loop(0, tb // n, inter, 0)


def _delta(main, conv_state, conv_w, alog, dtb, nw, s0, l, *, nb, t_total, t_valid, tb):
    m = main.shape[0]
    nt = t_total // tb
    cq, ck, cv, cz, cba = COL_BQ // B_W, COL_BK // B_W, COL_BV // B_W, COL_BZ // B_W, COL_BA // HEAD_DIM
    row = lambda b, t: b * nt + t
    return pl.pallas_call(
        functools.partial(_delta_kernel, tb=tb, t_valid=t_valid, t_total=t_total),
        grid=(nb, nt),
        in_specs=[pl.BlockSpec((tb, B_W), lambda b, t: (row(b, t), cq)),
                  pl.BlockSpec((tb, B_W), lambda b, t: (row(b, t), ck)),
                  pl.BlockSpec((tb, B_W), lambda b, t: (row(b, t), cv)),
                  pl.BlockSpec((tb, B_W), lambda b, t: (row(b, t), cz)),
                  pl.BlockSpec((tb, HEAD_DIM), lambda b, t: (row(b, t), cba)),
                  pl.BlockSpec((None, SUBLANES, 3 * B_W), lambda b, t: (b, 0, 0)),
                  pl.BlockSpec((None, B_CONV, 3 * B_W), lambda b, t: (l, 0, 0)),
                  pl.BlockSpec((None, 1, HEAD_DIM), lambda b, t: (l, 0, 0)),
                  pl.BlockSpec((None, 1, HEAD_DIM), lambda b, t: (l, 0, 0)),
                  pl.BlockSpec((None, 1, HEAD_DIM), lambda b, t: (l, 0, 0)),
                  pl.BlockSpec((None, B_HEADS, HEAD_DIM, HEAD_DIM), lambda b, t: (b, 0, 0, 0))],
        out_specs=[pl.BlockSpec((tb, B_W), lambda b, t: (row(b, t), 0)),
                   pl.BlockSpec((None, B_HEADS, HEAD_DIM, HEAD_DIM), lambda b, t: (b, 0, 0, 0))],
        out_shape=[jax.ShapeDtypeStruct((m, B_W), BF16),
                   jax.ShapeDtypeStruct((nb, B_HEADS, HEAD_DIM, HEAD_DIM), F32)],
        scratch_shapes=[pltpu.VMEM((3, tb + SUBLANES, B_W), F32),
                        pltpu.VMEM((B_HEADS, tb, HEAD_DIM), F32),
                        pltpu.VMEM((B_HEADS, tb, HEAD_DIM), F32),
                        pltpu.VMEM((B_HEADS, tb, HEAD_DIM), F32),
                        pltpu.VMEM((tb, HEAD_DIM), F32),
                        pltpu.VMEM((tb, HEAD_DIM), F32),
                        pltpu.VMEM((tb, HEAD_DIM), F32),
                        pltpu.VMEM((B_HEADS, tb, HEAD_DIM), F32),
                        pltpu.VMEM((B_HEADS, tb, HEAD_DIM), BF16),
                        pltpu.VMEM((B_HEADS, tb, HEAD_DIM), BF16),
                        pltpu.VMEM((B_HEADS, tb, HEAD_DIM), BF16),
                        pltpu.VMEM((B_HEADS, tb // DN_PAIR, 2, HEAD_DIM, DN_PAIR), BF16)],
        compiler_params=_cparams("parallel", "arbitrary"), name="deltanet")(
            main, main, main, main, main, conv_state, conv_w, alog, dtb, nw, s0)


def _bias_from_rel(rel, tbl_ref, h):
    bias = jnp.full(rel.shape, tbl_ref[h], F32)
    for b in range(1, N_BUCKETS):
        bias = jnp.where(rel >= BUCKET_START[b], tbl_ref[b * C_HEADS + h], bias)
    return bias


def _lambda(lamp_ref, lam_init):
    lp = lamp_ref[...]
    s1 = jnp.sum(lp[0:1] * lp[1:2], axis=-1, keepdims=True)
    s2 = jnp.sum(lp[2:3] * lp[3:4], axis=-1, keepdims=True)
    return jnp.exp(s1) - jnp.exp(s2) + lam_init


def _split_halves(qh):
    lane = lax.broadcasted_iota(jnp.int32, qh.shape, 1)
    return jnp.where(lane < C_HALF, qh, 0.0), jnp.where(lane >= C_HALF, qh, 0.0)


def _softmax_update(m_ref, l_ref, acc_ref, idx, s, pv):
    chunks = [s[:, c * HEAD_DIM:(c + 1) * HEAD_DIM] for c in range(s.shape[1] // HEAD_DIM)]
    mx = chunks[0]
    for c in chunks[1:]:
        mx = jnp.maximum(mx, c)
    m_old = m_ref[idx]
    m_new = jnp.maximum(m_old, jnp.max(mx, axis=-1, keepdims=True))
    p = [jnp.exp2(c - m_new) for c in chunks]
    ps = p[0]
    for c in p[1:]:
        ps = ps + c
    a = jnp.exp2(m_old - m_new)
    l_ref[idx] = a * l_ref[idx] + jnp.sum(ps, axis=-1, keepdims=True)
    acc_ref[idx] = a * acc_ref[idx] + pv(jnp.concatenate([c.astype(BF16) for c in p], axis=1))
    m_ref[idx] = m_new


def _prompt_attn_step(h, b, qi, tbl_ref, q_ref, k_ref, v_ref, lamp_ref, nw_ref, o_ref, ko_ref, vo_ref,
                      kb_s, vb_s, bias_s, m_s, l_s, acc_s, *, tq, lam_init):
    far_bias = tbl_ref[(N_BUCKETS - 1) * C_HEADS + h]

    @pl.when((qi == 0) & (b == 0))
    def _():
        r = lax.broadcasted_iota(jnp.int32, (tq, tq), 0) - lax.broadcasted_iota(jnp.int32, (tq, tq), 1)
        bias_s[0] = jnp.where(r >= 0, (_bias_from_rel(r, tbl_ref, h) - far_bias) * LOG2E, NEG_INF)
        bias_s[1] = (_bias_from_rel(r + tq, tbl_ref, h) - far_bias) * LOG2E

    @pl.when(qi == 0)
    def _():
        kf = k_ref[...]
        vf = v_ref[...]
        ko_ref[...] = kf
        vo_ref[...] = vf
        kb_s[...] = kf.astype(BF16)
        vb_s[...] = vf.astype(BF16)

    m_s[...] = jnp.full(m_s.shape, NEG_INF, F32)
    l_s[...] = jnp.zeros(l_s.shape, F32)
    acc_s[...] = jnp.zeros(acc_s.shape, F32)
    q1, q2 = _split_halves(q_ref[...] * (C_HALF ** -0.5 * LOG2E))
    q1 = q1.astype(BF16)
    q2 = q2.astype(BF16)

    def blocks(kblocks, biases):
        starts = [pl.multiple_of(kj * tq, tq) for kj in kblocks]
        scores = [[_dot_nt(qq, kb_s[pl.ds(st, tq), :]) for qq in (q1, q2)] for st in starts]
        for st, sc, bias in zip(starts, scores, biases):
            vb = vb_s[pl.ds(st, tq), :]
            for idx, s in enumerate(sc):
                if bias is not None:
                    s = s + bias
                _softmax_update(m_s, l_s, acc_s, idx, s, lambda p: _dot(p, vb))

    n_far = jnp.maximum(qi - 1, 0)

    def far_pair(j, carry):
        blocks([2 * j, 2 * j + 1], [None, None])
        return carry

    lax.fori_loop(0, n_far // 2, far_pair, 0)

    @pl.when(lax.rem(n_far, 2) == 1)
    def _():
        blocks([n_far - 1], [None])

    @pl.when(qi >= 1)
    def _():
        blocks([qi - 1, qi], [bias_s[1], bias_s[0]])

    @pl.when(qi == 0)
    def _():
        blocks([qi], [bias_s[0]])

    lam = _lambda(lamp_ref, lam_init)
    o = acc_s[0] / l_s[0] - lam * (acc_s[1] / l_s[1])
    o_ref[...] = (_rms(o, nw_ref[...]) * (1.0 - lam_init)).astype(o_ref.dtype)


N_PROMPT_IN, N_SAMPLE_IN, N_PROMPT_SCRATCH = 5, 3, 6


def _attn_kernel(pt_ref, tbl_ref, *refs, n_alias, pages, nb, nq, steps_s, chunks_s, tq, ts, page, lam_init):
    del pt_ref
    q_ref, k_ref, v_ref, lamp_ref, nw_ref = refs[:N_PROMPT_IN]
    refs = refs[N_PROMPT_IN + n_alias:]
    qs_ref, kn_ref, vn_ref = refs[:N_SAMPLE_IN]
    k_refs = refs[N_SAMPLE_IN:N_SAMPLE_IN + pages]
    v_refs = refs[N_SAMPLE_IN + pages:N_SAMPLE_IN + 2 * pages]
    o_ref, ko_ref, vo_ref, os_ref = refs[N_SAMPLE_IN + 2 * pages:N_SAMPLE_IN + 2 * pages + 4]
    scratch = refs[N_SAMPLE_IN + 2 * pages + 4:]
    h, b, qi = pl.program_id(0), pl.program_id(1), pl.program_id(2)
    _prompt_attn_step(h, b, qi, tbl_ref, q_ref, k_ref, v_ref, lamp_ref, nw_ref, o_ref, ko_ref, vo_ref,
                      *scratch[:N_PROMPT_SCRATCH], tq=tq, lam_init=lam_init)
    step = (h * nb + b) * nq + qi

    @pl.when(step < steps_s)
    def _():
        _sample_attn_step(lax.rem(step, chunks_s), chunks_s, tbl_ref, qs_ref, kn_ref, vn_ref, lamp_ref, nw_ref,
                          k_refs, v_refs, os_ref, *scratch[N_PROMPT_SCRATCH:], ts=ts, page=page, lam_init=lam_init)


def _attention(main, q_bm, kn_bm, vn_bm, cache_k, cache_v, page_table, tbl, lamp, nw, l, kv_stacks, *,
               depth, nb, t_total, tq, nb_s, ts, n_pool, lam_init):
    assert tq > FAR_DISTANCE
    m = main.shape[0]
    nq = t_total // tq
    page = cache_k.shape[2]
    n_pages = page_table.shape[0] // nb_s
    pages = math.gcd(n_pages, PAGES_PER_STEP)
    chunks_s = n_pages // pages
    steps_s = nb_s * chunks_s
    assert pages * page > FAR_DISTANCE + ts
    assert steps_s <= C_HEADS * nb * nq
    cq, ck, cv = COL_CQ // HEAD_DIM, COL_CK // HEAD_DIM, COL_CV // HEAD_DIM

    def s_step(h, b, i):
        return jnp.minimum((h * nb + b) * nq + i, steps_s - 1)

    def kv_page(k):
        def index(h, b, i, pt):
            s = s_step(h, b, i)
            return (l * n_pool + pt[(s // chunks_s) * n_pages + (s % chunks_s) * pages + k], 0, 0, 0)
        return pl.BlockSpec((None, C_HEADS, page, HEAD_DIM), index)

    tok = pl.BlockSpec((ts, C_W), lambda h, b, i, pt: (s_step(h, b, i) // chunks_s, 0))
    kv_shape = jax.ShapeDtypeStruct((depth, nb, C_HEADS, t_total, HEAD_DIM), F32)
    kv_spec = pl.BlockSpec((None, None, None, t_total, HEAD_DIM), lambda h, b, i, pt: (l, b, h, 0, 0))
    in_specs = [pl.BlockSpec(memory_space=pltpu.SMEM),
                pl.BlockSpec((tq, HEAD_DIM), lambda h, b, i, pt: (b * nq + i, cq + h)),
                pl.BlockSpec((t_total, HEAD_DIM), lambda h, b, i, pt: (b, ck + h)),
                pl.BlockSpec((t_total, HEAD_DIM), lambda h, b, i, pt: (b, cv + h)),
                pl.BlockSpec((None, 4, C_HALF), lambda h, b, i, pt: (l, 0, 0)),
                pl.BlockSpec((None, 1, HEAD_DIM), lambda h, b, i, pt: (l, 0, 0))]
    args = [tbl, main, main, main, lamp, nw]
    aliases = {}
    if kv_stacks is not None:
        aliases = {1 + len(args): 1, 2 + len(args): 2}
        in_specs += [pl.BlockSpec(memory_space=pl.ANY)] * 2
        args += list(kv_stacks)
    in_specs += [tok, tok, tok] + [kv_page(k) for k in range(pages)] + [kv_page(k) for k in range(pages)]
    args += [q_bm, kn_bm, vn_bm] + [cache_k] * pages + [cache_v] * pages
    rows_s = C_HEADS * 2 * ts
    grid_spec = pltpu.PrefetchScalarGridSpec(
        num_scalar_prefetch=1, grid=(C_HEADS, nb, nq), in_specs=in_specs,
        out_specs=[pl.BlockSpec((tq, HEAD_DIM), lambda h, b, i, pt: (b * nq + i, h)), kv_spec, kv_spec, tok],
        scratch_shapes=[pltpu.VMEM((t_total, HEAD_DIM), BF16),
                        pltpu.VMEM((t_total, HEAD_DIM), BF16),
                        pltpu.VMEM((2, tq, tq), F32),
                        pltpu.VMEM((2, tq, HEAD_DIM), F32),
                        pltpu.VMEM((2, tq, HEAD_DIM), F32),
                        pltpu.VMEM((2, tq, HEAD_DIM), F32),
                        pltpu.VMEM((rows_s, HEAD_DIM), F32),
                        pltpu.VMEM((rows_s, HEAD_DIM), F32),
                        pltpu.VMEM((rows_s, HEAD_DIM), F32),
                        pltpu.VMEM((page, C_W), F32),
                        pltpu.VMEM((page, C_W), F32)])
    return pl.pallas_call(
        functools.partial(_attn_kernel, n_alias=len(aliases), pages=pages, nb=nb, nq=nq, steps_s=steps_s,
                          chunks_s=chunks_s, tq=tq, ts=ts, page=page, lam_init=lam_init),
        grid_spec=grid_spec,
        out_shape=[jax.ShapeDtypeStruct((m, C_W), BF16), kv_shape, kv_shape,
                   jax.ShapeDtypeStruct((nb_s * ts, C_W), F32)],
        input_output_aliases=aliases,
        compiler_params=_cparams("arbitrary", "arbitrary", "arbitrary"), name="attention")(page_table, *args)


def _sample_attn_step(s_id, nsteps, tbl_ref, q_ref, kn_ref, vn_ref, lamp_ref, nw_ref, k_refs, v_refs, o_ref,
                      m_s, l_s, acc_s, kn_s, vn_s, *, ts, page, lam_init):
    width = len(k_refs) * page
    rows = 2 * ts

    @pl.when(s_id == 0)
    def _():
        m_s[...] = jnp.full(m_s.shape, NEG_INF, F32)
        l_s[...] = jnp.zeros(l_s.shape, F32)
        acc_s[...] = jnp.zeros(acc_s.shape, F32)

    q = q_ref[...] * (C_HALF ** -0.5 * LOG2E)
    qq = []
    for h in range(C_HEADS):
        q1, q2 = _split_halves(q[:, h * HEAD_DIM:(h + 1) * HEAD_DIM])
        qq.append(jnp.concatenate([q1, q2], axis=0).astype(BF16))

    def update(keys, vals, bias):
        s = jnp.concatenate([_dot_nt(qq[h], keys(h)) for h in range(C_HEADS)], axis=0)
        if bias is not None:
            s = s + bias

        def pv(p):
            return jnp.concatenate([_dot(p[h * rows:(h + 1) * rows], vals(h)) for h in range(C_HEADS)], axis=0)

        _softmax_update(m_s, l_s, acc_s, slice(None), s, pv)

    def past_keys(h):
        return jnp.concatenate([r[h].astype(BF16) for r in k_refs], axis=0)

    def past_vals(h):
        return jnp.concatenate([r[h].astype(BF16) for r in v_refs], axis=0)

    @pl.when(s_id < nsteps - 1)
    def _():
        update(past_keys, past_vals, None)

    @pl.when(s_id == nsteps - 1)
    def _():
        tq_pos = lax.rem(lax.broadcasted_iota(jnp.int32, (rows, width), 0), ts)
        rel_past = tq_pos + width - lax.broadcasted_iota(jnp.int32, (rows, width), 1)
        rel_new = (lax.rem(lax.broadcasted_iota(jnp.int32, (rows, page), 0), ts)
                   - lax.broadcasted_iota(jnp.int32, (rows, page), 1))
        bias_past, bias_new = [], []
        for h in range(C_HEADS):
            far_bias = tbl_ref[(N_BUCKETS - 1) * C_HEADS + h]
            bias_past.append((_bias_from_rel(rel_past, tbl_ref, h) - far_bias) * LOG2E)
            bias_new.append(jnp.where(rel_new >= 0, (_bias_from_rel(rel_new, tbl_ref, h) - far_bias) * LOG2E, NEG_INF))
        update(past_keys, past_vals, jnp.concatenate(bias_past, axis=0))
        kn_s[...] = jnp.zeros(kn_s.shape, F32)
        vn_s[...] = jnp.zeros(vn_s.shape, F32)
        kn_s[0:ts, :] = kn_ref[...]
        vn_s[0:ts, :] = vn_ref[...]
        update(lambda h: kn_s[:, h * HEAD_DIM:(h + 1) * HEAD_DIM].astype(BF16),
               lambda h: vn_s[:, h * HEAD_DIM:(h + 1) * HEAD_DIM].astype(BF16),
               jnp.concatenate(bias_new, axis=0))
        lam = _lambda(lamp_ref, lam_init)
        on = acc_s[...] / l_s[...]
        for h in range(C_HEADS):
            o = on[h * rows:h * rows + ts] - lam * on[h * rows + ts:(h + 1) * rows]
            o_ref[:, h * HEAD_DIM:(h + 1) * HEAD_DIM] = (_rms(o, nw_ref[...]) * (1.0 - lam_init)).astype(o_ref.dtype)


def _outproj_kernel(ob_ref, oc_ref, oa_ref, wb_ref, wc_ref, wa_ref, x_ref, nw_ref, xo_ref, ho_ref):
    acc = _dot(ob_ref[...], wb_ref[...]) + _dot(oc_ref[...], wc_ref[...]) + _dot(oa_ref[...], wa_ref[...])
    x = x_ref[...] + acc
    xo_ref[...] = x
    ho_ref[...] = _rms(x, nw_ref[...]).astype(ho_ref.dtype)


def _outproj(ob, oc, oa, w, x, nw, l, tm):
    m, d = x.shape
    return pl.pallas_call(
        _outproj_kernel, grid=(m // tm,),
        in_specs=[pl.BlockSpec((tm, B_W), lambda i: (i, 0)),
                  pl.BlockSpec((tm, C_W), lambda i: (i, 0)),
                  pl.BlockSpec((tm, A_W), lambda i: (i, 0)),
                  pl.BlockSpec((None, B_W, d), lambda i: (l, 0, 0)),
                  pl.BlockSpec((None, C_W, d), lambda i: (l, 1, 0)),
                  pl.BlockSpec((None, A_W, d), lambda i: (l, (B_W + C_W) // A_W, 0)),
                  pl.BlockSpec((tm, d), lambda i: (i, 0)),
                  pl.BlockSpec((None, 1, d), lambda i: (l, 0, 0))],
        out_specs=[pl.BlockSpec((tm, d), lambda i: (i, 0)), pl.BlockSpec((tm, d), lambda i: (i, 0))],
        out_shape=[jax.ShapeDtypeStruct((m, d), F32), jax.ShapeDtypeStruct((m, d), BF16)],
        compiler_params=_cparams("parallel"), name="outproj")(ob, oc, oa, w, w, w, x, nw)


def _ffn_up_kernel(h_ref, wg_ref, wu_ref, cw_ref, cb_ref, st_ref, o_ref, so_ref, buf_s, wg_s, wu_s,
                   *, tm, stride, head, blocks):
    i = pl.program_id(1)

    @pl.when(i == 0)
    def _():
        wg_s[...] = wg_ref[...].astype(BF16)
        wu_s[...] = wu_ref[...].astype(BF16)

    @pl.when(i % blocks == 0)
    def _():
        buf_s[0:head, :] = st_ref[...]

    hn = h_ref[...]
    g = _dot(hn, wg_s[...])
    u = _dot(hn, wu_s[...])
    buf_s[head:head + tm, :] = g
    y = cw_ref[FFN_CONV - 1:FFN_CONV, :] * g + cb_ref[...]
    for tap in range(FFN_CONV - 1):
        off = head - (FFN_CONV - 1 - tap) * stride
        y = y + cw_ref[tap:tap + 1, :] * buf_s[off:off + tm, :]
    o_ref[...] = (_gelu(y) * u).astype(o_ref.dtype)
    last = buf_s[tm:tm + head, :]
    so_ref[...] = last
    buf_s[0:head, :] = last


def _ffn_up(hn, wg, wu, cw, cb, state, l, *, tm, tn, stride, blocks):
    m, d = hn.shape
    dff = wg.shape[-1]
    head = state.shape[1]
    nseq = state.shape[0]
    return pl.pallas_call(
        functools.partial(_ffn_up_kernel, tm=tm, stride=stride, head=head, blocks=blocks),
        grid=(dff // tn, m // tm),
        in_specs=[pl.BlockSpec((tm, d), lambda j, i: (i, 0)),
                  pl.BlockSpec((None, d, tn), lambda j, i: (l, 0, j)),
                  pl.BlockSpec((None, d, tn), lambda j, i: (l, 0, j)),
                  pl.BlockSpec((None, FFN_CONV, tn), lambda j, i: (l, 0, j)),
                  pl.BlockSpec((None, 1, tn), lambda j, i: (l, 0, j)),
                  pl.BlockSpec((None, head, tn), lambda j, i: (i // blocks, 0, j))],
        out_specs=[pl.BlockSpec((tm, tn), lambda j, i: (i, j)),
                   pl.BlockSpec((None, head, tn), lambda j, i: (i // blocks, 0, j))],
        out_shape=[jax.ShapeDtypeStruct((m, dff), BF16), jax.ShapeDtypeStruct((nseq, head, dff), F32)],
        scratch_shapes=[pltpu.VMEM((tm + head, tn), F32), pltpu.VMEM((d, tn), BF16), pltpu.VMEM((d, tn), BF16)],
        compiler_params=_cparams("parallel", "arbitrary"), name="ffn_up")(hn, wg, wu, cw, cb, state)


def _ffn_down_kernel(h_ref, w_ref, x_ref, o_ref):
    o_ref[...] = x_ref[...] + _dot(h_ref[...], w_ref[...])


def _ffn_down(hid, w, x, l, tm, tn):
    m, dff = hid.shape
    d = x.shape[1]
    return pl.pallas_call(
        _ffn_down_kernel, grid=(m // tm, d // tn),
        in_specs=[pl.BlockSpec((tm, dff), lambda i, j: (i, 0)),
                  pl.BlockSpec((None, dff, tn), lambda i, j: (l, 0, j)),
                  pl.BlockSpec((tm, tn), lambda i, j: (i, j))],
        out_specs=pl.BlockSpec((tm, tn), lambda i, j: (i, j)),
        out_shape=jax.ShapeDtypeStruct((m, d), F32),
        compiler_params=_cparams("parallel", "parallel"), name="ffn_down")(hid, w, x)


def _pick(total, want):
    t = min(total, want)
    while total % t:
        t //= 2
    return t


def kernel(x_prompt, x_sample, cache_k, cache_v, state_conv_b, state_delta, state_ffn_conv, page_table, attn_norm_w, w_in, a_ln_w, a_ln_b, a_ws, a_bs, b_conv_w, b_a_log, b_dt_bias, b_norm_w, c_lam_q1, c_lam_k1, c_lam_q2, c_lam_k2, c_norm_w, rel_bias, w_out, ffn_norm_w, w_gate, w_up, ffn_conv_w, ffn_conv_b, w_down, final_norm_w):
    bp, tp, d = x_prompt.shape
    bs, ts, _ = x_sample.shape
    depth = w_in.shape[0]
    dff = w_gate.shape[-1]
    n_pool, page = cache_k.shape[1], cache_k.shape[2]
    mp, ms = bp * tp, bs * ts

    o_au, o_av, o_qkv, o_z, o_beta = 0, A_W, 2 * A_W, 2 * A_W + 3 * B_W, 2 * A_W + 4 * B_W
    o_cq = o_beta + 2 * B_HEADS
    o_ck, o_cv = o_cq + C_W, o_cq + 2 * C_W
    w_main = jnp.concatenate([
        w_in[:, :, o_qkv:o_z], w_in[:, :, o_z:o_beta], w_in[:, :, o_cq:o_ck], w_in[:, :, o_beta:o_cq],
        jnp.zeros((depth, d, COL_AU - COL_BA - 2 * B_HEADS), w_in.dtype),
        w_in[:, :, o_au:o_qkv], w_in[:, :, o_ck:o_cv + C_W]], axis=-1).astype(BF16)
    w_out_r = jnp.concatenate([w_out[:, A_W:], w_out[:, :A_W]], axis=1).astype(BF16)
    wg, wu = w_gate, w_up
    wd = w_down.astype(BF16)
    attn_nw = attn_norm_w.reshape(depth, 1, d)
    ffn_nw = ffn_norm_w.reshape(depth, 1, d)
    lnw = a_ln_w.reshape(depth, 1, A_W)
    lnb = a_ln_b.reshape(depth, 1, A_W)
    bcol_p = jnp.swapaxes(a_bs, 1, 2)
    tril = jnp.tril(jnp.ones((ts, ts), F32))
    assert ms <= A_CHUNK
    wmix_s = jnp.einsum('lhts,bc->lhtbsc', a_ws[:, :, :ts, :ts] * tril, jnp.eye(bs, dtype=F32)).reshape(
        depth, A_HEADS, ms, ms)
    wmix_s = jnp.pad(wmix_s, ((0, 0), (0, 0), (0, A_CHUNK - ms), (0, A_CHUNK - ms)))
    bcol_s = jnp.repeat(jnp.swapaxes(a_bs[:, :, :ts], 1, 2), bs, axis=1)
    pad_lanes = jnp.zeros((depth, 1, HEAD_DIM - 2 * B_HEADS), F32)
    alog = jnp.concatenate([jnp.zeros((depth, 1, B_HEADS), F32), b_a_log.reshape(depth, 1, B_HEADS), pad_lanes], -1)
    dtb = jnp.concatenate([jnp.zeros((depth, 1, B_HEADS), F32), b_dt_bias.reshape(depth, 1, B_HEADS), pad_lanes], -1)
    b_nw = b_norm_w.reshape(depth, 1, HEAD_DIM)
    c_nw = c_norm_w.reshape(depth, 1, HEAD_DIM)
    lamp = jnp.stack([c_lam_q1, c_lam_k1, c_lam_q2, c_lam_k2], axis=1)
    tbl = rel_bias.reshape(-1)
    ffn_cb = ffn_conv_b.reshape(depth, 1, dff)
    ck_flat = jnp.swapaxes(cache_k, 2, 3).reshape(depth * n_pool, C_HEADS, page, HEAD_DIM)
    cv_flat = jnp.swapaxes(cache_v, 2, 3).reshape(depth * n_pool, C_HEADS, page, HEAD_DIM)
    pt_flat = page_table.reshape(-1)

    conv_head = SUBLANES
    zero_conv_p = jnp.zeros((bp, conv_head, 3 * B_W), F32)
    zero_delta_p = jnp.zeros((bp, B_HEADS, HEAD_DIM, HEAD_DIM), F32)
    zero_ffn_p = jnp.zeros((bp, SUBLANES, dff), F32)
    ffn_head_s = max(SUBLANES, (FFN_CONV - 1) * bs)

    tm_in = _pick(mp, TM_IN)
    tm_out = _pick(mp, TM_OUT)
    tm_up = _pick(tp, TM_UP)
    tm_down = _pick(mp, TM_DOWN)
    tb_delta = _pick(tp, TB_DELTA)
    tq = _pick(tp, TQ)
    ts_pad_delta = DN_PAIR

    xp = x_prompt.reshape(mp, d)
    xs = jnp.swapaxes(x_sample, 0, 1).reshape(ms, d)

    def to_bm(a):
        return jnp.swapaxes(a.reshape(ts, bs, -1), 0, 1)

    def to_tm(a):
        return jnp.swapaxes(a, 0, 1).reshape(ms, -1)

    kv_stacks = None
    p_conv, p_delta, p_ffn = [], [], []
    s_k, s_v, s_conv, s_delta, s_ffn, s_av = [], [], [], [], [], []
    for l in range(depth):
        lam_init = 0.8 - 0.6 * math.exp(-0.3 * l)

        main = _inproj(xp, attn_nw, w_main, l, tm_in, TN)
        main_s = _inproj(xs, attn_nw, w_main, l, ms, TN)
        main_bm = to_bm(main_s)
        q_bm = main_bm[:, :, COL_CQ:COL_CQ + C_W].reshape(ms, C_W)
        kn_bm = main_bm[:, :, COL_CK:COL_CK + C_W]
        vn_bm = main_bm[:, :, COL_CV:COL_CV + C_W]
        out_c, k_stack, v_stack, out_c_s = _attention(
            main, q_bm, kn_bm.reshape(ms, C_W), vn_bm.reshape(ms, C_W), ck_flat, cv_flat, pt_flat, tbl, lamp, c_nw,
            l, kv_stacks, depth=depth, nb=bp, t_total=tp, tq=tq, nb_s=bs, ts=ts, n_pool=n_pool, lam_init=lam_init)
        kv_stacks = (k_stack, v_stack)

        (out_a,) = _gmlp(main, lnw, lnb, a_ws, bcol_p, l, rows=4 * A_CHUNK, chunk=A_CHUNK, causal_mask=True,
                         want_v=False)
        out_b, delta_new = _delta(main, zero_conv_p, b_conv_w, alog, dtb, b_nw, zero_delta_p, l,
                                  nb=bp, t_total=tp, t_valid=tp, tb=tb_delta)
        xp, hn2 = _outproj(out_b, out_c, out_a, w_out_r, xp, ffn_nw, l, tm_out)
        hid, ffn_last = _ffn_up(hn2, wg, wu, ffn_conv_w, ffn_cb, zero_ffn_p, l, tm=tm_up, tn=512, stride=1,
                                blocks=tp // tm_up)
        xp = _ffn_down(hid, wd, xp, l, tm_down, TN)
        main3 = main.reshape(bp, tp, N_MAIN)
        p_conv.append(main3[:, tp - (B_CONV - 1):, COL_BQ:COL_BQ + 3 * B_W])
        p_delta.append(delta_new)
        p_ffn.append(ffn_last[:, SUBLANES - (FFN_CONV - 1):])

        out_a_s, va_s = _gmlp(main_s, lnw, lnb, wmix_s, bcol_s, l, rows=ms, chunk=A_CHUNK, causal_mask=False,
                              want_v=True)
        main_pad = jnp.pad(main_bm, ((0, 0), (0, ts_pad_delta - ts), (0, 0))).reshape(bs * ts_pad_delta, N_MAIN)
        conv_s = jnp.pad(state_conv_b[l], ((0, 0), (conv_head - (B_CONV - 1), 0), (0, 0)))
        out_b_s, delta_new_s = _delta(main_pad, conv_s, b_conv_w, alog, dtb, b_nw, state_delta[l], l,
                                      nb=bs, t_total=ts_pad_delta, t_valid=ts, tb=ts_pad_delta)
        out_b_s = to_tm(out_b_s.reshape(bs, ts_pad_delta, B_W)[:, :ts])
        out_c_s = to_tm(out_c_s.reshape(bs, ts, C_W)).astype(BF16)
        xs, hn2_s = _outproj(out_b_s, out_c_s, out_a_s, w_out_r, xs, ffn_nw, l, ms)
        ffn_state_s = jnp.swapaxes(state_ffn_conv[l], 0, 1).reshape(1, (FFN_CONV - 1) * bs, dff)
        ffn_state_s = jnp.pad(ffn_state_s, ((0, 0), (ffn_head_s - (FFN_CONV - 1) * bs, 0), (0, 0)))
        hid_s, ffn_last_s = _ffn_up(hn2_s, wg, wu, ffn_conv_w, ffn_cb, ffn_state_s, l, tm=ms, tn=512, stride=bs,
                                    blocks=1)
        xs = _ffn_down(hid_s, wd, xs, l, ms, TN)
        s_k.append(kn_bm.reshape(bs, ts, C_HEADS, HEAD_DIM))
        s_v.append(vn_bm.reshape(bs, ts, C_HEADS, HEAD_DIM))
        s_conv.append(main_bm[:, ts - (B_CONV - 1):, COL_BQ:COL_BQ + 3 * B_W])
        s_delta.append(delta_new_s)
        s_ffn.append(jnp.swapaxes(ffn_last_s[0, ffn_head_s - (FFN_CONV - 1) * bs:].reshape(FFN_CONV - 1, bs, dff),
                                  0, 1))
        s_av.append(to_bm(va_s).reshape(bs, ts, A_HEADS, HEAD_DIM))

    y_prompt = _rmsnorm(xp, final_norm_w, _pick(mp, 512)).reshape(bp, tp, d)
    y_sample = jnp.swapaxes(_rmsnorm(xs, final_norm_w, ms).reshape(ts, bs, d), 0, 1)
    k_prompt, v_prompt = (jnp.swapaxes(a, 2, 3) for a in kv_stacks)
    return (y_prompt, y_sample,
            k_prompt, v_prompt, jnp.stack(p_conv), jnp.stack(p_delta), jnp.stack(p_ffn),
            jnp.stack(s_k), jnp.stack(s_v), jnp.stack(s_conv), jnp.stack(s_delta), jnp.stack(s_ffn),
            jnp.stack(s_av))
```

```python
import functools
import math

import numpy as np
import jax
import jax.numpy as jnp
from jax import lax
from jax.experimental import pallas as pl
from jax.experimental.pallas import tpu as pltpu

F32 = jnp.float32
BF16 = jnp.bfloat16
HI = lax.Precision.HIGHEST

HEAD_DIM = 128
SUBLANES = 8
A_HEADS, B_HEADS, C_HEADS = 4, 6, 6
A_W, B_W, C_W = A_HEADS * HEAD_DIM, B_HEADS * HEAD_DIM, C_HEADS * HEAD_DIM
A_CHUNK = 128
B_CONV = 4
DN_CHUNK = 64
DN_PAIR = 2 * DN_CHUNK
C_HALF = HEAD_DIM // 2
N_BUCKETS = 32
MAX_DISTANCE = 128
FFN_CONV = 3
EPS = 1e-6
NEG_INF = -1e30
VMEM_LIMIT = 56 * 1024 * 1024
TM_IN, TM_OUT, TM_UP, TM_DOWN, TB_DELTA, TQ = 1024, 512, 1024, 1024, 256, 512
TN = 512
PAGES_PER_STEP = 16
LOG2E = math.log2(math.e)

COL_BQ, COL_BK, COL_BV, COL_BZ, COL_CQ = 0, B_W, 2 * B_W, 3 * B_W, 4 * B_W
COL_BA = COL_CQ + C_W
COL_AU = COL_BA + 2 * HEAD_DIM
COL_AV = COL_AU + A_W
COL_CK = COL_AV + A_W
COL_CV = COL_CK + C_W
N_MAIN = COL_CV + C_W


def _t5_bucket_starts():
    n = np.arange(4 * MAX_DISTANCE)
    max_exact = N_BUCKETS // 2
    nf = np.maximum(n, max_exact).astype(np.float64)
    large = max_exact + (np.log(nf / max_exact) / math.log(MAX_DISTANCE / max_exact)
                         * (N_BUCKETS - max_exact)).astype(np.int64)
    bucket = np.where(n < max_exact, n, np.minimum(large, N_BUCKETS - 1))
    return [int(n[bucket == b].min()) for b in range(N_BUCKETS)]


BUCKET_START = _t5_bucket_starts()
FAR_DISTANCE = BUCKET_START[-1]


def _cparams(*sem):
    return pltpu.CompilerParams(dimension_semantics=sem, vmem_limit_bytes=VMEM_LIMIT)


def _gelu(x):
    return 0.5 * x * (1.0 + jnp.tanh(math.sqrt(2.0 / math.pi) * (x + 0.044715 * (x * x * x))))


def _sigmoid(x):
    return 1.0 / (1.0 + jnp.exp(-x))


def _silu(x):
    return x * _sigmoid(x)


def _softplus(x):
    return jnp.maximum(x, 0.0) + jnp.log(1.0 + jnp.exp(-jnp.abs(x)))


def _rms(x, w):
    return x * lax.rsqrt(jnp.mean(x * x, axis=-1, keepdims=True) + EPS) * w


def _dot(a, b, **kw):
    return jnp.dot(a, b, preferred_element_type=F32, **kw)


def _dot_nt(a, b, **kw):
    return lax.dot_general(a, b, (((1,), (1,)), ((), ())), preferred_element_type=F32, **kw)


def _split2(x):
    hi = x.astype(BF16)
    return hi, (x - hi.astype(F32)).astype(BF16)


def _split3(x):
    hi = x.astype(BF16)
    r = x - hi.astype(F32)
    mid = r.astype(BF16)
    return hi, mid, (r - mid.astype(F32)).astype(BF16)


def _dot3(a_hl, b_hl):
    (ah, al), (bh, bl) = a_hl, b_hl
    return _dot(jnp.concatenate([ah, ah, al], axis=1), jnp.concatenate([bh, bl, bh], axis=0))


def _rmsnorm_kernel(x_ref, w_ref, o_ref):
    o_ref[...] = _rms(x_ref[...], w_ref[...]).astype(o_ref.dtype)


def _rmsnorm(x, w, tm):
    m, d = x.shape
    return pl.pallas_call(
        _rmsnorm_kernel, grid=(m // tm,),
        in_specs=[pl.BlockSpec((tm, d), lambda i: (i, 0)), pl.BlockSpec((1, d), lambda i: (0, 0))],
        out_specs=pl.BlockSpec((tm, d), lambda i: (i, 0)),
        out_shape=jax.ShapeDtypeStruct((m, d), F32),
        compiler_params=_cparams("parallel"), name="final_rmsnorm")(x, w.reshape(1, d))


def _inproj_kernel(x_ref, xs_ref, nw_ref, w_ref, o_ref, os_ref, hn_s, hns_s):
    i, j = pl.program_id(0), pl.program_id(1)

    @pl.when(j == 0)
    def _():
        hn_s[...] = _rms(x_ref[...], nw_ref[...]).astype(BF16)

    o_ref[...] = _dot(hn_s[...], w_ref[...])

    @pl.when(i == pl.num_programs(0) - 1)
    def _():
        @pl.when(j == 0)
        def _():
            hns_s[...] = _rms(xs_ref[...], nw_ref[...]).astype(BF16)

        os_ref[...] = _dot(hns_s[...], w_ref[...])


def _inproj(x, xs, nw, w, l, tm, tn):
    m, d = x.shape
    ms = xs.shape[0]
    n = w.shape[-1]
    ni = m // tm
    return pl.pallas_call(
        _inproj_kernel, grid=(ni, n // tn),
        in_specs=[pl.BlockSpec((tm, d), lambda i, j: (i, 0)),
                  pl.BlockSpec((ms, d), lambda i, j: (0, 0)),
                  pl.BlockSpec((None, 1, d), lambda i, j: (l, 0, 0)),
                  pl.BlockSpec((None, d, tn), lambda i, j: (l, 0, j))],
        out_specs=[pl.BlockSpec((tm, tn), lambda i, j: (i, j)),
                   pl.BlockSpec((ms, tn), lambda i, j: (0, jnp.where(i == ni - 1, j, 0)))],
        out_shape=[jax.ShapeDtypeStruct((m, n), F32), jax.ShapeDtypeStruct((ms, n), F32)],
        scratch_shapes=[pltpu.VMEM((tm, d), BF16), pltpu.VMEM((ms, d), BF16)],
        compiler_params=_cparams("arbitrary", "arbitrary"), name="inproj")(x, xs, nw, w)


def _gmlp_kernel(u_ref, v_ref, lnw_ref, lnb_ref, w_ref, bcol_ref, *out_refs, chunk, rows, causal_mask):
    o_ref = out_refs[0]
    nchunk = max(rows // chunk, 1)
    live = min(rows, chunk)
    if causal_mask:
        ii = lax.broadcasted_iota(jnp.int32, (chunk, chunk), 0)
        jj = lax.broadcasted_iota(jnp.int32, (chunk, chunk), 1)
        keep = ii >= jj
    for h in range(A_HEADS):
        sl = slice(h * HEAD_DIM, (h + 1) * HEAD_DIM)
        u = _gelu(u_ref[:, sl])
        v = _gelu(v_ref[:, sl])
        mu = jnp.mean(v, axis=-1, keepdims=True)
        vc = v - mu
        vn = vc * lax.rsqrt(jnp.mean(vc * vc, axis=-1, keepdims=True) + EPS) * lnw_ref[:, sl] + lnb_ref[:, sl]
        if len(out_refs) > 1:
            out_refs[1][:, sl] = vn
        w = w_ref[h]
        if causal_mask:
            w = jnp.where(keep, w, 0.0)
        wb = w.astype(BF16)
        vb = vn.astype(BF16)
        bcol = bcol_ref[0:live, h:h + 1]
        if live < chunk:
            vb = jnp.concatenate([vb, jnp.zeros((chunk - live, HEAD_DIM), BF16)], axis=0)
        for c in range(nchunk):
            rs = slice(c * live, (c + 1) * live)
            mixed = _dot(wb, vb[c * chunk:(c + 1) * chunk])[0:live] + bcol
            o_ref[rs, sl] = (u[rs] * mixed).astype(o_ref.dtype)


def _gmlp(main, lnw, lnb, wmix, bcol, l, *, rows, chunk, causal_mask, want_v):
    m = main.shape[0]
    live = min(rows, chunk)
    cu, cv = COL_AU // A_W, COL_AV // A_W
    out_shape = [jax.ShapeDtypeStruct((m, A_W), BF16)]
    out_specs = [pl.BlockSpec((rows, A_W), lambda i: (i, 0))]
    if want_v:
        out_shape.append(jax.ShapeDtypeStruct((m, A_W), F32))
        out_specs.append(pl.BlockSpec((rows, A_W), lambda i: (i, 0)))
    return pl.pallas_call(
        functools.partial(_gmlp_kernel, chunk=chunk, rows=rows, causal_mask=causal_mask),
        grid=(m // rows,),
        in_specs=[pl.BlockSpec((rows, A_W), lambda i: (i, cu)),
                  pl.BlockSpec((rows, A_W), lambda i: (i, cv)),
                  pl.BlockSpec((None, 1, A_W), lambda i: (l, 0, 0)),
                  pl.BlockSpec((None, 1, A_W), lambda i: (l, 0, 0)),
                  pl.BlockSpec((None, A_HEADS, chunk, chunk), lambda i: (l, 0, 0, 0)),
                  pl.BlockSpec((None, live, A_HEADS), lambda i: (l, 0, 0))],
        out_specs=out_specs, out_shape=out_shape,
        compiler_params=_cparams("parallel"), name="gmlp")(main, main, lnw, lnb, wmix, bcol)


def _delta_kernel(q_ref, k_ref, v_ref, z_ref, ba_ref, cs_ref, cw_ref, alog_ref, dtb_ref, nw_ref, s0_ref,
                  o_ref, s_ref, buf_s, qs_s, ks_s, vs_s, g_s, bt_s, gc_s, u_s, w_s, ai_s, qd_s, kd_s,
                  *, tb, t_valid, t_total):
    t = pl.program_id(1)
    head = SUBLANES

    @pl.when(t == 0)
    def _():
        s_ref[...] = s0_ref[...]
        for j in range(3):
            buf_s[j, 0:head, :] = cs_ref[:, j * B_W:(j + 1) * B_W]

    if t_valid < t_total:
        row = t * tb + lax.broadcasted_iota(jnp.int32, (tb, 1), 0)
        live = row < t_valid
    else:
        live = None

    def keep(x):
        return x if live is None else jnp.where(live, x, 0.0)

    for j, (src, dst) in enumerate(((q_ref, qs_s), (k_ref, ks_s), (v_ref, vs_s))):
        x = src[...]
        buf_s[j, head:head + tb, :] = x
        w = cw_ref[:, j * B_W:(j + 1) * B_W]
        y = w[B_CONV - 1:B_CONV] * x
        for tap in range(B_CONV - 1):
            off = head - (B_CONV - 1) + tap
            y = y + w[tap:tap + 1] * buf_s[j, off:off + tb, :]
        buf_s[j, 0:head, :] = buf_s[j, tb:tb + head, :]
        y = keep(_silu(y))
        for h in range(B_HEADS):
            sl = slice(h * HEAD_DIM, (h + 1) * HEAD_DIM)
            seg = y[:, sl]
            if j == 0:
                seg = seg * lax.rsqrt(jnp.sum(seg * seg, axis=-1, keepdims=True) + EPS) * (HEAD_DIM ** -0.5)
            elif j == 1:
                seg = seg * lax.rsqrt(jnp.sum(seg * seg, axis=-1, keepdims=True) + EPS)
            dst[h] = seg

    ba = ba_ref[...]
    bt_s[...] = keep(_sigmoid(ba))
    g_s[...] = keep(-jnp.exp(alog_ref[...]) * _softplus(ba + dtb_ref[...]))

    n = DN_PAIR
    ii = lax.broadcasted_iota(jnp.int32, (n, n), 0)
    jj = lax.broadcasted_iota(jnp.int32, (n, n), 1)
    same = (ii >= DN_CHUNK) == (jj >= DN_CHUNK)
    incl = same & (ii >= jj)
    strict = same & (ii > jj)
    tri = jnp.where(incl, 1.0, 0.0)
    eye = jnp.where(ii == jj, 1.0, 0.0)
    rowi = lax.broadcasted_iota(jnp.int32, (n, 1), 0)
    lane = lax.broadcasted_iota(jnp.int32, (n, n), 1)
    zeros_half = jnp.zeros((DN_CHUNK, HEAD_DIM), F32)

    tri3 = jnp.concatenate([tri.astype(BF16)] * 3, axis=1)

    heads = range(B_HEADS)
    gcol_of = lambda gc, h: gc[:, B_HEADS + h:B_HEADS + h + 1]

    def intra(p, carry):
        r0 = pl.multiple_of(p * n, n)
        rows = pl.ds(r0, n)
        gc = _dot(tri3, jnp.concatenate(_split3(g_s[rows, :]), axis=0))
        gct = gc.T
        bt = bt_s[rows, :]
        gc_s[rows, :] = gc
        q = [qs_s[h, rows, :] for h in heads]
        k = [ks_s[h, rows, :] for h in heads]
        decay = [jnp.exp(jnp.where(incl, gcol_of(gc, h) - gct[B_HEADS + h:B_HEADS + h + 1, :], NEG_INF))
                 for h in heads]
        kb = [k[h] * bt[:, h:h + 1] for h in heads]
        kq = [_dot_nt(jnp.concatenate([kb[h].astype(BF16), q[h].astype(BF16)], axis=0), k[h].astype(BF16))
              for h in heads]
        a = [-jnp.where(strict, kq[h][0:n] * decay[h], 0.0) for h in heads]
        pm = [eye + a[h] for h in heads]
        a_hl = [_split2(a[h]) for h in heads]
        a = [_dot3(a_hl[h], a_hl[h]) for h in heads]
        for _ in range(int(math.log2(DN_CHUNK)) - 2):
            a_hl = [_split2(a[h]) for h in heads]
            pm_hl = [_split2(pm[h]) for h in heads]
            prod = [_dot3(tuple(jnp.concatenate([x, y], axis=0) for x, y in zip(a_hl[h], pm_hl[h])), a_hl[h])
                    for h in heads]
            a = [prod[h][0:n] for h in heads]
            pm = [pm[h] + prod[h][n:2 * n] for h in heads]
        prod = [_dot3(_split2(pm[h]), _split2(a[h])) for h in heads]
        pm = [pm[h] + prod[h] for h in heads]
        egc = [jnp.exp(gcol_of(gc, h)) for h in heads]
        sol = [_dot3(_split2(pm[h]),
                     _split2(jnp.concatenate([vs_s[h, rows, :] * bt[:, h:h + 1], kb[h] * egc[h]], axis=1)))
               for h in heads]
        for h in heads:
            u_s[h, rows, :] = sol[h][:, 0:HEAD_DIM]
            w_s[h, rows, :] = sol[h][:, HEAD_DIM:2 * HEAD_DIM].astype(BF16)
            ai_s[h, rows, :] = (kq[h][n:2 * n] * decay[h]).astype(BF16)
            qd_s[h, rows, :] = (q[h] * egc[h]).astype(BF16)
            gcol = gcol_of(gc, h)
            glast = jnp.where(rowi < DN_CHUNK, gcol[DN_CHUNK - 1:DN_CHUNK], gcol[n - 1:n])
            kdt = (k[h] * jnp.exp(glast - gcol)).T
            kd_s[h, p, 0] = jnp.where(lane < DN_CHUNK, kdt, 0.0).astype(BF16)
            kd_s[h, p, 1] = jnp.where(lane >= DN_CHUNK, kdt, 0.0).astype(BF16)
        return carry

    def inter(p, carry):
        r0 = pl.multiple_of(p * n, n)
        s = [s_ref[h] for h in heads]
        outs = [[] for _ in heads]
        for c in range(2):
            rs = pl.ds(pl.multiple_of(r0 + c * DN_CHUNK, DN_CHUNK), DN_CHUNK)
            ws = [_dot(jnp.concatenate([w_s[h, rs, :], qd_s[h, rs, :]], axis=0), s[h].astype(BF16)) for h in heads]
            v_new = [u_s[h, rs, :] - ws[h][0:DN_CHUNK] for h in heads]
            v_pad = [jnp.concatenate([v_new[h], zeros_half] if c == 0 else [zeros_half, v_new[h]],
                                     axis=0).astype(BF16) for h in heads]
            av = [_dot(jnp.concatenate([ai_s[h, rs, :], kd_s[h, p, c]], axis=0), v_pad[h]) for h in heads]
            last = gc_s[pl.ds(pl.multiple_of(r0 + (c + 1) * DN_CHUNK - SUBLANES, SUBLANES), SUBLANES), :]
            for h in heads:
                outs[h].append(ws[h][DN_CHUNK:2 * DN_CHUNK] + av[h][0:DN_CHUNK])
                gl = jnp.exp(last[SUBLANES - 1:SUBLANES, B_HEADS + h:B_HEADS + h + 1])
                s[h] = s[h] * gl + av[h][DN_CHUNK:DN_CHUNK + HEAD_DIM]
        for h in heads:
            sl = slice(h * HEAD_DIM, (h + 1) * HEAD_DIM)
            s_ref[h] = s[h]
            o = jnp.concatenate(outs[h], axis=0)
            z = z_ref[pl.ds(r0, n), sl]
            o_ref[pl.ds(r0, n), sl] = (_rms(o, nw_ref[...]) * _silu(z)).astype(o_ref.dtype)
        return carry

    lax.fori_loop(0, tb // n, intra, 0)
    lax.fori_loop(0, tb // n, inter, 0)


def _delta(main, conv_state, conv_w, alog, dtb, nw, s0, l, *, nb, t_total, t_valid, tb):
    m = main.shape[0]
    nt = t_total // tb
    cq, ck, cv, cz, cba = COL_BQ // B_W, COL_BK // B_W, COL_BV // B_W, COL_BZ // B_W, COL_BA // HEAD_DIM
    row = lambda b, t: b * nt + t
    return pl.pallas_call(
        functools.partial(_delta_kernel, tb=tb, t_valid=t_valid, t_total=t_total),
        grid=(nb, nt),
        in_specs=[pl.BlockSpec((tb, B_W), lambda b, t: (row(b, t), cq)),
                  pl.BlockSpec((tb, B_W), lambda b, t: (row(b, t), ck)),
                  pl.BlockSpec((tb, B_W), lambda b, t: (row(b, t), cv)),
                  pl.BlockSpec((tb, B_W), lambda b, t: (row(b, t), cz)),
                  pl.BlockSpec((tb, HEAD_DIM), lambda b, t: (row(b, t), cba)),
                  pl.BlockSpec((None, SUBLANES, 3 * B_W), lambda b, t: (b, 0, 0)),
                  pl.BlockSpec((None, B_CONV, 3 * B_W), lambda b, t: (l, 0, 0)),
                  pl.BlockSpec((None, 1, HEAD_DIM), lambda b, t: (l, 0, 0)),
                  pl.BlockSpec((None, 1, HEAD_DIM), lambda b, t: (l, 0, 0)),
                  pl.BlockSpec((None, 1, HEAD_DIM), lambda b, t: (l, 0, 0)),
                  pl.BlockSpec((None, B_HEADS, HEAD_DIM, HEAD_DIM), lambda b, t: (b, 0, 0, 0))],
        out_specs=[pl.BlockSpec((tb, B_W), lambda b, t: (row(b, t), 0)),
                   pl.BlockSpec((None, B_HEADS, HEAD_DIM, HEAD_DIM), lambda b, t: (b, 0, 0, 0))],
        out_shape=[jax.ShapeDtypeStruct((m, B_W), BF16),
                   jax.ShapeDtypeStruct((nb, B_HEADS, HEAD_DIM, HEAD_DIM), F32)],
        scratch_shapes=[pltpu.VMEM((3, tb + SUBLANES, B_W), F32),
                        pltpu.VMEM((B_HEADS, tb, HEAD_DIM), F32),
                        pltpu.VMEM((B_HEADS, tb, HEAD_DIM), F32),
                        pltpu.VMEM((B_HEADS, tb, HEAD_DIM), F32),
                        pltpu.VMEM((tb, HEAD_DIM), F32),
                        pltpu.VMEM((tb, HEAD_DIM), F32),
                        pltpu.VMEM((tb, HEAD_DIM), F32),
                        pltpu.VMEM((B_HEADS, tb, HEAD_DIM), F32),
                        pltpu.VMEM((B_HEADS, tb, HEAD_DIM), BF16),
                        pltpu.VMEM((B_HEADS, tb, HEAD_DIM), BF16),
                        pltpu.VMEM((B_HEADS, tb, HEAD_DIM), BF16),
                        pltpu.VMEM((B_HEADS, tb // DN_PAIR, 2, HEAD_DIM, DN_PAIR), BF16)],
        compiler_params=_cparams("parallel", "arbitrary"), name="deltanet")(
            main, main, main, main, main, conv_state, conv_w, alog, dtb, nw, s0)


def _bias_from_rel(rel, tbl_ref, h):
    bias = jnp.full(rel.shape, tbl_ref[h], F32)
    for b in range(1, N_BUCKETS):
        bias = jnp.where(rel >= BUCKET_START[b], tbl_ref[b * C_HEADS + h], bias)
    return bias


def _lambda(lamp_ref, lam_init):
    lp = lamp_ref[...]
    s1 = jnp.sum(lp[0:1] * lp[1:2], axis=-1, keepdims=True)
    s2 = jnp.sum(lp[2:3] * lp[3:4], axis=-1, keepdims=True)
    return jnp.exp(s1) - jnp.exp(s2) + lam_init


def _split_halves(qh):
    lane = lax.broadcasted_iota(jnp.int32, qh.shape, 1)
    return jnp.where(lane < C_HALF, qh, 0.0), jnp.where(lane >= C_HALF, qh, 0.0)


def _softmax_update(m_ref, l_ref, acc_ref, idx, s, pv):
    chunks = [s[:, c * HEAD_DIM:(c + 1) * HEAD_DIM] for c in range(s.shape[1] // HEAD_DIM)]
    mx = chunks[0]
    for c in chunks[1:]:
        mx = jnp.maximum(mx, c)
    m_old = m_ref[idx]
    m_new = jnp.maximum(m_old, jnp.max(mx, axis=-1, keepdims=True))
    p = [jnp.exp2(c - m_new) for c in chunks]
    ps = p[0]
    for c in p[1:]:
        ps = ps + c
    a = jnp.exp2(m_old - m_new)
    l_ref[idx] = a * l_ref[idx] + jnp.sum(ps, axis=-1, keepdims=True)
    acc_ref[idx] = a * acc_ref[idx] + pv(jnp.concatenate([c.astype(BF16) for c in p], axis=1))
    m_ref[idx] = m_new


def _prompt_attn_step(h, b, qi, tbl_ref, q_ref, k_ref, v_ref, lamp_ref, nw_ref, o_ref, ko_ref, vo_ref,
                      kb_s, vb_s, bias_s, m_s, l_s, acc_s, *, tq, lam_init):
    far_bias = tbl_ref[(N_BUCKETS - 1) * C_HEADS + h]

    @pl.when((qi == 0) & (b == 0))
    def _():
        r = lax.broadcasted_iota(jnp.int32, (tq, tq), 0) - lax.broadcasted_iota(jnp.int32, (tq, tq), 1)
        bias_s[0] = jnp.where(r >= 0, (_bias_from_rel(r, tbl_ref, h) - far_bias) * LOG2E, NEG_INF)
        bias_s[1] = (_bias_from_rel(r + tq, tbl_ref, h) - far_bias) * LOG2E

    @pl.when(qi == 0)
    def _():
        kf = k_ref[...]
        vf = v_ref[...]
        ko_ref[...] = kf
        vo_ref[...] = vf
        kb_s[...] = kf.astype(BF16)
        vb_s[...] = vf.astype(BF16)

    m_s[...] = jnp.full(m_s.shape, NEG_INF, F32)
    l_s[...] = jnp.zeros(l_s.shape, F32)
    acc_s[...] = jnp.zeros(acc_s.shape, F32)
    q1, q2 = _split_halves(q_ref[...] * (C_HALF ** -0.5 * LOG2E))
    q1 = q1.astype(BF16)
    q2 = q2.astype(BF16)

    def blocks(kblocks, biases):
        starts = [pl.multiple_of(kj * tq, tq) for kj in kblocks]
        scores = [[_dot_nt(qq, kb_s[pl.ds(st, tq), :]) for qq in (q1, q2)] for st in starts]
        for st, sc, bias in zip(starts, scores, biases):
            vb = vb_s[pl.ds(st, tq), :]
            for idx, s in enumerate(sc):
                if bias is not None:
                    s = s + bias
                _softmax_update(m_s, l_s, acc_s, idx, s, lambda p: _dot(p, vb))

    n_far = jnp.maximum(qi - 1, 0)

    def far_pair(j, carry):
        blocks([2 * j, 2 * j + 1], [None, None])
        return carry

    lax.fori_loop(0, n_far // 2, far_pair, 0)

    @pl.when(lax.rem(n_far, 2) == 1)
    def _():
        blocks([n_far - 1], [None])

    @pl.when(qi >= 1)
    def _():
        blocks([qi - 1, qi], [bias_s[1], bias_s[0]])

    @pl.when(qi == 0)
    def _():
        blocks([qi], [bias_s[0]])

    lam = _lambda(lamp_ref, lam_init)
    o = acc_s[0] / l_s[0] - lam * (acc_s[1] / l_s[1])
    o_ref[...] = (_rms(o, nw_ref[...]) * (1.0 - lam_init)).astype(o_ref.dtype)


N_PROMPT_IN, N_SAMPLE_IN, N_PROMPT_SCRATCH = 5, 3, 6


def _attn_kernel(pt_ref, tbl_ref, *refs, n_alias, pages, nb, nq, steps_s, chunks_s, tq, ts, page, lam_init):
    del pt_ref
    q_ref, k_ref, v_ref, lamp_ref, nw_ref = refs[:N_PROMPT_IN]
    refs = refs[N_PROMPT_IN + n_alias:]
    qs_ref, kn_ref, vn_ref = refs[:N_SAMPLE_IN]
    k_refs = refs[N_SAMPLE_IN:N_SAMPLE_IN + pages]
    v_refs = refs[N_SAMPLE_IN + pages:N_SAMPLE_IN + 2 * pages]
    o_ref, ko_ref, vo_ref, os_ref = refs[N_SAMPLE_IN + 2 * pages:N_SAMPLE_IN + 2 * pages + 4]
    scratch = refs[N_SAMPLE_IN + 2 * pages + 4:]
    h, b, qi = pl.program_id(0), pl.program_id(1), pl.program_id(2)
    _prompt_attn_step(h, b, qi, tbl_ref, q_ref, k_ref, v_ref, lamp_ref, nw_ref, o_ref, ko_ref, vo_ref,
                      *scratch[:N_PROMPT_SCRATCH], tq=tq, lam_init=lam_init)
    step = (h * nb + b) * nq + qi

    @pl.when(step < steps_s)
    def _():
        _sample_attn_step(lax.rem(step, chunks_s), chunks_s, tbl_ref, qs_ref, kn_ref, vn_ref, lamp_ref, nw_ref,
                          k_refs, v_refs, os_ref, *scratch[N_PROMPT_SCRATCH:], ts=ts, page=page, lam_init=lam_init)


def _attention(main, q_bm, kn_bm, vn_bm, cache_k, cache_v, page_table, tbl, lamp, nw, l, kv_stacks, *,
               depth, nb, t_total, tq, nb_s, ts, n_pool, lam_init):
    assert tq > FAR_DISTANCE
    m = main.shape[0]
    nq = t_total // tq
    page = cache_k.shape[2]
    n_pages = page_table.shape[0] // nb_s
    pages = math.gcd(n_pages, PAGES_PER_STEP)
    chunks_s = n_pages // pages
    steps_s = nb_s * chunks_s
    assert pages * page > FAR_DISTANCE + ts
    assert steps_s <= C_HEADS * nb * nq
    cq, ck, cv = COL_CQ // HEAD_DIM, COL_CK // HEAD_DIM, COL_CV // HEAD_DIM

    def s_step(h, b, i):
        return jnp.minimum((h * nb + b) * nq + i, steps_s - 1)

    def kv_page(k):
        def index(h, b, i, pt):
            s = s_step(h, b, i)
            return (l * n_pool + pt[(s // chunks_s) * n_pages + (s % chunks_s) * pages + k], 0, 0, 0)
        return pl.BlockSpec((None, C_HEADS, page, HEAD_DIM), index)

    tok = pl.BlockSpec((ts, C_W), lambda h, b, i, pt: (s_step(h, b, i) // chunks_s, 0))
    kv_shape = jax.ShapeDtypeStruct((depth, nb, C_HEADS, t_total, HEAD_DIM), F32)
    kv_spec = pl.BlockSpec((None, None, None, t_total, HEAD_DIM), lambda h, b, i, pt: (l, b, h, 0, 0))
    in_specs = [pl.BlockSpec(memory_space=pltpu.SMEM),
                pl.BlockSpec((tq, HEAD_DIM), lambda h, b, i, pt: (b * nq + i, cq + h)),
                pl.BlockSpec((t_total, HEAD_DIM), lambda h, b, i, pt: (b, ck + h)),
                pl.BlockSpec((t_total, HEAD_DIM), lambda h, b, i, pt: (b, cv + h)),
                pl.BlockSpec((None, 4, C_HALF), lambda h, b, i, pt: (l, 0, 0)),
                pl.BlockSpec((None, 1, HEAD_DIM), lambda h, b, i, pt: (l, 0, 0))]
    args = [tbl, main, main, main, lamp, nw]
    aliases = {}
    if kv_stacks is not None:
        aliases = {1 + len(args): 1, 2 + len(args): 2}
        in_specs += [pl.BlockSpec(memory_space=pl.ANY)] * 2
        args += list(kv_stacks)
    in_specs += [tok, tok, tok] + [kv_page(k) for k in range(pages)] + [kv_page(k) for k in range(pages)]
    args += [q_bm, kn_bm, vn_bm] + [cache_k] * pages + [cache_v] * pages
    rows_s = C_HEADS * 2 * ts
    grid_spec = pltpu.PrefetchScalarGridSpec(
        num_scalar_prefetch=1, grid=(C_HEADS, nb, nq), in_specs=in_specs,
        out_specs=[pl.BlockSpec((tq, HEAD_DIM), lambda h, b, i, pt: (b * nq + i, h)), kv_spec, kv_spec, tok],
        scratch_shapes=[pltpu.VMEM((t_total, HEAD_DIM), BF16),
                        pltpu.VMEM((t_total, HEAD_DIM), BF16),
                        pltpu.VMEM((2, tq, tq), F32),
                        pltpu.VMEM((2, tq, HEAD_DIM), F32),
                        pltpu.VMEM((2, tq, HEAD_DIM), F32),
                        pltpu.VMEM((2, tq, HEAD_DIM), F32),
                        pltpu.VMEM((rows_s, HEAD_DIM), F32),
                        pltpu.VMEM((rows_s, HEAD_DIM), F32),
                        pltpu.VMEM((rows_s, HEAD_DIM), F32),
                        pltpu.VMEM((page, C_W), F32),
                        pltpu.VMEM((page, C_W), F32)])
    return pl.pallas_call(
        functools.partial(_attn_kernel, n_alias=len(aliases), pages=pages, nb=nb, nq=nq, steps_s=steps_s,
                          chunks_s=chunks_s, tq=tq, ts=ts, page=page, lam_init=lam_init),
        grid_spec=grid_spec,
        out_shape=[jax.ShapeDtypeStruct((m, C_W), BF16), kv_shape, kv_shape,
                   jax.ShapeDtypeStruct((nb_s * ts, C_W), F32)],
        input_output_aliases=aliases,
        compiler_params=_cparams("arbitrary", "arbitrary", "arbitrary"), name="attention")(page_table, *args)


def _sample_attn_step(s_id, nsteps, tbl_ref, q_ref, kn_ref, vn_ref, lamp_ref, nw_ref, k_refs, v_refs, o_ref,
                      m_s, l_s, acc_s, kn_s, vn_s, *, ts, page, lam_init):
    width = len(k_refs) * page
    rows = 2 * ts

    @pl.when(s_id == 0)
    def _():
        m_s[...] = jnp.full(m_s.shape, NEG_INF, F32)
        l_s[...] = jnp.zeros(l_s.shape, F32)
        acc_s[...] = jnp.zeros(acc_s.shape, F32)

    q = q_ref[...] * (C_HALF ** -0.5 * LOG2E)
    qq = []
    for h in range(C_HEADS):
        q1, q2 = _split_halves(q[:, h * HEAD_DIM:(h + 1) * HEAD_DIM])
        qq.append(jnp.concatenate([q1, q2], axis=0).astype(BF16))

    def update(keys, vals, bias):
        s = jnp.concatenate([_dot_nt(qq[h], keys(h)) for h in range(C_HEADS)], axis=0)
        if bias is not None:
            s = s + bias

        def pv(p):
            return jnp.concatenate([_dot(p[h * rows:(h + 1) * rows], vals(h)) for h in range(C_HEADS)], axis=0)

        _softmax_update(m_s, l_s, acc_s, slice(None), s, pv)

    def past_keys(h):
        return jnp.concatenate([r[h].astype(BF16) for r in k_refs], axis=0)

    def past_vals(h):
        return jnp.concatenate([r[h].astype(BF16) for r in v_refs], axis=0)

    @pl.when(s_id < nsteps - 1)
    def _():
        update(past_keys, past_vals, None)

    @pl.when(s_id == nsteps - 1)
    def _():
        tq_pos = lax.rem(lax.broadcasted_iota(jnp.int32, (rows, width), 0), ts)
        rel_past = tq_pos + width - lax.broadcasted_iota(jnp.int32, (rows, width), 1)
        rel_new = (lax.rem(lax.broadcasted_iota(jnp.int32, (rows, page), 0), ts)
                   - lax.broadcasted_iota(jnp.int32, (rows, page), 1))
        bias_past, bias_new = [], []
        for h in range(C_HEADS):
            far_bias = tbl_ref[(N_BUCKETS - 1) * C_HEADS + h]
            bias_past.append((_bias_from_rel(rel_past, tbl_ref, h) - far_bias) * LOG2E)
            bias_new.append(jnp.where(rel_new >= 0, (_bias_from_rel(rel_new, tbl_ref, h) - far_bias) * LOG2E, NEG_INF))
        update(past_keys, past_vals, jnp.concatenate(bias_past, axis=0))
        kn_s[...] = jnp.zeros(kn_s.shape, F32)
        vn_s[...] = jnp.zeros(vn_s.shape, F32)
        kn_s[0:ts, :] = kn_ref[...]
        vn_s[0:ts, :] = vn_ref[...]
        update(lambda h: kn_s[:, h * HEAD_DIM:(h + 1) * HEAD_DIM].astype(BF16),
               lambda h: vn_s[:, h * HEAD_DIM:(h + 1) * HEAD_DIM].astype(BF16),
               jnp.concatenate(bias_new, axis=0))
        lam = _lambda(lamp_ref, lam_init)
        on = acc_s[...] / l_s[...]
        for h in range(C_HEADS):
            o = on[h * rows:h * rows + ts] - lam * on[h * rows + ts:(h + 1) * rows]
            o_ref[:, h * HEAD_DIM:(h + 1) * HEAD_DIM] = (_rms(o, nw_ref[...]) * (1.0 - lam_init)).astype(o_ref.dtype)


def _outproj_rows(ob_ref, oc_ref, oa_ref, x_ref, wb_ref, wc_ref, wa_ref, nw_ref, xo_ref, ho_ref):
    acc = _dot(ob_ref[...], wb_ref[...]) + _dot(oc_ref[...], wc_ref[...]) + _dot(oa_ref[...], wa_ref[...])
    x = x_ref[...] + acc
    xo_ref[...] = x
    ho_ref[...] = _rms(x, nw_ref[...]).astype(ho_ref.dtype)


def _outproj_kernel(ob_ref, oc_ref, oa_ref, x_ref, obs_ref, ocs_ref, oas_ref, xs_ref, wb_ref, wc_ref, wa_ref,
                    nw_ref, xo_ref, ho_ref, xos_ref, hos_ref):
    weights = (wb_ref, wc_ref, wa_ref, nw_ref)
    _outproj_rows(ob_ref, oc_ref, oa_ref, x_ref, *weights, xo_ref, ho_ref)

    @pl.when(pl.program_id(0) == pl.num_programs(0) - 1)
    def _():
        _outproj_rows(obs_ref, ocs_ref, oas_ref, xs_ref, *weights, xos_ref, hos_ref)


def _outproj(mix, x, mix_s, xs, w, nw, l, tm):
    m, d = x.shape
    ms = xs.shape[0]
    rows = lambda r, width: pl.BlockSpec((r, width), lambda i: (i, 0))
    once = lambda r, width: pl.BlockSpec((r, width), lambda i: (0, 0))
    return pl.pallas_call(
        _outproj_kernel, grid=(m // tm,),
        in_specs=[rows(tm, B_W), rows(tm, C_W), rows(tm, A_W), rows(tm, d),
                  once(ms, B_W), once(ms, C_W), once(ms, A_W), once(ms, d),
                  pl.BlockSpec((None, B_W, d), lambda i: (l, 0, 0)),
                  pl.BlockSpec((None, C_W, d), lambda i: (l, 1, 0)),
                  pl.BlockSpec((None, A_W, d), lambda i: (l, (B_W + C_W) // A_W, 0)),
                  pl.BlockSpec((None, 1, d), lambda i: (l, 0, 0))],
        out_specs=[rows(tm, d), rows(tm, d), once(ms, d), once(ms, d)],
        out_shape=[jax.ShapeDtypeStruct((m, d), F32), jax.ShapeDtypeStruct((m, d), BF16),
                   jax.ShapeDtypeStruct((ms, d), F32), jax.ShapeDtypeStruct((ms, d), BF16)],
        compiler_params=_cparams("arbitrary"), name="outproj")(*mix, x, *mix_s, xs, w, w, w, nw)


def _ffn_up_rows(h_ref, wg_s, wu_s, cw_ref, cb_ref, buf_s, o_ref, so_ref, *, rows, head, stride):
    hn = h_ref[...]
    g = _dot(hn, wg_s[...])
    u = _dot(hn, wu_s[...])
    buf_s[head:head + rows, :] = g
    y = cw_ref[FFN_CONV - 1:FFN_CONV, :] * g + cb_ref[...]
    for tap in range(FFN_CONV - 1):
        off = head - (FFN_CONV - 1 - tap) * stride
        y = y + cw_ref[tap:tap + 1, :] * buf_s[off:off + rows, :]
    o_ref[...] = (_gelu(y) * u).astype(o_ref.dtype)
    last = buf_s[rows:rows + head, :]
    so_ref[...] = last
    buf_s[0:head, :] = last


def _ffn_up_kernel(h_ref, wg_ref, wu_ref, cw_ref, cb_ref, st_ref, hs_ref, sts_ref, o_ref, so_ref, os_ref, sos_ref,
                   buf_s, wg_s, wu_s, bufs_s, *, tm, blocks, stride_s):
    i = pl.program_id(1)

    @pl.when(i == 0)
    def _():
        wg_s[...] = wg_ref[...].astype(BF16)
        wu_s[...] = wu_ref[...].astype(BF16)

    @pl.when(i % blocks == 0)
    def _():
        buf_s[0:st_ref.shape[0], :] = st_ref[...]

    _ffn_up_rows(h_ref, wg_s, wu_s, cw_ref, cb_ref, buf_s, o_ref, so_ref, rows=tm, head=st_ref.shape[0], stride=1)

    @pl.when(i == pl.num_programs(1) - 1)
    def _():
        bufs_s[0:sts_ref.shape[0], :] = sts_ref[...]
        _ffn_up_rows(hs_ref, wg_s, wu_s, cw_ref, cb_ref, bufs_s, os_ref, sos_ref, rows=hs_ref.shape[0],
                     head=sts_ref.shape[0], stride=stride_s)


def _ffn_up(hn, hn_s, wg, wu, cw, cb, state, state_s, l, *, tm, tn, blocks, stride_s):
    m, d = hn.shape
    ms = hn_s.shape[0]
    dff = wg.shape[-1]
    head, head_s = state.shape[1], state_s.shape[1]
    nseq = state.shape[0]
    return pl.pallas_call(
        functools.partial(_ffn_up_kernel, tm=tm, blocks=blocks, stride_s=stride_s),
        grid=(dff // tn, m // tm),
        in_specs=[pl.BlockSpec((tm, d), lambda j, i: (i, 0)),
                  pl.BlockSpec((None, d, tn), lambda j, i: (l, 0, j)),
                  pl.BlockSpec((None, d, tn), lambda j, i: (l, 0, j)),
                  pl.BlockSpec((None, FFN_CONV, tn), lambda j, i: (l, 0, j)),
                  pl.BlockSpec((None, 1, tn), lambda j, i: (l, 0, j)),
                  pl.BlockSpec((None, head, tn), lambda j, i: (i // blocks, 0, j)),
                  pl.BlockSpec((ms, d), lambda j, i: (0, 0)),
                  pl.BlockSpec((None, head_s, tn), lambda j, i: (0, 0, j))],
        out_specs=[pl.BlockSpec((tm, tn), lambda j, i: (i, j)),
                   pl.BlockSpec((None, head, tn), lambda j, i: (i // blocks, 0, j)),
                   pl.BlockSpec((ms, tn), lambda j, i: (0, j)),
                   pl.BlockSpec((None, head_s, tn), lambda j, i: (0, 0, j))],
        out_shape=[jax.ShapeDtypeStruct((m, dff), BF16), jax.ShapeDtypeStruct((nseq, head, dff), F32),
                   jax.ShapeDtypeStruct((ms, dff), BF16), jax.ShapeDtypeStruct((1, head_s, dff), F32)],
        scratch_shapes=[pltpu.VMEM((tm + head, tn), F32), pltpu.VMEM((d, tn), BF16), pltpu.VMEM((d, tn), BF16),
                        pltpu.VMEM((ms + head_s, tn), F32)],
        compiler_params=_cparams("arbitrary", "arbitrary"), name="ffn_up")(
            hn, wg, wu, cw, cb, state, hn_s, state_s)


def _ffn_down_kernel(h_ref, x_ref, hs_ref, xs_ref, w_ref, o_ref, os_ref):
    o_ref[...] = x_ref[...] + _dot(h_ref[...], w_ref[...])

    @pl.when(pl.program_id(0) == pl.num_programs(0) - 1)
    def _():
        os_ref[...] = xs_ref[...] + _dot(hs_ref[...], w_ref[...])


def _ffn_down(hid, x, hid_s, xs, w, l, tm, tn):
    m, dff = hid.shape
    ms = hid_s.shape[0]
    d = x.shape[1]
    ni = m // tm
    s_tile = pl.BlockSpec((ms, tn), lambda i, j: (0, jnp.where(i == ni - 1, j, 0)))
    return pl.pallas_call(
        _ffn_down_kernel, grid=(ni, d // tn),
        in_specs=[pl.BlockSpec((tm, dff), lambda i, j: (i, 0)),
                  pl.BlockSpec((tm, tn), lambda i, j: (i, j)),
                  pl.BlockSpec((ms, dff), lambda i, j: (0, 0)),
                  s_tile,
                  pl.BlockSpec((None, dff, tn), lambda i, j: (l, 0, j))],
        out_specs=[pl.BlockSpec((tm, tn), lambda i, j: (i, j)), s_tile],
        out_shape=[jax.ShapeDtypeStruct((m, d), F32), jax.ShapeDtypeStruct((ms, d), F32)],
        compiler_params=_cparams("arbitrary", "arbitrary"), name="ffn_down")(hid, x, hid_s, xs, w)


def _pick(total, want):
    t = min(total, want)
    while total % t:
        t //= 2
    return t


def kernel(x_prompt, x_sample, cache_k, cache_v, state_conv_b, state_delta, state_ffn_conv, page_table, attn_norm_w, w_in, a_ln_w, a_ln_b, a_ws, a_bs, b_conv_w, b_a_log, b_dt_bias, b_norm_w, c_lam_q1, c_lam_k1, c_lam_q2, c_lam_k2, c_norm_w, rel_bias, w_out, ffn_norm_w, w_gate, w_up, ffn_conv_w, ffn_conv_b, w_down, final_norm_w):
    bp, tp, d = x_prompt.shape
    bs, ts, _ = x_sample.shape
    depth = w_in.shape[0]
    dff = w_gate.shape[-1]
    n_pool, page = cache_k.shape[1], cache_k.shape[2]
    mp, ms = bp * tp, bs * ts

    o_au, o_av, o_qkv, o_z, o_beta = 0, A_W, 2 * A_W, 2 * A_W + 3 * B_W, 2 * A_W + 4 * B_W
    o_cq = o_beta + 2 * B_HEADS
    o_ck, o_cv = o_cq + C_W, o_cq + 2 * C_W
    w_main = jnp.concatenate([
        w_in[:, :, o_qkv:o_z], w_in[:, :, o_z:o_beta], w_in[:, :, o_cq:o_ck], w_in[:, :, o_beta:o_cq],
        jnp.zeros((depth, d, COL_AU - COL_BA - 2 * B_HEADS), w_in.dtype),
        w_in[:, :, o_au:o_qkv], w_in[:, :, o_ck:o_cv + C_W]], axis=-1).astype(BF16)
    w_out_r = jnp.concatenate([w_out[:, A_W:], w_out[:, :A_W]], axis=1).astype(BF16)
    wg, wu = w_gate, w_up
    wd = w_down.astype(BF16)
    attn_nw = attn_norm_w.reshape(depth, 1, d)
    ffn_nw = ffn_norm_w.reshape(depth, 1, d)
    lnw = a_ln_w.reshape(depth, 1, A_W)
    lnb = a_ln_b.reshape(depth, 1, A_W)
    bcol_p = jnp.swapaxes(a_bs, 1, 2)
    tril = jnp.tril(jnp.ones((ts, ts), F32))
    assert ms <= A_CHUNK
    wmix_s = jnp.einsum('lhts,bc->lhtbsc', a_ws[:, :, :ts, :ts] * tril, jnp.eye(bs, dtype=F32)).reshape(
        depth, A_HEADS, ms, ms)
    wmix_s = jnp.pad(wmix_s, ((0, 0), (0, 0), (0, A_CHUNK - ms), (0, A_CHUNK - ms)))
    bcol_s = jnp.repeat(jnp.swapaxes(a_bs[:, :, :ts], 1, 2), bs, axis=1)
    pad_lanes = jnp.zeros((depth, 1, HEAD_DIM - 2 * B_HEADS), F32)
    alog = jnp.concatenate([jnp.zeros((depth, 1, B_HEADS), F32), b_a_log.reshape(depth, 1, B_HEADS), pad_lanes], -1)
    dtb = jnp.concatenate([jnp.zeros((depth, 1, B_HEADS), F32), b_dt_bias.reshape(depth, 1, B_HEADS), pad_lanes], -1)
    b_nw = b_norm_w.reshape(depth, 1, HEAD_DIM)
    c_nw = c_norm_w.reshape(depth, 1, HEAD_DIM)
    lamp = jnp.stack([c_lam_q1, c_lam_k1, c_lam_q2, c_lam_k2], axis=1)
    tbl = rel_bias.reshape(-1)
    ffn_cb = ffn_conv_b.reshape(depth, 1, dff)
    ck_flat = jnp.swapaxes(cache_k, 2, 3).reshape(depth * n_pool, C_HEADS, page, HEAD_DIM)
    cv_flat = jnp.swapaxes(cache_v, 2, 3).reshape(depth * n_pool, C_HEADS, page, HEAD_DIM)
    pt_flat = page_table.reshape(-1)

    conv_head = SUBLANES
    zero_conv_p = jnp.zeros((bp, conv_head, 3 * B_W), F32)
    zero_delta_p = jnp.zeros((bp, B_HEADS, HEAD_DIM, HEAD_DIM), F32)
    zero_ffn_p = jnp.zeros((bp, SUBLANES, dff), F32)
    ffn_head_s = max(SUBLANES, (FFN_CONV - 1) * bs)

    tm_in = _pick(mp, TM_IN)
    tm_out = _pick(mp, TM_OUT)
    tm_up = _pick(tp, TM_UP)
    tm_down = _pick(mp, TM_DOWN)
    tb_delta = _pick(tp, TB_DELTA)
    tq = _pick(tp, TQ)
    ts_pad_delta = DN_PAIR

    xp = x_prompt.reshape(mp, d)
    xs = jnp.swapaxes(x_sample, 0, 1).reshape(ms, d)

    def to_bm(a):
        return jnp.swapaxes(a.reshape(ts, bs, -1), 0, 1)

    def to_tm(a):
        return jnp.swapaxes(a, 0, 1).reshape(ms, -1)

    kv_stacks = None
    p_conv, p_delta, p_ffn = [], [], []
    s_k, s_v, s_conv, s_delta, s_ffn, s_av = [], [], [], [], [], []
    for l in range(depth):
        lam_init = 0.8 - 0.6 * math.exp(-0.3 * l)

        main, main_s = _inproj(xp, xs, attn_nw, w_main, l, tm_in, TN)
        main_bm = to_bm(main_s)
        q_bm = main_bm[:, :, COL_CQ:COL_CQ + C_W].reshape(ms, C_W)
        kn_bm = main_bm[:, :, COL_CK:COL_CK + C_W]
        vn_bm = main_bm[:, :, COL_CV:COL_CV + C_W]
        out_c, k_stack, v_stack, out_c_s = _attention(
            main, q_bm, kn_bm.reshape(ms, C_W), vn_bm.reshape(ms, C_W), ck_flat, cv_flat, pt_flat, tbl, lamp, c_nw,
            l, kv_stacks, depth=depth, nb=bp, t_total=tp, tq=tq, nb_s=bs, ts=ts, n_pool=n_pool, lam_init=lam_init)
        kv_stacks = (k_stack, v_stack)

        (out_a,) = _gmlp(main, lnw, lnb, a_ws, bcol_p, l, rows=4 * A_CHUNK, chunk=A_CHUNK, causal_mask=True,
                         want_v=False)
        out_b, delta_new = _delta(main, zero_conv_p, b_conv_w, alog, dtb, b_nw, zero_delta_p, l,
                                  nb=bp, t_total=tp, t_valid=tp, tb=tb_delta)
        main3 = main.reshape(bp, tp, N_MAIN)
        p_conv.append(main3[:, tp - (B_CONV - 1):, COL_BQ:COL_BQ + 3 * B_W])
        p_delta.append(delta_new)

        out_a_s, va_s = _gmlp(main_s, lnw, lnb, wmix_s, bcol_s, l, rows=ms, chunk=A_CHUNK, causal_mask=False,
                              want_v=True)
        main_pad = jnp.pad(main_bm, ((0, 0), (0, ts_pad_delta - ts), (0, 0))).reshape(bs * ts_pad_delta, N_MAIN)
        conv_s = jnp.pad(state_conv_b[l], ((0, 0), (conv_head - (B_CONV - 1), 0), (0, 0)))
        out_b_s, delta_new_s = _delta(main_pad, conv_s, b_conv_w, alog, dtb, b_nw, state_delta[l], l,
                                      nb=bs, t_total=ts_pad_delta, t_valid=ts, tb=ts_pad_delta)
        out_b_s = to_tm(out_b_s.reshape(bs, ts_pad_delta, B_W)[:, :ts])
        out_c_s = to_tm(out_c_s.reshape(bs, ts, C_W)).astype(BF16)

        xp, hn2, xs, hn2_s = _outproj((out_b, out_c, out_a), xp, (out_b_s, out_c_s, out_a_s), xs, w_out_r, ffn_nw,
                                      l, tm_out)
        ffn_state_s = jnp.swapaxes(state_ffn_conv[l], 0, 1).reshape(1, (FFN_CONV - 1) * bs, dff)
        ffn_state_s = jnp.pad(ffn_state_s, ((0, 0), (ffn_head_s - (FFN_CONV - 1) * bs, 0), (0, 0)))

        hid, ffn_last, hid_s, ffn_last_s = _ffn_up(hn2, hn2_s, wg, wu, ffn_conv_w, ffn_cb, zero_ffn_p, ffn_state_s,
                                                   l, tm=tm_up, tn=TN, blocks=tp // tm_up, stride_s=bs)
        xp, xs = _ffn_down(hid, xp, hid_s, xs, wd, l, tm_down, TN)
        p_ffn.append(ffn_last[:, SUBLANES - (FFN_CONV - 1):])
        s_k.append(kn_bm.reshape(bs, ts, C_HEADS, HEAD_DIM))
        s_v.append(vn_bm.reshape(bs, ts, C_HEADS, HEAD_DIM))
        s_conv.append(main_bm[:, ts - (B_CONV - 1):, COL_BQ:COL_BQ + 3 * B_W])
        s_delta.append(delta_new_s)
        s_ffn.append(jnp.swapaxes(ffn_last_s[0, ffn_head_s - (FFN_CONV - 1) * bs:].reshape(FFN_CONV - 1, bs, dff),
                                  0, 1))
        s_av.append(to_bm(va_s).reshape(bs, ts, A_HEADS, HEAD_DIM))

    y_prompt = _rmsnorm(xp, final_norm_w, _pick(mp, 512)).reshape(bp, tp, d)
    y_sample = jnp.swapaxes(_rmsnorm(xs, final_norm_w, ms).reshape(ts, bs, d), 0, 1)
    k_prompt, v_prompt = (jnp.swapaxes(a, 2, 3) for a in kv_stacks)
    return (y_prompt, y_sample,
            k_prompt, v_prompt, jnp.stack(p_conv), jnp.stack(p_delta), jnp.stack(p_ffn),
            jnp.stack(s_k), jnp.stack(s_v), jnp.stack(s_conv), jnp.stack(s_delta), jnp.stack(s_ffn),
            jnp.stack(s_av))
```

```python
import functools
import math

import numpy as np
import jax
import jax.numpy as jnp
from jax import lax
from jax.experimental import pallas as pl
from jax.experimental.pallas import tpu as pltpu

F32 = jnp.float32
BF16 = jnp.bfloat16
HI = lax.Precision.HIGHEST

HEAD_DIM = 128
SUBLANES = 8
A_HEADS, B_HEADS, C_HEADS = 4, 6, 6
A_W, B_W, C_W = A_HEADS * HEAD_DIM, B_HEADS * HEAD_DIM, C_HEADS * HEAD_DIM
A_CHUNK = 128
B_CONV = 4
DN_CHUNK = 64
DN_PAIR = 2 * DN_CHUNK
C_HALF = HEAD_DIM // 2
N_BUCKETS = 32
MAX_DISTANCE = 128
FFN_CONV = 3
EPS = 1e-6
NEG_INF = -1e30
VMEM_LIMIT = 56 * 1024 * 1024
TM_IN, TM_OUT, TM_UP, TM_DOWN, TB_DELTA, TQ = 1024, 512, 1024, 1024, 256, 512
TN = 512
PAGES_PER_STEP = 16
LOG2E = math.log2(math.e)

COL_BQ, COL_BK, COL_BV, COL_BZ, COL_CQ = 0, B_W, 2 * B_W, 3 * B_W, 4 * B_W
COL_BA = COL_CQ + C_W
COL_AU = COL_BA + 2 * HEAD_DIM
COL_AV = COL_AU + A_W
COL_CK = COL_AV + A_W
COL_CV = COL_CK + C_W
N_MAIN = COL_CV + C_W


def _t5_bucket_starts():
    n = np.arange(4 * MAX_DISTANCE)
    max_exact = N_BUCKETS // 2
    nf = np.maximum(n, max_exact).astype(np.float64)
    large = max_exact + (np.log(nf / max_exact) / math.log(MAX_DISTANCE / max_exact)
                         * (N_BUCKETS - max_exact)).astype(np.int64)
    bucket = np.where(n < max_exact, n, np.minimum(large, N_BUCKETS - 1))
    return [int(n[bucket == b].min()) for b in range(N_BUCKETS)]


BUCKET_START = _t5_bucket_starts()
FAR_DISTANCE = BUCKET_START[-1]


def _cparams(*sem):
    return pltpu.CompilerParams(dimension_semantics=sem, vmem_limit_bytes=VMEM_LIMIT)


def _gelu(x):
    return 0.5 * x * (1.0 + jnp.tanh(math.sqrt(2.0 / math.pi) * (x + 0.044715 * (x * x * x))))


def _sigmoid(x):
    return 1.0 / (1.0 + jnp.exp(-x))


def _silu(x):
    return x * _sigmoid(x)


def _softplus(x):
    return jnp.maximum(x, 0.0) + jnp.log(1.0 + jnp.exp(-jnp.abs(x)))


def _rms(x, w):
    return x * lax.rsqrt(jnp.mean(x * x, axis=-1, keepdims=True) + EPS) * w


def _dot(a, b, **kw):
    return jnp.dot(a, b, preferred_element_type=F32, **kw)


def _dot_nt(a, b, **kw):
    return lax.dot_general(a, b, (((1,), (1,)), ((), ())), preferred_element_type=F32, **kw)


def _split2(x):
    hi = x.astype(BF16)
    return hi, (x - hi.astype(F32)).astype(BF16)


def _split3(x):
    hi = x.astype(BF16)
    r = x - hi.astype(F32)
    mid = r.astype(BF16)
    return hi, mid, (r - mid.astype(F32)).astype(BF16)


def _dot3(a_hl, b_hl):
    (ah, al), (bh, bl) = a_hl, b_hl
    return _dot(jnp.concatenate([ah, ah, al], axis=1), jnp.concatenate([bh, bl, bh], axis=0))


def _rmsnorm_kernel(x_ref, w_ref, o_ref):
    o_ref[...] = _rms(x_ref[...], w_ref[...]).astype(o_ref.dtype)


def _rmsnorm(x, w, tm):
    m, d = x.shape
    return pl.pallas_call(
        _rmsnorm_kernel, grid=(m // tm,),
        in_specs=[pl.BlockSpec((tm, d), lambda i: (i, 0)), pl.BlockSpec((1, d), lambda i: (0, 0))],
        out_specs=pl.BlockSpec((tm, d), lambda i: (i, 0)),
        out_shape=jax.ShapeDtypeStruct((m, d), F32),
        compiler_params=_cparams("parallel"), name="final_rmsnorm")(x, w.reshape(1, d))


def _inproj_kernel(x_ref, xs_ref, nw_ref, w_ref, o_ref, os_ref, hn_s, hns_s):
    i, j = pl.program_id(0), pl.program_id(1)

    @pl.when(j == 0)
    def _():
        hn_s[...] = _rms(x_ref[...], nw_ref[...]).astype(BF16)

    o_ref[...] = _dot(hn_s[...], w_ref[...])

    @pl.when(i == pl.num_programs(0) - 1)
    def _():
        @pl.when(j == 0)
        def _():
            hns_s[...] = _rms(xs_ref[...], nw_ref[...]).astype(BF16)

        os_ref[...] = _dot(hns_s[...], w_ref[...])


def _inproj(x, xs, nw, w, l, tm, tn):
    m, d = x.shape
    ms = xs.shape[0]
    n = w.shape[-1]
    ni = m // tm
    return pl.pallas_call(
        _inproj_kernel, grid=(ni, n // tn),
        in_specs=[pl.BlockSpec((tm, d), lambda i, j: (i, 0)),
                  pl.BlockSpec((ms, d), lambda i, j: (0, 0)),
                  pl.BlockSpec((None, 1, d), lambda i, j: (l, 0, 0)),
                  pl.BlockSpec((None, d, tn), lambda i, j: (l, 0, j))],
        out_specs=[pl.BlockSpec((tm, tn), lambda i, j: (i, j)),
                   pl.BlockSpec((ms, tn), lambda i, j: (0, jnp.where(i == ni - 1, j, 0)))],
        out_shape=[jax.ShapeDtypeStruct((m, n), F32), jax.ShapeDtypeStruct((ms, n), F32)],
        scratch_shapes=[pltpu.VMEM((tm, d), BF16), pltpu.VMEM((ms, d), BF16)],
        compiler_params=_cparams("arbitrary", "arbitrary"), name="inproj")(x, xs, nw, w)


def _gmlp_kernel(u_ref, v_ref, lnw_ref, lnb_ref, w_ref, bcol_ref, *out_refs, chunk, rows, causal_mask):
    o_ref = out_refs[0]
    nchunk = max(rows // chunk, 1)
    live = min(rows, chunk)
    if causal_mask:
        ii = lax.broadcasted_iota(jnp.int32, (chunk, chunk), 0)
        jj = lax.broadcasted_iota(jnp.int32, (chunk, chunk), 1)
        keep = ii >= jj
    for h in range(A_HEADS):
        sl = slice(h * HEAD_DIM, (h + 1) * HEAD_DIM)
        u = _gelu(u_ref[:, sl])
        v = _gelu(v_ref[:, sl])
        mu = jnp.mean(v, axis=-1, keepdims=True)
        vc = v - mu
        vn = vc * lax.rsqrt(jnp.mean(vc * vc, axis=-1, keepdims=True) + EPS) * lnw_ref[:, sl] + lnb_ref[:, sl]
        if len(out_refs) > 1:
            out_refs[1][:, sl] = vn
        w = w_ref[h]
        if causal_mask:
            w = jnp.where(keep, w, 0.0)
        wb = w.astype(BF16)
        vb = vn.astype(BF16)
        bcol = bcol_ref[0:live, h:h + 1]
        if live < chunk:
            vb = jnp.concatenate([vb, jnp.zeros((chunk - live, HEAD_DIM), BF16)], axis=0)
        for c in range(nchunk):
            rs = slice(c * live, (c + 1) * live)
            mixed = _dot(wb, vb[c * chunk:(c + 1) * chunk])[0:live] + bcol
            o_ref[rs, sl] = (u[rs] * mixed).astype(o_ref.dtype)


def _gmlp(main, lnw, lnb, wmix, bcol, l, *, rows, chunk, causal_mask, want_v):
    m = main.shape[0]
    live = min(rows, chunk)
    cu, cv = COL_AU // A_W, COL_AV // A_W
    out_shape = [jax.ShapeDtypeStruct((m, A_W), BF16)]
    out_specs = [pl.BlockSpec((rows, A_W), lambda i: (i, 0))]
    if want_v:
        out_shape.append(jax.ShapeDtypeStruct((m, A_W), F32))
        out_specs.append(pl.BlockSpec((rows, A_W), lambda i: (i, 0)))
    return pl.pallas_call(
        functools.partial(_gmlp_kernel, chunk=chunk, rows=rows, causal_mask=causal_mask),
        grid=(m // rows,),
        in_specs=[pl.BlockSpec((rows, A_W), lambda i: (i, cu)),
                  pl.BlockSpec((rows, A_W), lambda i: (i, cv)),
                  pl.BlockSpec((None, 1, A_W), lambda i: (l, 0, 0)),
                  pl.BlockSpec((None, 1, A_W), lambda i: (l, 0, 0)),
                  pl.BlockSpec((None, A_HEADS, chunk, chunk), lambda i: (l, 0, 0, 0)),
                  pl.BlockSpec((None, live, A_HEADS), lambda i: (l, 0, 0))],
        out_specs=out_specs, out_shape=out_shape,
        compiler_params=_cparams("parallel"), name="gmlp")(main, main, lnw, lnb, wmix, bcol)


def _delta_kernel(q_ref, k_ref, v_ref, z_ref, ba_ref, cs_ref, cw_ref, alog_ref, dtb_ref, nw_ref, s0_ref,
                  o_ref, s_ref, buf_s, qs_s, ks_s, vs_s, g_s, bt_s, gc_s, u_s, w_s, ai_s, qd_s, kd_s,
                  *, tb, t_valid, t_total):
    t = pl.program_id(1)
    head = SUBLANES

    @pl.when(t == 0)
    def _():
        s_ref[...] = s0_ref[...]
        for j in range(3):
            buf_s[j, 0:head, :] = cs_ref[:, j * B_W:(j + 1) * B_W]

    if t_valid < t_total:
        row = t * tb + lax.broadcasted_iota(jnp.int32, (tb, 1), 0)
        live = row < t_valid
    else:
        live = None

    def keep(x):
        return x if live is None else jnp.where(live, x, 0.0)

    for j, (src, dst) in enumerate(((q_ref, qs_s), (k_ref, ks_s), (v_ref, vs_s))):
        x = src[...]
        buf_s[j, head:head + tb, :] = x
        w = cw_ref[:, j * B_W:(j + 1) * B_W]
        y = w[B_CONV - 1:B_CONV] * x
        for tap in range(B_CONV - 1):
            off = head - (B_CONV - 1) + tap
            y = y + w[tap:tap + 1] * buf_s[j, off:off + tb, :]
        buf_s[j, 0:head, :] = buf_s[j, tb:tb + head, :]
        y = keep(_silu(y))
        for h in range(B_HEADS):
            sl = slice(h * HEAD_DIM, (h + 1) * HEAD_DIM)
            seg = y[:, sl]
            if j == 0:
                seg = seg * lax.rsqrt(jnp.sum(seg * seg, axis=-1, keepdims=True) + EPS) * (HEAD_DIM ** -0.5)
            elif j == 1:
                seg = seg * lax.rsqrt(jnp.sum(seg * seg, axis=-1, keepdims=True) + EPS)
            dst[h] = seg

    ba = ba_ref[...]
    bt_s[...] = keep(_sigmoid(ba))
    g_s[...] = keep(-jnp.exp(alog_ref[...]) * _softplus(ba + dtb_ref[...]))

    n = DN_PAIR
    ii = lax.broadcasted_iota(jnp.int32, (n, n), 0)
    jj = lax.broadcasted_iota(jnp.int32, (n, n), 1)
    same = (ii >= DN_CHUNK) == (jj >= DN_CHUNK)
    incl = same & (ii >= jj)
    strict = same & (ii > jj)
    tri = jnp.where(incl, 1.0, 0.0)
    eye = jnp.where(ii == jj, 1.0, 0.0)
    rowi = lax.broadcasted_iota(jnp.int32, (n, 1), 0)
    lane = lax.broadcasted_iota(jnp.int32, (n, n), 1)
    zeros_half = jnp.zeros((DN_CHUNK, HEAD_DIM), F32)

    tri3 = jnp.concatenate([tri.astype(BF16)] * 3, axis=1)

    heads = range(B_HEADS)
    gcol_of = lambda gc, h: gc[:, B_HEADS + h:B_HEADS + h + 1]

    def intra(p, carry):
        r0 = pl.multiple_of(p * n, n)
        rows = pl.ds(r0, n)
        gc = _dot(tri3, jnp.concatenate(_split3(g_s[rows, :]), axis=0))
        gct = gc.T
        bt = bt_s[rows, :]
        gc_s[rows, :] = gc
        q = [qs_s[h, rows, :] for h in heads]
        k = [ks_s[h, rows, :] for h in heads]
        decay = [jnp.exp(jnp.where(incl, gcol_of(gc, h) - gct[B_HEADS + h:B_HEADS + h + 1, :], NEG_INF))
                 for h in heads]
        kb = [k[h] * bt[:, h:h + 1] for h in heads]
        kq = [_dot_nt(jnp.concatenate([kb[h].astype(BF16), q[h].astype(BF16)], axis=0), k[h].astype(BF16))
              for h in heads]
        a = [-jnp.where(strict, kq[h][0:n] * decay[h], 0.0) for h in heads]
        pm = [eye + a[h] for h in heads]
        a_hl = [_split2(a[h]) for h in heads]
        a = [_dot3(a_hl[h], a_hl[h]) for h in heads]
        for _ in range(int(math.log2(DN_CHUNK)) - 2):
            a_hl = [_split2(a[h]) for h in heads]
            pm_hl = [_split2(pm[h]) for h in heads]
            prod = [_dot3(tuple(jnp.concatenate([x, y], axis=0) for x, y in zip(a_hl[h], pm_hl[h])), a_hl[h])
                    for h in heads]
            a = [prod[h][0:n] for h in heads]
            pm = [pm[h] + prod[h][n:2 * n] for h in heads]
        prod = [_dot3(_split2(pm[h]), _split2(a[h])) for h in heads]
        pm = [pm[h] + prod[h] for h in heads]
        egc = [jnp.exp(gcol_of(gc, h)) for h in heads]
        sol = [_dot3(_split2(pm[h]),
                     _split2(jnp.concatenate([vs_s[h, rows, :] * bt[:, h:h + 1], kb[h] * egc[h]], axis=1)))
               for h in heads]
        for h in heads:
            u_s[h, rows, :] = sol[h][:, 0:HEAD_DIM]
            w_s[h, rows, :] = sol[h][:, HEAD_DIM:2 * HEAD_DIM].astype(BF16)
            ai_s[h, rows, :] = (kq[h][n:2 * n] * decay[h]).astype(BF16)
            qd_s[h, rows, :] = (q[h] * egc[h]).astype(BF16)
            gcol = gcol_of(gc, h)
            glast = jnp.where(rowi < DN_CHUNK, gcol[DN_CHUNK - 1:DN_CHUNK], gcol[n - 1:n])
            kdt = (k[h] * jnp.exp(glast - gcol)).T
            kd_s[h, p, 0] = jnp.where(lane < DN_CHUNK, kdt, 0.0).astype(BF16)
            kd_s[h, p, 1] = jnp.where(lane >= DN_CHUNK, kdt, 0.0).astype(BF16)
        return carry

    def inter(p, carry):
        r0 = pl.multiple_of(p * n, n)
        s = [s_ref[h] for h in heads]
        outs = [[] for _ in heads]
        for c in range(2):
            rs = pl.ds(pl.multiple_of(r0 + c * DN_CHUNK, DN_CHUNK), DN_CHUNK)
            ws = [_dot(jnp.concatenate([w_s[h, rs, :], qd_s[h, rs, :]], axis=0), s[h].astype(BF16)) for h in heads]
            v_new = [u_s[h, rs, :] - ws[h][0:DN_CHUNK] for h in heads]
            v_pad = [jnp.concatenate([v_new[h], zeros_half] if c == 0 else [zeros_half, v_new[h]],
                                     axis=0).astype(BF16) for h in heads]
            av = [_dot(jnp.concatenate([ai_s[h, rs, :], kd_s[h, p, c]], axis=0), v_pad[h]) for h in heads]
            last = gc_s[pl.ds(pl.multiple_of(r0 + (c + 1) * DN_CHUNK - SUBLANES, SUBLANES), SUBLANES), :]
            for h in heads:
                outs[h].append(ws[h][DN_CHUNK:2 * DN_CHUNK] + av[h][0:DN_CHUNK])
                gl = jnp.exp(last[SUBLANES - 1:SUBLANES, B_HEADS + h:B_HEADS + h + 1])
                s[h] = s[h] * gl + av[h][DN_CHUNK:DN_CHUNK + HEAD_DIM]
        for h in heads:
            sl = slice(h * HEAD_DIM, (h + 1) * HEAD_DIM)
            s_ref[h] = s[h]
            o = jnp.concatenate(outs[h], axis=0)
            z = z_ref[pl.ds(r0, n), sl]
            o_ref[pl.ds(r0, n), sl] = (_rms(o, nw_ref[...]) * _silu(z)).astype(o_ref.dtype)
        return carry

    lax.fori_loop(0, tb // n, intra, 0)
    lax.fori_loop(0, tb // n, inter, 0)


def _delta(main, conv_state, conv_w, alog, dtb, nw, s0, l, *, nb, t_total, t_valid, tb):
    m = main.shape[0]
    nt = t_total // tb
    cq, ck, cv, cz, cba = COL_BQ // B_W, COL_BK // B_W, COL_BV // B_W, COL_BZ // B_W, COL_BA // HEAD_DIM
    row = lambda b, t: b * nt + t
    return pl.pallas_call(
        functools.partial(_delta_kernel, tb=tb, t_valid=t_valid, t_total=t_total),
        grid=(nb, nt),
        in_specs=[pl.BlockSpec((tb, B_W), lambda b, t: (row(b, t), cq)),
                  pl.BlockSpec((tb, B_W), lambda b, t: (row(b, t), ck)),
                  pl.BlockSpec((tb, B_W), lambda b, t: (row(b, t), cv)),
                  pl.BlockSpec((tb, B_W), lambda b, t: (row(b, t), cz)),
                  pl.BlockSpec((tb, HEAD_DIM), lambda b, t: (row(b, t), cba)),
                  pl.BlockSpec((None, SUBLANES, 3 * B_W), lambda b, t: (b, 0, 0)),
                  pl.BlockSpec((None, B_CONV, 3 * B_W), lambda b, t: (l, 0, 0)),
                  pl.BlockSpec((None, 1, HEAD_DIM), lambda b, t: (l, 0, 0)),
                  pl.BlockSpec((None, 1, HEAD_DIM), lambda b, t: (l, 0, 0)),
                  pl.BlockSpec((None, 1, HEAD_DIM), lambda b, t: (l, 0, 0)),
                  pl.BlockSpec((None, B_HEADS, HEAD_DIM, HEAD_DIM), lambda b, t: (b, 0, 0, 0))],
        out_specs=[pl.BlockSpec((tb, B_W), lambda b, t: (row(b, t), 0)),
                   pl.BlockSpec((None, B_HEADS, HEAD_DIM, HEAD_DIM), lambda b, t: (b, 0, 0, 0))],
        out_shape=[jax.ShapeDtypeStruct((m, B_W), BF16),
                   jax.ShapeDtypeStruct((nb, B_HEADS, HEAD_DIM, HEAD_DIM), F32)],
        scratch_shapes=[pltpu.VMEM((3, tb + SUBLANES, B_W), F32),
                        pltpu.VMEM((B_HEADS, tb, HEAD_DIM), F32),
                        pltpu.VMEM((B_HEADS, tb, HEAD_DIM), F32),
                        pltpu.VMEM((B_HEADS, tb, HEAD_DIM), F32),
                        pltpu.VMEM((tb, HEAD_DIM), F32),
                        pltpu.VMEM((tb, HEAD_DIM), F32),
                        pltpu.VMEM((tb, HEAD_DIM), F32),
                        pltpu.VMEM((B_HEADS, tb, HEAD_DIM), F32),
                        pltpu.VMEM((B_HEADS, tb, HEAD_DIM), BF16),
                        pltpu.VMEM((B_HEADS, tb, HEAD_DIM), BF16),
                        pltpu.VMEM((B_HEADS, tb, HEAD_DIM), BF16),
                        pltpu.VMEM((B_HEADS, tb // DN_PAIR, 2, HEAD_DIM, DN_PAIR), BF16)],
        compiler_params=_cparams("parallel", "arbitrary"), name="deltanet")(
            main, main, main, main, main, conv_state, conv_w, alog, dtb, nw, s0)


def _bias_from_rel(rel, tbl_ref, h):
    bias = jnp.full(rel.shape, tbl_ref[h], F32)
    for b in range(1, N_BUCKETS):
        bias = jnp.where(rel >= BUCKET_START[b], tbl_ref[b * C_HEADS + h], bias)
    return bias


def _lambda(lamp_ref, lam_init):
    lp = lamp_ref[...]
    s1 = jnp.sum(lp[0:1] * lp[1:2], axis=-1, keepdims=True)
    s2 = jnp.sum(lp[2:3] * lp[3:4], axis=-1, keepdims=True)
    return jnp.exp(s1) - jnp.exp(s2) + lam_init


def _split_halves(qh):
    lane = lax.broadcasted_iota(jnp.int32, qh.shape, 1)
    return jnp.where(lane < C_HALF, qh, 0.0), jnp.where(lane >= C_HALF, qh, 0.0)


def _softmax_update(m_ref, l_ref, acc_ref, idx, s, pv):
    chunks = [s[:, c * HEAD_DIM:(c + 1) * HEAD_DIM] for c in range(s.shape[1] // HEAD_DIM)]
    mx = chunks[0]
    for c in chunks[1:]:
        mx = jnp.maximum(mx, c)
    m_old = m_ref[idx]
    m_new = jnp.maximum(m_old, jnp.max(mx, axis=-1, keepdims=True))
    p = [jnp.exp2(c - m_new) for c in chunks]
    ps = p[0]
    for c in p[1:]:
        ps = ps + c
    a = jnp.exp2(m_old - m_new)
    l_ref[idx] = a * l_ref[idx] + jnp.sum(ps, axis=-1, keepdims=True)
    acc_ref[idx] = a * acc_ref[idx] + pv(jnp.concatenate([c.astype(BF16) for c in p], axis=1))
    m_ref[idx] = m_new


def _prompt_attn_step(h, b, qi, tbl_ref, q_ref, k_ref, v_ref, lamp_ref, nw_ref, o_ref, ko_ref, vo_ref,
                      kb_s, vb_s, bias_s, m_s, l_s, acc_s, *, tq, lam_init):
    far_bias = tbl_ref[(N_BUCKETS - 1) * C_HEADS + h]

    @pl.when((qi == 0) & (b == 0))
    def _():
        r = lax.broadcasted_iota(jnp.int32, (tq, tq), 0) - lax.broadcasted_iota(jnp.int32, (tq, tq), 1)
        bias_s[0] = jnp.where(r >= 0, (_bias_from_rel(r, tbl_ref, h) - far_bias) * LOG2E, NEG_INF)
        bias_s[1] = (_bias_from_rel(r + tq, tbl_ref, h) - far_bias) * LOG2E

    @pl.when(qi == 0)
    def _():
        kf = k_ref[...]
        vf = v_ref[...]
        ko_ref[...] = kf
        vo_ref[...] = vf
        kb_s[...] = kf.astype(BF16)
        vb_s[...] = vf.astype(BF16)

    m_s[...] = jnp.full(m_s.shape, NEG_INF, F32)
    l_s[...] = jnp.zeros(l_s.shape, F32)
    acc_s[...] = jnp.zeros(acc_s.shape, F32)
    q1, q2 = _split_halves(q_ref[...] * (C_HALF ** -0.5 * LOG2E))
    q1 = q1.astype(BF16)
    q2 = q2.astype(BF16)

    def blocks(kblocks, biases):
        starts = [pl.multiple_of(kj * tq, tq) for kj in kblocks]
        scores = [[_dot_nt(qq, kb_s[pl.ds(st, tq), :]) for qq in (q1, q2)] for st in starts]
        for st, sc, bias in zip(starts, scores, biases):
            vb = vb_s[pl.ds(st, tq), :]
            for idx, s in enumerate(sc):
                if bias is not None:
                    s = s + bias
                _softmax_update(m_s, l_s, acc_s, idx, s, lambda p: _dot(p, vb))

    n_far = jnp.maximum(qi - 1, 0)

    def far_pair(j, carry):
        blocks([2 * j, 2 * j + 1], [None, None])
        return carry

    lax.fori_loop(0, n_far // 2, far_pair, 0)

    @pl.when(lax.rem(n_far, 2) == 1)
    def _():
        blocks([n_far - 1], [None])

    @pl.when(qi >= 1)
    def _():
        blocks([qi - 1, qi], [bias_s[1], bias_s[0]])

    @pl.when(qi == 0)
    def _():
        blocks([qi], [bias_s[0]])

    lam = _lambda(lamp_ref, lam_init)
    o = acc_s[0] / l_s[0] - lam * (acc_s[1] / l_s[1])
    o_ref[...] = (_rms(o, nw_ref[...]) * (1.0 - lam_init)).astype(o_ref.dtype)


N_PROMPT_IN, N_SAMPLE_IN, N_PROMPT_SCRATCH = 5, 3, 6


def _attn_kernel(pt_ref, tbl_ref, *refs, n_alias, pages, nb, nq, skip, steps_s, chunks_s, tq, ts, page, lam_init):
    del pt_ref
    q_ref, k_ref, v_ref, lamp_ref, nw_ref = refs[:N_PROMPT_IN]
    refs = refs[N_PROMPT_IN + n_alias:]
    qs_ref, kn_ref, vn_ref = refs[:N_SAMPLE_IN]
    k_refs = refs[N_SAMPLE_IN:N_SAMPLE_IN + pages]
    v_refs = refs[N_SAMPLE_IN + pages:N_SAMPLE_IN + 2 * pages]
    o_ref, ko_ref, vo_ref, os_ref = refs[N_SAMPLE_IN + 2 * pages:N_SAMPLE_IN + 2 * pages + 4]
    scratch = refs[N_SAMPLE_IN + 2 * pages + 4:]
    h, b, qi = pl.program_id(0), pl.program_id(1), pl.program_id(2)
    _prompt_attn_step(h, b, qi, tbl_ref, q_ref, k_ref, v_ref, lamp_ref, nw_ref, o_ref, ko_ref, vo_ref,
                      *scratch[:N_PROMPT_SCRATCH], tq=tq, lam_init=lam_init)
    step = (h * nb + b) * (nq - skip) + qi - skip

    @pl.when((qi >= skip) & (step < steps_s))
    def _():
        _sample_attn_step(lax.rem(step, chunks_s), chunks_s, tbl_ref, qs_ref, kn_ref, vn_ref, lamp_ref, nw_ref,
                          k_refs, v_refs, os_ref, *scratch[N_PROMPT_SCRATCH:], ts=ts, page=page, lam_init=lam_init)


def _attention(main, q_bm, kn_bm, vn_bm, cache_k, cache_v, page_table, tbl, lamp, nw, l, kv_stacks, *,
               depth, nb, t_total, tq, nb_s, ts, n_pool, lam_init):
    assert tq > FAR_DISTANCE
    m = main.shape[0]
    nq = t_total // tq
    page = cache_k.shape[2]
    n_pages = page_table.shape[0] // nb_s
    pages = math.gcd(n_pages, PAGES_PER_STEP)
    chunks_s = n_pages // pages
    steps_s = nb_s * chunks_s
    assert pages * page > FAR_DISTANCE + ts
    skip = 2 if steps_s <= C_HEADS * nb * (nq - 2) else 0
    assert steps_s <= C_HEADS * nb * (nq - skip)
    cq, ck, cv = COL_CQ // HEAD_DIM, COL_CK // HEAD_DIM, COL_CV // HEAD_DIM

    def s_step(h, b, i):
        return jnp.minimum((h * nb + b) * (nq - skip) + jnp.maximum(i - skip, 0), steps_s - 1)

    def kv_page(k):
        def index(h, b, i, pt):
            s = s_step(h, b, i)
            return (l * n_pool + pt[(s // chunks_s) * n_pages + (s % chunks_s) * pages + k], 0, 0, 0)
        return pl.BlockSpec((None, C_HEADS, page, HEAD_DIM), index)

    tok = pl.BlockSpec((ts, C_W), lambda h, b, i, pt: (s_step(h, b, i) // chunks_s, 0))
    kv_shape = jax.ShapeDtypeStruct((depth, nb, C_HEADS, t_total, HEAD_DIM), F32)
    kv_spec = pl.BlockSpec((None, None, None, t_total, HEAD_DIM), lambda h, b, i, pt: (l, b, h, 0, 0))
    in_specs = [pl.BlockSpec(memory_space=pltpu.SMEM),
                pl.BlockSpec((tq, HEAD_DIM), lambda h, b, i, pt: (b * nq + i, cq + h)),
                pl.BlockSpec((t_total, HEAD_DIM), lambda h, b, i, pt: (b, ck + h)),
                pl.BlockSpec((t_total, HEAD_DIM), lambda h, b, i, pt: (b, cv + h)),
                pl.BlockSpec((None, 4, C_HALF), lambda h, b, i, pt: (l, 0, 0)),
                pl.BlockSpec((None, 1, HEAD_DIM), lambda h, b, i, pt: (l, 0, 0))]
    args = [tbl, main, main, main, lamp, nw]
    aliases = {}
    if kv_stacks is not None:
        aliases = {1 + len(args): 1, 2 + len(args): 2}
        in_specs += [pl.BlockSpec(memory_space=pl.ANY)] * 2
        args += list(kv_stacks)
    in_specs += [tok, tok, tok] + [kv_page(k) for k in range(pages)] + [kv_page(k) for k in range(pages)]
    args += [q_bm, kn_bm, vn_bm] + [cache_k] * pages + [cache_v] * pages
    rows_s = C_HEADS * 2 * ts
    grid_spec = pltpu.PrefetchScalarGridSpec(
        num_scalar_prefetch=1, grid=(C_HEADS, nb, nq), in_specs=in_specs,
        out_specs=[pl.BlockSpec((tq, HEAD_DIM), lambda h, b, i, pt: (b * nq + i, h)), kv_spec, kv_spec, tok],
        scratch_shapes=[pltpu.VMEM((t_total, HEAD_DIM), BF16),
                        pltpu.VMEM((t_total, HEAD_DIM), BF16),
                        pltpu.VMEM((2, tq, tq), F32),
                        pltpu.VMEM((2, tq, HEAD_DIM), F32),
                        pltpu.VMEM((2, tq, HEAD_DIM), F32),
                        pltpu.VMEM((2, tq, HEAD_DIM), F32),
                        pltpu.VMEM((rows_s, HEAD_DIM), F32),
                        pltpu.VMEM((rows_s, HEAD_DIM), F32),
                        pltpu.VMEM((rows_s, HEAD_DIM), F32),
                        pltpu.VMEM((page, C_W), F32),
                        pltpu.VMEM((page, C_W), F32)])
    return pl.pallas_call(
        functools.partial(_attn_kernel, n_alias=len(aliases), pages=pages, nb=nb, nq=nq, skip=skip, steps_s=steps_s,
                          chunks_s=chunks_s, tq=tq, ts=ts, page=page, lam_init=lam_init),
        grid_spec=grid_spec,
        out_shape=[jax.ShapeDtypeStruct((m, C_W), BF16), kv_shape, kv_shape,
                   jax.ShapeDtypeStruct((nb_s * ts, C_W), F32)],
        input_output_aliases=aliases,
        compiler_params=_cparams("arbitrary", "arbitrary", "arbitrary"), name="attention")(page_table, *args)


def _sample_attn_step(s_id, nsteps, tbl_ref, q_ref, kn_ref, vn_ref, lamp_ref, nw_ref, k_refs, v_refs, o_ref,
                      m_s, l_s, acc_s, kn_s, vn_s, *, ts, page, lam_init):
    width = len(k_refs) * page
    rows = 2 * ts

    @pl.when(s_id == 0)
    def _():
        m_s[...] = jnp.full(m_s.shape, NEG_INF, F32)
        l_s[...] = jnp.zeros(l_s.shape, F32)
        acc_s[...] = jnp.zeros(acc_s.shape, F32)

    q = q_ref[...] * (C_HALF ** -0.5 * LOG2E)
    qq = []
    for h in range(C_HEADS):
        q1, q2 = _split_halves(q[:, h * HEAD_DIM:(h + 1) * HEAD_DIM])
        qq.append(jnp.concatenate([q1, q2], axis=0).astype(BF16))

    def update(keys, vals, bias):
        s = jnp.concatenate([_dot_nt(qq[h], keys(h)) for h in range(C_HEADS)], axis=0)
        if bias is not None:
            s = s + bias

        def pv(p):
            return jnp.concatenate([_dot(p[h * rows:(h + 1) * rows], vals(h)) for h in range(C_HEADS)], axis=0)

        _softmax_update(m_s, l_s, acc_s, slice(None), s, pv)

    def past_keys(h):
        return jnp.concatenate([r[h].astype(BF16) for r in k_refs], axis=0)

    def past_vals(h):
        return jnp.concatenate([r[h].astype(BF16) for r in v_refs], axis=0)

    @pl.when(s_id < nsteps - 1)
    def _():
        update(past_keys, past_vals, None)

    @pl.when(s_id == nsteps - 1)
    def _():
        tq_pos = lax.rem(lax.broadcasted_iota(jnp.int32, (rows, width), 0), ts)
        rel_past = tq_pos + width - lax.broadcasted_iota(jnp.int32, (rows, width), 1)
        rel_new = (lax.rem(lax.broadcasted_iota(jnp.int32, (rows, page), 0), ts)
                   - lax.broadcasted_iota(jnp.int32, (rows, page), 1))
        bias_past, bias_new = [], []
        for h in range(C_HEADS):
            far_bias = tbl_ref[(N_BUCKETS - 1) * C_HEADS + h]
            bias_past.append((_bias_from_rel(rel_past, tbl_ref, h) - far_bias) * LOG2E)
            bias_new.append(jnp.where(rel_new >= 0, (_bias_from_rel(rel_new, tbl_ref, h) - far_bias) * LOG2E, NEG_INF))
        update(past_keys, past_vals, jnp.concatenate(bias_past, axis=0))
        kn_s[...] = jnp.zeros(kn_s.shape, F32)
        vn_s[...] = jnp.zeros(vn_s.shape, F32)
        kn_s[0:ts, :] = kn_ref[...]
        vn_s[0:ts, :] = vn_ref[...]
        update(lambda h: kn_s[:, h * HEAD_DIM:(h + 1) * HEAD_DIM].astype(BF16),
               lambda h: vn_s[:, h * HEAD_DIM:(h + 1) * HEAD_DIM].astype(BF16),
               jnp.concatenate(bias_new, axis=0))
        lam = _lambda(lamp_ref, lam_init)
        on = acc_s[...] / l_s[...]
        for h in range(C_HEADS):
            o = on[h * rows:h * rows + ts] - lam * on[h * rows + ts:(h + 1) * rows]
            o_ref[:, h * HEAD_DIM:(h + 1) * HEAD_DIM] = (_rms(o, nw_ref[...]) * (1.0 - lam_init)).astype(o_ref.dtype)


def _outproj_rows(ob_ref, oc_ref, oa_ref, x_ref, wb_ref, wc_ref, wa_ref, nw_ref, xo_ref, ho_ref):
    acc = _dot(ob_ref[...], wb_ref[...]) + _dot(oc_ref[...], wc_ref[...]) + _dot(oa_ref[...], wa_ref[...])
    x = x_ref[...] + acc
    xo_ref[...] = x
    ho_ref[...] = _rms(x, nw_ref[...]).astype(ho_ref.dtype)


def _outproj_kernel(ob_ref, oc_ref, oa_ref, x_ref, obs_ref, ocs_ref, oas_ref, xs_ref, wb_ref, wc_ref, wa_ref,
                    nw_ref, xo_ref, ho_ref, xos_ref, hos_ref):
    weights = (wb_ref, wc_ref, wa_ref, nw_ref)
    _outproj_rows(ob_ref, oc_ref, oa_ref, x_ref, *weights, xo_ref, ho_ref)

    @pl.when(pl.program_id(0) == pl.num_programs(0) - 1)
    def _():
        _outproj_rows(obs_ref, ocs_ref, oas_ref, xs_ref, *weights, xos_ref, hos_ref)


def _outproj(mix, x, mix_s, xs, w, nw, l, tm):
    m, d = x.shape
    ms = xs.shape[0]
    rows = lambda r, width: pl.BlockSpec((r, width), lambda i: (i, 0))
    once = lambda r, width: pl.BlockSpec((r, width), lambda i: (0, 0))
    return pl.pallas_call(
        _outproj_kernel, grid=(m // tm,),
        in_specs=[rows(tm, B_W), rows(tm, C_W), rows(tm, A_W), rows(tm, d),
                  once(ms, B_W), once(ms, C_W), once(ms, A_W), once(ms, d),
                  pl.BlockSpec((None, B_W, d), lambda i: (l, 0, 0)),
                  pl.BlockSpec((None, C_W, d), lambda i: (l, 1, 0)),
                  pl.BlockSpec((None, A_W, d), lambda i: (l, (B_W + C_W) // A_W, 0)),
                  pl.BlockSpec((None, 1, d), lambda i: (l, 0, 0))],
        out_specs=[rows(tm, d), rows(tm, d), once(ms, d), once(ms, d)],
        out_shape=[jax.ShapeDtypeStruct((m, d), F32), jax.ShapeDtypeStruct((m, d), BF16),
                   jax.ShapeDtypeStruct((ms, d), F32), jax.ShapeDtypeStruct((ms, d), BF16)],
        compiler_params=_cparams("arbitrary"), name="outproj")(*mix, x, *mix_s, xs, w, w, w, nw)


def _ffn_up_rows(h_ref, wg_s, wu_s, cw_ref, cb_ref, buf_s, o_ref, so_ref, *, rows, head, stride):
    hn = h_ref[...]
    g = _dot(hn, wg_s[...])
    u = _dot(hn, wu_s[...])
    buf_s[head:head + rows, :] = g
    y = cw_ref[FFN_CONV - 1:FFN_CONV, :] * g + cb_ref[...]
    for tap in range(FFN_CONV - 1):
        off = head - (FFN_CONV - 1 - tap) * stride
        y = y + cw_ref[tap:tap + 1, :] * buf_s[off:off + rows, :]
    o_ref[...] = (_gelu(y) * u).astype(o_ref.dtype)
    last = buf_s[rows:rows + head, :]
    so_ref[...] = last
    buf_s[0:head, :] = last


def _ffn_up_kernel(h_ref, wg_ref, wu_ref, cw_ref, cb_ref, st_ref, hs_ref, sts_ref, o_ref, so_ref, os_ref, sos_ref,
                   buf_s, wg_s, wu_s, bufs_s, *, tm, blocks, stride_s):
    i = pl.program_id(1)

    @pl.when(i == 0)
    def _():
        wg_s[...] = wg_ref[...].astype(BF16)
        wu_s[...] = wu_ref[...].astype(BF16)

    @pl.when(i % blocks == 0)
    def _():
        buf_s[0:st_ref.shape[0], :] = st_ref[...]

    _ffn_up_rows(h_ref, wg_s, wu_s, cw_ref, cb_ref, buf_s, o_ref, so_ref, rows=tm, head=st_ref.shape[0], stride=1)

    @pl.when(i == pl.num_programs(1) - 1)
    def _():
        bufs_s[0:sts_ref.shape[0], :] = sts_ref[...]
        _ffn_up_rows(hs_ref, wg_s, wu_s, cw_ref, cb_ref, bufs_s, os_ref, sos_ref, rows=hs_ref.shape[0],
                     head=sts_ref.shape[0], stride=stride_s)


def _ffn_up(hn, hn_s, wg, wu, cw, cb, state, state_s, l, *, tm, tn, blocks, stride_s):
    m, d = hn.shape
    ms = hn_s.shape[0]
    dff = wg.shape[-1]
    head, head_s = state.shape[1], state_s.shape[1]
    nseq = state.shape[0]
    return pl.pallas_call(
        functools.partial(_ffn_up_kernel, tm=tm, blocks=blocks, stride_s=stride_s),
        grid=(dff // tn, m // tm),
        in_specs=[pl.BlockSpec((tm, d), lambda j, i: (i, 0)),
                  pl.BlockSpec((None, d, tn), lambda j, i: (l, 0, j)),
                  pl.BlockSpec((None, d, tn), lambda j, i: (l, 0, j)),
                  pl.BlockSpec((None, FFN_CONV, tn), lambda j, i: (l, 0, j)),
                  pl.BlockSpec((None, 1, tn), lambda j, i: (l, 0, j)),
                  pl.BlockSpec((None, head, tn), lambda j, i: (i // blocks, 0, j)),
                  pl.BlockSpec((ms, d), lambda j, i: (0, 0)),
                  pl.BlockSpec((None, head_s, tn), lambda j, i: (0, 0, j))],
        out_specs=[pl.BlockSpec((tm, tn), lambda j, i: (i, j)),
                   pl.BlockSpec((None, head, tn), lambda j, i: (i // blocks, 0, j)),
                   pl.BlockSpec((ms, tn), lambda j, i: (0, j)),
                   pl.BlockSpec((None, head_s, tn), lambda j, i: (0, 0, j))],
        out_shape=[jax.ShapeDtypeStruct((m, dff), BF16), jax.ShapeDtypeStruct((nseq, head, dff), F32),
                   jax.ShapeDtypeStruct((ms, dff), BF16), jax.ShapeDtypeStruct((1, head_s, dff), F32)],
        scratch_shapes=[pltpu.VMEM((tm + head, tn), F32), pltpu.VMEM((d, tn), BF16), pltpu.VMEM((d, tn), BF16),
                        pltpu.VMEM((ms + head_s, tn), F32)],
        compiler_params=_cparams("arbitrary", "arbitrary"), name="ffn_up")(
            hn, wg, wu, cw, cb, state, hn_s, state_s)


def _ffn_down_kernel(h_ref, x_ref, hs_ref, xs_ref, w_ref, o_ref, os_ref):
    o_ref[...] = x_ref[...] + _dot(h_ref[...], w_ref[...])

    @pl.when(pl.program_id(0) == pl.num_programs(0) - 1)
    def _():
        os_ref[...] = xs_ref[...] + _dot(hs_ref[...], w_ref[...])


def _ffn_down(hid, x, hid_s, xs, w, l, tm, tn):
    m, dff = hid.shape
    ms = hid_s.shape[0]
    d = x.shape[1]
    ni = m // tm
    s_tile = pl.BlockSpec((ms, tn), lambda i, j: (0, jnp.where(i == ni - 1, j, 0)))
    return pl.pallas_call(
        _ffn_down_kernel, grid=(ni, d // tn),
        in_specs=[pl.BlockSpec((tm, dff), lambda i, j: (i, 0)),
                  pl.BlockSpec((tm, tn), lambda i, j: (i, j)),
                  pl.BlockSpec((ms, dff), lambda i, j: (0, 0)),
                  s_tile,
                  pl.BlockSpec((None, dff, tn), lambda i, j: (l, 0, j))],
        out_specs=[pl.BlockSpec((tm, tn), lambda i, j: (i, j)), s_tile],
        out_shape=[jax.ShapeDtypeStruct((m, d), F32), jax.ShapeDtypeStruct((ms, d), F32)],
        compiler_params=_cparams("arbitrary", "arbitrary"), name="ffn_down")(hid, x, hid_s, xs, w)


def _pick(total, want):
    t = min(total, want)
    while total % t:
        t //= 2
    return t


def kernel(x_prompt, x_sample, cache_k, cache_v, state_conv_b, state_delta, state_ffn_conv, page_table, attn_norm_w, w_in, a_ln_w, a_ln_b, a_ws, a_bs, b_conv_w, b_a_log, b_dt_bias, b_norm_w, c_lam_q1, c_lam_k1, c_lam_q2, c_lam_k2, c_norm_w, rel_bias, w_out, ffn_norm_w, w_gate, w_up, ffn_conv_w, ffn_conv_b, w_down, final_norm_w):
    bp, tp, d = x_prompt.shape
    bs, ts, _ = x_sample.shape
    depth = w_in.shape[0]
    dff = w_gate.shape[-1]
    n_pool, page = cache_k.shape[1], cache_k.shape[2]
    mp, ms = bp * tp, bs * ts

    o_au, o_av, o_qkv, o_z, o_beta = 0, A_W, 2 * A_W, 2 * A_W + 3 * B_W, 2 * A_W + 4 * B_W
    o_cq = o_beta + 2 * B_HEADS
    o_ck, o_cv = o_cq + C_W, o_cq + 2 * C_W
    w_main = jnp.concatenate([
        w_in[:, :, o_qkv:o_z], w_in[:, :, o_z:o_beta], w_in[:, :, o_cq:o_ck], w_in[:, :, o_beta:o_cq],
        jnp.zeros((depth, d, COL_AU - COL_BA - 2 * B_HEADS), w_in.dtype),
        w_in[:, :, o_au:o_qkv], w_in[:, :, o_ck:o_cv + C_W]], axis=-1).astype(BF16)
    w_out_r = jnp.concatenate([w_out[:, A_W:], w_out[:, :A_W]], axis=1).astype(BF16)
    wg, wu = w_gate, w_up
    wd = w_down.astype(BF16)
    attn_nw = attn_norm_w.reshape(depth, 1, d)
    ffn_nw = ffn_norm_w.reshape(depth, 1, d)
    lnw = a_ln_w.reshape(depth, 1, A_W)
    lnb = a_ln_b.reshape(depth, 1, A_W)
    bcol_p = jnp.swapaxes(a_bs, 1, 2)
    tril = jnp.tril(jnp.ones((ts, ts), F32))
    assert ms <= A_CHUNK
    wmix_s = jnp.einsum('lhts,bc->lhtbsc', a_ws[:, :, :ts, :ts] * tril, jnp.eye(bs, dtype=F32)).reshape(
        depth, A_HEADS, ms, ms)
    wmix_s = jnp.pad(wmix_s, ((0, 0), (0, 0), (0, A_CHUNK - ms), (0, A_CHUNK - ms)))
    bcol_s = jnp.repeat(jnp.swapaxes(a_bs[:, :, :ts], 1, 2), bs, axis=1)
    pad_lanes = jnp.zeros((depth, 1, HEAD_DIM - 2 * B_HEADS), F32)
    alog = jnp.concatenate([jnp.zeros((depth, 1, B_HEADS), F32), b_a_log.reshape(depth, 1, B_HEADS), pad_lanes], -1)
    dtb = jnp.concatenate([jnp.zeros((depth, 1, B_HEADS), F32), b_dt_bias.reshape(depth, 1, B_HEADS), pad_lanes], -1)
    b_nw = b_norm_w.reshape(depth, 1, HEAD_DIM)
    c_nw = c_norm_w.reshape(depth, 1, HEAD_DIM)
    lamp = jnp.stack([c_lam_q1, c_lam_k1, c_lam_q2, c_lam_k2], axis=1)
    tbl = rel_bias.reshape(-1)
    ffn_cb = ffn_conv_b.reshape(depth, 1, dff)
    ck_flat = jnp.swapaxes(cache_k, 2, 3).reshape(depth * n_pool, C_HEADS, page, HEAD_DIM)
    cv_flat = jnp.swapaxes(cache_v, 2, 3).reshape(depth * n_pool, C_HEADS, page, HEAD_DIM)
    pt_flat = page_table.reshape(-1)

    conv_head = SUBLANES
    zero_conv_p = jnp.zeros((bp, conv_head, 3 * B_W), F32)
    zero_delta_p = jnp.zeros((bp, B_HEADS, HEAD_DIM, HEAD_DIM), F32)
    zero_ffn_p = jnp.zeros((bp, SUBLANES, dff), F32)
    ffn_head_s = max(SUBLANES, (FFN_CONV - 1) * bs)

    tm_in = _pick(mp, TM_IN)
    tm_out = _pick(mp, TM_OUT)
    tm_up = _pick(tp, TM_UP)
    tm_down = _pick(mp, TM_DOWN)
    tb_delta = _pick(tp, TB_DELTA)
    tq = _pick(tp, TQ)
    ts_pad_delta = DN_PAIR

    xp = x_prompt.reshape(mp, d)
    xs = jnp.swapaxes(x_sample, 0, 1).reshape(ms, d)

    def to_bm(a):
        return jnp.swapaxes(a.reshape(ts, bs, -1), 0, 1)

    def to_tm(a):
        return jnp.swapaxes(a, 0, 1).reshape(ms, -1)

    kv_stacks = None
    p_conv, p_delta, p_ffn = [], [], []
    s_k, s_v, s_conv, s_delta, s_ffn, s_av = [], [], [], [], [], []
    for l in range(depth):
        lam_init = 0.8 - 0.6 * math.exp(-0.3 * l)

        main, main_s = _inproj(xp, xs, attn_nw, w_main, l, tm_in, TN)
        main_bm = to_bm(main_s)
        q_bm = main_bm[:, :, COL_CQ:COL_CQ + C_W].reshape(ms, C_W)
        kn_bm = main_bm[:, :, COL_CK:COL_CK + C_W]
        vn_bm = main_bm[:, :, COL_CV:COL_CV + C_W]
        out_c, k_stack, v_stack, out_c_s = _attention(
            main, q_bm, kn_bm.reshape(ms, C_W), vn_bm.reshape(ms, C_W), ck_flat, cv_flat, pt_flat, tbl, lamp, c_nw,
            l, kv_stacks, depth=depth, nb=bp, t_total=tp, tq=tq, nb_s=bs, ts=ts, n_pool=n_pool, lam_init=lam_init)
        kv_stacks = (k_stack, v_stack)

        (out_a,) = _gmlp(main, lnw, lnb, a_ws, bcol_p, l, rows=4 * A_CHUNK, chunk=A_CHUNK, causal_mask=True,
                         want_v=False)
        out_b, delta_new = _delta(main, zero_conv_p, b_conv_w, alog, dtb, b_nw, zero_delta_p, l,
                                  nb=bp, t_total=tp, t_valid=tp, tb=tb_delta)
        main3 = main.reshape(bp, tp, N_MAIN)
        p_conv.append(main3[:, tp - (B_CONV - 1):, COL_BQ:COL_BQ + 3 * B_W])
        p_delta.append(delta_new)

        out_a_s, va_s = _gmlp(main_s, lnw, lnb, wmix_s, bcol_s, l, rows=ms, chunk=A_CHUNK, causal_mask=False,
                              want_v=True)
        main_pad = jnp.pad(main_bm, ((0, 0), (0, ts_pad_delta - ts), (0, 0))).reshape(bs * ts_pad_delta, N_MAIN)
        conv_s = jnp.pad(state_conv_b[l], ((0, 0), (conv_head - (B_CONV - 1), 0), (0, 0)))
        out_b_s, delta_new_s = _delta(main_pad, conv_s, b_conv_w, alog, dtb, b_nw, state_delta[l], l,
                                      nb=bs, t_total=ts_pad_delta, t_valid=ts, tb=ts_pad_delta)
        out_b_s = to_tm(out_b_s.reshape(bs, ts_pad_delta, B_W)[:, :ts])
        out_c_s = to_tm(out_c_s.reshape(bs, ts, C_W)).astype(BF16)

        xp, hn2, xs, hn2_s = _outproj((out_b, out_c, out_a), xp, (out_b_s, out_c_s, out_a_s), xs, w_out_r, ffn_nw,
                                      l, tm_out)
        ffn_state_s = jnp.swapaxes(state_ffn_conv[l], 0, 1).reshape(1, (FFN_CONV - 1) * bs, dff)
        ffn_state_s = jnp.pad(ffn_state_s, ((0, 0), (ffn_head_s - (FFN_CONV - 1) * bs, 0), (0, 0)))

        hid, ffn_last, hid_s, ffn_last_s = _ffn_up(hn2, hn2_s, wg, wu, ffn_conv_w, ffn_cb, zero_ffn_p, ffn_state_s,
                                                   l, tm=tm_up, tn=TN, blocks=tp // tm_up, stride_s=bs)
        xp, xs = _ffn_down(hid, xp, hid_s, xs, wd, l, tm_down, TN)
        p_ffn.append(ffn_last[:, SUBLANES - (FFN_CONV - 1):])
        s_k.append(kn_bm.reshape(bs, ts, C_HEADS, HEAD_DIM))
        s_v.append(vn_bm.reshape(bs, ts, C_HEADS, HEAD_DIM))
        s_conv.append(main_bm[:, ts - (B_CONV - 1):, COL_BQ:COL_BQ + 3 * B_W])
        s_delta.append(delta_new_s)
        s_ffn.append(jnp.swapaxes(ffn_last_s[0, ffn_head_s - (FFN_CONV - 1) * bs:].reshape(FFN_CONV - 1, bs, dff),
                                  0, 1))
        s_av.append(to_bm(va_s).reshape(bs, ts, A_HEADS, HEAD_DIM))

    y_prompt = _rmsnorm(xp, final_norm_w, _pick(mp, 512)).reshape(bp, tp, d)
    y_sample = jnp.swapaxes(_rmsnorm(xs, final_norm_w, ms).reshape(ts, bs, d), 0, 1)
    k_prompt, v_prompt = (jnp.swapaxes(a, 2, 3) for a in kv_stacks)
    return (y_prompt, y_sample,
            k_prompt, v_prompt, jnp.stack(p_conv), jnp.stack(p_delta), jnp.stack(p_ffn),
            jnp.stack(s_k), jnp.stack(s_v), jnp.stack(s_conv), jnp.stack(s_delta), jnp.stack(s_ffn),
            jnp.stack(s_av))
```
